```python
import jax
import jax.numpy as jnp
from jax import lax
import numpy as np

D_MODEL = 4096
BATCH = 2
SEQ = 4096
DEPTH = 1
DEC_BATCH = 16
DEC_SEQ = 64
PAST_LEN = 4096

CHUNK = 64
REC_BLOCK = CHUNK
MIX_WIDTH = D_MODEL
HG_WIDTH = MIX_WIDTH // 2
HG_DK = 128
HG_HEADS = HG_WIDTH // HG_DK
HG_DV = HG_WIDTH // HG_HEADS
ML_WIDTH = MIX_WIDTH - HG_WIDTH
ML_HEADS = 4
ML_DV = ML_WIDTH // ML_HEADS
ML_DQK = ML_DV // 2
CONV_K = 4
N_EXPERTS = 32
TOP_K = 4
D_FF = D_MODEL
SWIGLU_LIMIT = 7.0
SWIGLU_ALPHA = 1.702
MOE_BLOCK = 256
RMS_EPS = 1e-6
IN_SPLITS = (HG_HEADS * HG_DK, HG_HEADS * HG_DK, HG_WIDTH, HG_WIDTH,
             2 * ML_HEADS * ML_DQK, ML_WIDTH, ML_WIDTH, ML_HEADS, ML_HEADS)
IN_WIDTH = sum(IN_SPLITS)

kernel_name = 'hymba_hgrn2_mlstm_moe_stream_step'


def rms_norm(x, g):
    xf = x.astype(jnp.float32)
    y = xf * lax.rsqrt(jnp.mean(xf * xf, axis=-1, keepdims=True) + RMS_EPS)
    return (y * g.astype(jnp.float32)).astype(x.dtype)


def split_cols(z, sizes):
    points = np.cumsum(np.array(sizes))[:-1].tolist()
    return jnp.split(z, points, axis=-1)


def block_len(t):
    return REC_BLOCK if t % REC_BLOCK == 0 else t


def to_blocks(a, L):
    b, t = a.shape[:2]
    return jnp.moveaxis(a.reshape((b, t // L, L) + a.shape[2:]), 1, 0)


def from_blocks(a):
    nb, b, L = a.shape[:3]
    return jnp.moveaxis(a, 0, 1).reshape((b, nb * L) + a.shape[3:])


def causal_conv(u, buf, w, bias):
    t = u.shape[1]
    full = jnp.concatenate([buf.astype(u.dtype), u], axis=1)
    y = bias + sum(full[:, j:j + t] * w[j] for j in range(CONV_K))
    return y, full[:, t:]


def hgrn2_recurrence(q, k, v, lf, S0):
    L = block_len(q.shape[1])
    mask = jnp.tril(jnp.ones((L, L), bool))

    def step(S, blk):
        qb, kb, vb, lfb = blk
        b = jnp.cumsum(lfb, axis=1)
        dec = jnp.where(mask[None, :, :, None, None], b[:, :, None] - b[:, None, :], -jnp.inf)
        A = jnp.einsum('bthd,btshd,bshd->bhts', qb, jnp.exp(dec), kb)
        o = (jnp.einsum('bhts,bshv->bthv', A, vb)
             + jnp.einsum('bthd,bhdv->bthv', qb * jnp.exp(b), S))
        b_last = b[:, -1]
        S_new = (jnp.exp(b_last)[..., None] * S
                 + jnp.einsum('bshd,bshv->bhdv', kb * jnp.exp(b_last[:, None] - b), vb))
        return S_new, o

    S, o = lax.scan(step, S0, (to_blocks(q, L), to_blocks(k, L), to_blocks(v, L), to_blocks(lf, L)))
    return from_blocks(o), S


def mlstm_recurrence(q, k, v, ig, lf, C0, n0, m0):
    L = block_len(q.shape[1])
    mask = jnp.tril(jnp.ones((L, L), bool))

    def step(carry, blk):
        C, n, m = carry
        qb, kb, vb, ib, fb = blk
        b = jnp.cumsum(fb, axis=1)
        logD = jnp.where(mask[None, :, :, None],
                         b[:, :, None, :] - b[:, None, :, :] + ib[:, None, :, :], -jnp.inf)
        log_inter = b + m[:, None, :]
        m_t = jnp.maximum(jnp.max(logD, axis=2), log_inter)
        s = jnp.einsum('bthd,bshd->btsh', qb, kb) * jnp.exp(logD - m_t[:, :, None, :])
        w_inter = jnp.exp(log_inter - m_t)
        num = (jnp.einsum('btsh,bshv->bthv', s, vb)
               + w_inter[..., None] * jnp.einsum('bthd,bhdv->bthv', qb, C))
        den = jnp.sum(s, axis=2) + w_inter * jnp.einsum('bthd,bhd->bth', qb, n)
        h = num / jnp.maximum(jnp.abs(den), jnp.exp(-m_t))[..., None]
        b_last = b[:, -1]
        log_w = b_last[:, None] - b + ib
        m_new = jnp.maximum(b_last + m, jnp.max(log_w, axis=1))
        wk = jnp.exp(log_w - m_new[:, None])
        decay = jnp.exp(b_last + m - m_new)
        C_new = decay[..., None, None] * C + jnp.einsum('bsh,bshd,bshv->bhdv', wk, kb, vb)
        n_new = decay[..., None] * n + jnp.einsum('bsh,bshd->bhd', wk, kb)
        return (C_new, n_new, m_new), h

    (C, n, m), h = lax.scan(step, (C0, n0, m0),
                            (to_blocks(q, L), to_blocks(k, L), to_blocks(v, L),
                             to_blocks(ig, L), to_blocks(lf, L)))
    return from_blocks(h), C, n, m


def moe_ffn(h, w_router, b_router, w_gate_up, b_gate_up, w_down, b_down):
    t, d = h.shape
    logits = (h @ w_router + b_router).astype(jnp.float32)
    top_logit, top_e = lax.top_k(logits, TOP_K)
    gates = jax.nn.softmax(top_logit, axis=-1)
    n_assign = t * TOP_K
    flat_e = top_e.reshape(n_assign)
    order = jnp.argsort(flat_e)
    sorted_e = flat_e[order]
    sorted_tok = (order // TOP_K).astype(jnp.int32)
    sorted_gate = gates.reshape(n_assign)[order]
    counts = jnp.zeros((N_EXPERTS,), jnp.int32).at[flat_e].add(1)
    padded = (counts + MOE_BLOCK - 1) // MOE_BLOCK * MOE_BLOCK
    start = jnp.cumsum(counts) - counts
    pad_end = jnp.cumsum(padded)
    pad_start = pad_end - padded
    dest = pad_start[sorted_e] + jnp.arange(n_assign, dtype=jnp.int32) - start[sorted_e]
    n_blocks = -(-n_assign // MOE_BLOCK) + N_EXPERTS
    n_rows = n_blocks * MOE_BLOCK
    row_tok = jnp.full((n_rows,), t, jnp.int32).at[dest].set(sorted_tok)
    row_gate = jnp.zeros((n_rows,), jnp.float32).at[dest].set(sorted_gate)
    block_e = jnp.minimum(jnp.searchsorted(pad_end, jnp.arange(n_blocks) * MOE_BLOCK, side='right'),
                          N_EXPERTS - 1)
    h_pad = jnp.concatenate([h, jnp.zeros((1, d), h.dtype)], axis=0)
    xb = h_pad[row_tok].reshape(n_blocks, MOE_BLOCK, d)

    def expert_block(args):
        xe, e = args
        gu = xe @ w_gate_up[e] + b_gate_up[e]
        gate = jnp.minimum(gu[:, 0::2], SWIGLU_LIMIT)
        up = jnp.clip(gu[:, 1::2], -SWIGLU_LIMIT, SWIGLU_LIMIT)
        act = (up + 1) * gate * jax.nn.sigmoid(SWIGLU_ALPHA * gate)
        return act @ w_down[e] + b_down[e]

    yb = lax.map(expert_block, (xb, block_e)).reshape(n_rows, d)
    y = jax.ops.segment_sum(yb.astype(jnp.float32) * row_gate[:, None], row_tok, num_segments=t + 1)[:t]
    return y.astype(h.dtype)


def trunk_layer(x, c, states, lb, lw):
    (w_ada, b_ada, g_mix, g_ffn, w_in, conv_w, conv_b, b_igate, b_fgate, g_hgrn, g_mlstm,
     w_out, w_router, b_router, w_gate_up, b_gate_up, w_down, b_down) = lw
    hg_S, conv_buf, ml_C, ml_n, ml_m = states
    f32 = jnp.float32
    B, T, D = x.shape
    mod = jnp.einsum('bd,de->be', jax.nn.silu(c), w_ada) + b_ada
    shift1, scale1, gate1, shift2, scale2, gate2 = jnp.split(mod[:, None, :], 6, axis=-1)

    h = rms_norm(x, g_mix) * (1 + scale1) + shift1
    z = jnp.einsum('btd,de->bte', h, w_in)
    hq, hf, hi, hg, mqk, mv, mo, mi, mf = split_cols(z, IN_SPLITS)

    q_h = (jax.nn.silu(hq.astype(f32)) * HG_DK ** -0.5).reshape(B, T, HG_HEADS, HG_DK)
    f_h = lb + (1 - lb) * jax.nn.sigmoid(hf.astype(f32))
    lf_h = jnp.log(f_h).reshape(B, T, HG_HEADS, HG_DK)
    k_h = (1 - f_h).reshape(B, T, HG_HEADS, HG_DK)
    v_h = hi.astype(f32).reshape(B, T, HG_HEADS, HG_DV)
    o_h, hg_S_new = hgrn2_recurrence(q_h, k_h, v_h, lf_h, hg_S.astype(f32))
    o_h = rms_norm(o_h.reshape(B, T, HG_WIDTH), g_hgrn) * jax.nn.silu(hg.astype(f32))

    qk_c, conv_new = causal_conv(mqk, conv_buf, conv_w, conv_b)
    qk_c = jax.nn.silu(qk_c.astype(f32))
    q_m = qk_c[..., :ML_HEADS * ML_DQK].reshape(B, T, ML_HEADS, ML_DQK)
    k_m = qk_c[..., ML_HEADS * ML_DQK:].reshape(B, T, ML_HEADS, ML_DQK) * ML_DQK ** -0.5
    v_m = mv.astype(f32).reshape(B, T, ML_HEADS, ML_DV)
    ig = mi.astype(f32) + b_igate.astype(f32)
    lf_m = jax.nn.log_sigmoid(mf.astype(f32) + b_fgate.astype(f32))
    h_m, C_new, n_new, m_new = mlstm_recurrence(q_m, k_m, v_m, ig, lf_m,
                                                ml_C.astype(f32), ml_n.astype(f32), ml_m.astype(f32))
    h_m = (rms_norm(h_m, g_mlstm.reshape(ML_HEADS, ML_DV))
           * jax.nn.sigmoid(mo.astype(f32)).reshape(B, T, ML_HEADS, ML_DV)).reshape(B, T, ML_WIDTH)

    mix = jnp.einsum('bte,ed->btd', jnp.concatenate([o_h, h_m], axis=-1).astype(x.dtype), w_out)
    x = x + gate1 * mix

    h2 = rms_norm(x, g_ffn) * (1 + scale2) + shift2
    ff = moe_ffn(h2.reshape(B * T, D), w_router, b_router, w_gate_up, b_gate_up, w_down, b_down)
    x = x + gate2 * ff.reshape(B, T, D)
    return x, (hg_S_new, conv_new, C_new, n_new, m_new)


def run_trunk(x, c, states, weights, lb_all, g_final):
    new = ([], [], [], [], [])
    for l in range(DEPTH):
        layer_states = tuple(s[l] for s in states)
        layer_w = tuple(w[l] for w in weights)
        x, layer_new = trunk_layer(x, c, layer_states, lb_all[l], layer_w)
        for acc, s_new, s_old in zip(new, layer_new, states):
            acc.append(s_new.astype(s_old.dtype))
    y = rms_norm(x, g_final)
    return y, tuple(jnp.stack(a, axis=0) for a in new)


def setup_inputs(seed: int = 0) -> dict:
    key = jax.random.key(seed)
    ks = jax.random.split(key, 32)
    f32 = jnp.float32
    D = D_MODEL
    QK2 = 2 * ML_HEADS * ML_DQK

    def nrm(k, shape, s):
        return jax.random.normal(k, shape, f32) * s

    return {
        'x_prompt': nrm(ks[0], (BATCH, SEQ, D), 1.0),
        'x_sample': nrm(ks[1], (DEC_BATCH, DEC_SEQ, D), 1.0),
        'c_prompt': nrm(ks[2], (BATCH, D), 1.0),
        'c_sample': nrm(ks[3], (DEC_BATCH, D), 1.0),
        'state_hgrn_S': nrm(ks[4], (DEPTH, DEC_BATCH, HG_HEADS, HG_DK, HG_DV), 0.5),
        'state_conv': nrm(ks[5], (DEPTH, DEC_BATCH, CONV_K - 1, QK2), 1.0),
        'state_mlstm_C': nrm(ks[6], (DEPTH, DEC_BATCH, ML_HEADS, ML_DQK, ML_DV), 0.5),
        'state_mlstm_n': nrm(ks[7], (DEPTH, DEC_BATCH, ML_HEADS, ML_DQK), 1.0),
        'state_mlstm_m': nrm(ks[8], (DEPTH, DEC_BATCH, ML_HEADS), 1.0),
        'w_ada': nrm(ks[9], (DEPTH, D, 6 * D), 0.5 * D ** -0.5),
        'b_ada': nrm(ks[10], (DEPTH, 6 * D), 0.02),
        'g_mix': 1.0 + nrm(ks[11], (DEPTH, D), 0.02),
        'g_ffn': 1.0 + nrm(ks[12], (DEPTH, D), 0.02),
        'w_in': nrm(ks[13], (DEPTH, D, IN_WIDTH), D ** -0.5),
        'lb_logits': nrm(ks[14], (DEPTH + 1, HG_HEADS * HG_DK), 1.0),
        'conv_w': nrm(ks[15], (DEPTH, CONV_K, QK2), CONV_K ** -0.5),
        'conv_b': nrm(ks[16], (DEPTH, QK2), 0.02),
        'b_igate': nrm(ks[17], (DEPTH, ML_HEADS), 0.1),
        'b_fgate': 3.0 + nrm(ks[18], (DEPTH, ML_HEADS), 0.1),
        'g_hgrn': 1.0 + nrm(ks[19], (DEPTH, HG_WIDTH), 0.02),
        'g_mlstm': 1.0 + nrm(ks[20], (DEPTH, ML_WIDTH), 0.02),
        'w_out': nrm(ks[21], (DEPTH, MIX_WIDTH, D), MIX_WIDTH ** -0.5),
        'w_router': nrm(ks[22], (DEPTH, D, N_EXPERTS), D ** -0.5),
        'b_router': nrm(ks[23], (DEPTH, N_EXPERTS), 0.01),
        'w_gate_up': nrm(ks[24], (DEPTH, N_EXPERTS, D, 2 * D_FF), D ** -0.5),
        'b_gate_up': nrm(ks[25], (DEPTH, N_EXPERTS, 2 * D_FF), 0.02),
        'w_down': nrm(ks[26], (DEPTH, N_EXPERTS, D_FF, D), D_FF ** -0.5),
        'b_down': nrm(ks[27], (DEPTH, N_EXPERTS, D), 0.02),
        'g_final': 1.0 + nrm(ks[28], (D,), 0.02),
    }


def reference(x_prompt, x_sample, c_prompt, c_sample, state_hgrn_S, state_conv, state_mlstm_C,
              state_mlstm_n, state_mlstm_m, w_ada, b_ada, g_mix, g_ffn, w_in, lb_logits, conv_w, conv_b,
              b_igate, b_fgate, g_hgrn, g_mlstm, w_out, w_router, b_router, w_gate_up, b_gate_up,
              w_down, b_down, g_final):
    lb_all = jnp.cumsum(jax.nn.softmax(lb_logits.astype(jnp.float32), axis=0), axis=0)
    weights = (w_ada, b_ada, g_mix, g_ffn, w_in, conv_w, conv_b, b_igate, b_fgate, g_hgrn, g_mlstm,
               w_out, w_router, b_router, w_gate_up, b_gate_up, w_down, b_down)
    bp = x_prompt.shape[0]
    dt = x_prompt.dtype
    zero_states = (jnp.zeros((DEPTH, bp, HG_HEADS, HG_DK, HG_DV), dt),
                   jnp.zeros((DEPTH, bp, CONV_K - 1, 2 * ML_HEADS * ML_DQK), dt),
                   jnp.zeros((DEPTH, bp, ML_HEADS, ML_DQK, ML_DV), dt),
                   jnp.zeros((DEPTH, bp, ML_HEADS, ML_DQK), dt),
                   jnp.zeros((DEPTH, bp, ML_HEADS), dt))
    y_prompt, (p_S, p_conv, p_C, p_n, p_m) = run_trunk(x_prompt, c_prompt, zero_states, weights, lb_all, g_final)
    sample_states = (state_hgrn_S, state_conv, state_mlstm_C, state_mlstm_n, state_mlstm_m)
    y_sample, (s_S, s_conv, s_C, s_n, s_m) = run_trunk(x_sample, c_sample, sample_states, weights, lb_all, g_final)
    return (y_prompt, y_sample, p_S, p_conv, p_C, p_n, p_m, s_S, s_conv, s_C, s_n, s_m)
```

```python
import functools

import numpy as np
import jax
import jax.numpy as jnp
from jax import lax
from jax.experimental import pallas as pl
from jax.experimental.pallas import tpu as pltpu

F32 = jnp.float32
BF16 = jnp.bfloat16

CHUNK = 64
TOP_K = 4
CONV_K = 4
RMS_EPS = 1e-6
SWIGLU_LIMIT = 7.0
SWIGLU_ALPHA = 1.702
MOE_BLOCK = 256
LANES = 128
GATE_LANES = 128
VMEM_LIMIT = 56 * 1024 * 1024
HGRN_LEVELS = (32, 16, 8, 4, 2, 1)


def _cparams(n_axes, vmem=VMEM_LIMIT):
    return pltpu.CompilerParams(dimension_semantics=("arbitrary",) * n_axes,
                                vmem_limit_bytes=vmem)


def _sigmoid(x):
    return 1.0 / (1.0 + jnp.exp(-x))


def _silu(x):
    return x * _sigmoid(x)


class _Rows:
    def __init__(self, bp, tp, bs, ts):
        assert tp % CHUNK == 0 and ts % CHUNK == 0
        self.bp, self.tp, self.bs, self.ts = bp, tp, bs, ts
        self.nblk_p, self.nblk_s = tp // CHUNK, ts // CHUNK
        self.groups_p = bp * self.nblk_p
        self.groups = self.groups_p + bs * self.nblk_s
        self.rows = self.groups * CHUNK
        self.nseq = bp + bs

    def seq_of_group(self, g):
        return jnp.where(g < self.groups_p, g // self.nblk_p,
                         self.bp + (g - self.groups_p) // self.nblk_s)

    def blk_in_seq(self, g):
        return jnp.where(g < self.groups_p, g % self.nblk_p, (g - self.groups_p) % self.nblk_s)

    def nblk_of_group(self, g):
        return jnp.where(g < self.groups_p, self.nblk_p, self.nblk_s)


def _ada_kernel(c_ref, w_ref, b_ref, o_ref):
    c = c_ref[...]
    s = _silu(c).astype(BF16)
    o_ref[...] = jnp.dot(s, w_ref[...].astype(BF16), preferred_element_type=F32) + b_ref[...]


def _ada(c_pad, w_ada, b_ada, tn=512):
    m, d = c_pad.shape
    n = w_ada.shape[1]
    tn = min(tn, n)
    assert n % tn == 0
    return pl.pallas_call(
        _ada_kernel,
        out_shape=jax.ShapeDtypeStruct((m, n), F32),
        grid=(n // tn,),
        in_specs=[pl.BlockSpec((m, d), lambda j: (0, 0)),
                  pl.BlockSpec((d, tn), lambda j: (0, j)),
                  pl.BlockSpec((1, tn), lambda j: (0, j))],
        out_specs=pl.BlockSpec((m, tn), lambda j: (0, j)),
        compiler_params=_cparams(1),
        name="ada_mod",
    )(c_pad, w_ada, b_ada.reshape(1, n))


def _prep_kernel(rows, tm, d, x_ref, g_ref, mod_ref, wg_ref, h_ref, gates_ref):
    i = pl.program_id(0)
    g = g_ref[...]
    for gi in range(tm // CHUNK):
        seq = rows.seq_of_group(i * (tm // CHUNK) + gi)
        m = mod_ref[pl.ds(seq, 1), :]
        shift, scale = m[:, 0:d], m[:, d:2 * d]
        x = x_ref[gi * CHUNK:(gi + 1) * CHUNK, :]
        ms = jnp.mean(x * x, axis=-1, keepdims=True)
        y = x * lax.rsqrt(ms + RMS_EPS) * g
        h = y * (1.0 + scale) + shift
        h_ref[gi * CHUNK:(gi + 1) * CHUNK, :] = h.astype(BF16)
        gates_ref[gi * CHUNK:(gi + 1) * CHUNK, :] = jnp.dot(
            h, wg_ref[...], precision=lax.Precision.HIGHEST, preferred_element_type=F32)


def _prep(rows, x_all, g_mix, mod, wg_pad, tm=256):
    r, d = x_all.shape
    tm = min(tm, r)
    assert r % tm == 0
    return pl.pallas_call(
        functools.partial(_prep_kernel, rows, tm, d),
        out_shape=(jax.ShapeDtypeStruct((r, d), BF16),
                   jax.ShapeDtypeStruct((r, GATE_LANES), F32)),
        grid=(r // tm,),
        in_specs=[pl.BlockSpec((tm, d), lambda i: (i, 0)),
                  pl.BlockSpec((1, d), lambda i: (0, 0)),
                  pl.BlockSpec(mod.shape, lambda i: (0, 0)),
                  pl.BlockSpec(wg_pad.shape, lambda i: (0, 0))],
        out_specs=(pl.BlockSpec((tm, d), lambda i: (i, 0)),
                   pl.BlockSpec((tm, GATE_LANES), lambda i: (i, 0))),
        compiler_params=_cparams(1),
        name="prep_norm_mod",
    )(x_all, g_mix.reshape(1, d), mod, wg_pad)


def _inproj_kernel(h_ref, w_ref, z_ref, wb_ref):
    @pl.when(pl.program_id(1) == 0)
    def _():
        wb_ref[...] = w_ref[...].astype(BF16)

    z_ref[...] = jnp.dot(h_ref[...], wb_ref[...], preferred_element_type=F32)


def _inproj(h_all, w_in, n_main, tm=1024, tn=512):
    r, d = h_all.shape
    tm, tn = min(tm, r), min(tn, n_main)
    assert r % tm == 0 and n_main % tn == 0
    return pl.pallas_call(
        _inproj_kernel,
        out_shape=jax.ShapeDtypeStruct((r, n_main), F32),
        grid=(n_main // tn, r // tm),
        in_specs=[pl.BlockSpec((tm, d), lambda j, i: (i, 0)),
                  pl.BlockSpec((d, tn), lambda j, i: (0, j))],
        out_specs=pl.BlockSpec((tm, tn), lambda j, i: (i, j)),
        scratch_shapes=[pltpu.VMEM((d, tn), BF16)],
        compiler_params=_cparams(2),
        name="in_proj",
    )(h_all, w_in)


def _hgrn_consts():
    L = CHUNK
    t = np.arange(L)[:, None]
    s = np.arange(L)[None, :]
    mats = [(s <= t)]
    masks = []
    for m in HGRN_LEVELS:
        start = (t // (2 * m)) * (2 * m)
        mats.append(s <= start + m - 1)
        masks.append((t // (2 * m) == s // (2 * m)) & (t % (2 * m) >= m) & (s % (2 * m) < m))
    mstack = np.concatenate(mats, axis=0).astype(np.float32)
    masks = np.stack(masks, axis=0).astype(np.float32)
    return jnp.asarray(mstack, BF16), jnp.asarray(masks, F32)


def _split3(x):
    x1 = x.astype(BF16)
    r1 = x - x1.astype(F32)
    x2 = r1.astype(BF16)
    x3 = (r1 - x2.astype(F32)).astype(BF16)
    return x1, x2, x3


def _hgrn_kernel(rows, nh, dk, dv, hq_ref, hf_ref, hi_ref, hg_ref, s0_ref, lb_ref, gn_ref,
                 mstack_ref, masks_ref, og_ref, st_ref, r_ref, o_ref):
    L = CHUNK
    g = pl.program_id(0)
    blk = rows.blk_in_seq(g)
    is_prompt = g < rows.groups_p

    @pl.when(jnp.logical_and(blk == 0, is_prompt))
    def _():
        st_ref[...] = jnp.zeros_like(st_ref)

    @pl.when(jnp.logical_and(blk == 0, jnp.logical_not(is_prompt)))
    def _():
        for h in range(nh):
            st_ref[0, h] = s0_ref[0, h].T

    lb = lb_ref[...]
    f = lb + (1.0 - lb) * _sigmoid(hf_ref[...])
    lf = jnp.log(f)
    mstack = mstack_ref[...]
    p1, p2, p3 = _split3(lf)
    r_ref[...] = (jnp.dot(mstack, p1, preferred_element_type=F32)
                  + jnp.dot(mstack, p2, preferred_element_type=F32)
                  + jnp.dot(mstack, p3, preferred_element_type=F32))

    for h in range(nh):
        cs = slice(h * dk, (h + 1) * dk)
        vs = slice(h * dv, (h + 1) * dv)
        q = _silu(hq_ref[:, cs]) * (dk ** -0.5)
        k = 1.0 - f[:, cs]
        v = hi_ref[:, vs]
        vb = v.astype(BF16)
        b = r_ref[0:L, cs]
        st = st_ref[0, h]
        qe = (q * jnp.exp(b)).astype(BF16)
        o = lax.dot_general(qe, st.astype(BF16), (((1,), (1,)), ((), ())),
                            preferred_element_type=F32)
        a = jnp.zeros((L, L), F32)
        for li in range(len(HGRN_LEVELS)):
            rl = r_ref[(li + 1) * L:(li + 2) * L, cs]
            e = jnp.exp(-jnp.abs(b - rl))
            al = lax.dot_general((q * e).astype(BF16), (k * e).astype(BF16),
                                 (((1,), (1,)), ((), ())), preferred_element_type=F32)
            a = a + al * masks_ref[li]
        diag = jnp.sum(q * k, axis=-1, keepdims=True)
        o = o + jnp.dot(a.astype(BF16), vb, preferred_element_type=F32) + diag * v
        o_ref[:, vs] = o
        b_last = b[L - 1:L, :]
        kd = (k * jnp.exp(b_last - b)).astype(BF16)
        st_ref[0, h] = st * jnp.exp(b_last) + lax.dot_general(
            vb, kd, (((0,), (0,)), ((), ())), preferred_element_type=F32)

    o = o_ref[...]
    ms = jnp.mean(o * o, axis=-1, keepdims=True)
    og = o * lax.rsqrt(ms + RMS_EPS) * gn_ref[...] * _silu(hg_ref[...])
    og_ref[...] = og.astype(BF16)


def _hgrn(rows, z, s0, lb, g_hgrn):
    bs, nh, dk, dv = s0.shape
    hw = nh * dk
    assert nh * dv == hw and hw % LANES == 0
    mstack, masks = _hgrn_consts()
    n_lv = len(HGRN_LEVELS)
    L = CHUNK

    def zspec(sec):
        return pl.BlockSpec((L, hw), lambda g, sec=sec: (g, sec))

    og, st = pl.pallas_call(
        functools.partial(_hgrn_kernel, rows, nh, dk, dv),
        out_shape=(jax.ShapeDtypeStruct((rows.rows, hw), BF16),
                   jax.ShapeDtypeStruct((rows.nseq, nh, dv, dk), F32)),
        grid=(rows.groups,),
        in_specs=[zspec(0), zspec(1), zspec(2), zspec(3),
                  pl.BlockSpec((1, nh, dk, dv),
                               lambda g: (jnp.maximum(rows.seq_of_group(g) - rows.bp, 0), 0, 0, 0)),
                  pl.BlockSpec((1, hw), lambda g: (0, 0)),
                  pl.BlockSpec((1, hw), lambda g: (0, 0)),
                  pl.BlockSpec(mstack.shape, lambda g: (0, 0)),
                  pl.BlockSpec(masks.shape, lambda g: (0, 0, 0))],
        out_specs=(pl.BlockSpec((L, hw), lambda g: (g, 0)),
                   pl.BlockSpec((1, nh, dv, dk), lambda g: (rows.seq_of_group(g), 0, 0, 0))),
        scratch_shapes=[pltpu.VMEM(((n_lv + 1) * L, hw), F32),
                        pltpu.VMEM((L, hw), F32)],
        compiler_params=_cparams(1),
        name="hgrn2",
    )(z, z, z, z, s0, lb.reshape(1, hw), g_hgrn.reshape(1, hw), mstack, masks)
    return og, st


def _log_sigmoid(x):
    return jnp.minimum(x, 0.0) - jnp.log(1.0 + jnp.exp(-jnp.abs(x)))


def _mlstm_kernel(rows, nh, dqk, dv, mqk_ref, mv_ref, mo_ref, gates_ref, conv0_ref, c0_ref,
                  n0_ref, m0_ref, cw_ref, cb_ref, gbias_ref, gn_ref,
                  hm_ref, conv_ref, c_ref, n_ref, m_ref, ubuf_ref):
    L = CHUNK
    kq = nh * dqk
    pad = 8
    g = pl.program_id(0)
    blk = rows.blk_in_seq(g)
    is_prompt = g < rows.groups_p

    @pl.when(jnp.logical_and(blk == 0, is_prompt))
    def _():
        ubuf_ref[0:pad, :] = jnp.zeros((pad, 2 * kq), F32)
        c_ref[...] = jnp.zeros_like(c_ref)
        n_ref[...] = jnp.zeros_like(n_ref)
        m_ref[...] = jnp.zeros_like(m_ref)

    @pl.when(jnp.logical_and(blk == 0, jnp.logical_not(is_prompt)))
    def _():
        ubuf_ref[0:pad, :] = jnp.zeros((pad, 2 * kq), F32)
        ubuf_ref[pad - (CONV_K - 1):pad, :] = conv0_ref[0]
        c_ref[...] = c0_ref[...]
        n_ref[...] = n0_ref[...]
        m_ref[...] = m0_ref[...]

    ubuf_ref[pad:pad + L, :] = mqk_ref[...]
    acc = cb_ref[...] + jnp.zeros((L, 2 * kq), F32)
    for j in range(CONV_K):
        off = pad - (CONV_K - 1) + j
        acc = acc + ubuf_ref[off:off + L, :] * cw_ref[j:j + 1, :]
    new_tail = ubuf_ref[pad + L - (CONV_K - 1):pad + L, :]
    conv_ref[0] = new_tail
    ubuf_ref[pad - (CONV_K - 1):pad, :] = new_tail
    qk = _silu(acc)

    gt = gates_ref[...] + gbias_ref[...]
    lane = lax.broadcasted_iota(jnp.int32, gt.shape, 1)
    pg = jnp.where(lane < nh, gt, _log_sigmoid(gt))
    pgt = pg.T
    ti = lax.broadcasted_iota(jnp.int32, (L, L), 0)
    si = lax.broadcasted_iota(jnp.int32, (L, L), 1)
    tri = si <= ti

    for h in range(nh):
        q = qk[:, h * dqk:(h + 1) * dqk]
        k = qk[:, kq + h * dqk:kq + (h + 1) * dqk] * (dqk ** -0.5)
        v = mv_ref[:, h * dv:(h + 1) * dv]
        qb, kb, vb = q.astype(BF16), k.astype(BF16), v.astype(BF16)
        ig_c, lf_c = pg[:, h:h + 1], pg[:, nh + h:nh + h + 1]
        ig_r, lf_r = pgt[h:h + 1, :], pgt[nh + h:nh + h + 1, :]
        b_c = jnp.sum(jnp.where(tri, lf_r, 0.0), axis=1, keepdims=True)
        b_r = jnp.sum(jnp.where(ti <= si, lf_c, 0.0), axis=0, keepdims=True)
        m_prev = m_ref[0, :, h:h + 1]
        log_d = jnp.where(tri, b_c - b_r + ig_r, -jnp.inf)
        log_inter = b_c + m_prev
        m_t = jnp.maximum(jnp.max(log_d, axis=1, keepdims=True), log_inter)
        dm = jnp.exp(log_d - m_t)
        s_mat = lax.dot_general(qb, kb, (((1,), (1,)), ((), ())), preferred_element_type=F32) * dm
        w_inter = jnp.exp(log_inter - m_t)
        c_h = c_ref[0, h]
        n_h = n_ref[0, h:h + 1, :]
        num = (jnp.dot(s_mat.astype(BF16), vb, preferred_element_type=F32)
               + w_inter * jnp.dot(qb, c_h.astype(BF16), preferred_element_type=F32))
        den = (jnp.sum(s_mat, axis=1, keepdims=True)
               + w_inter * jnp.sum(q * n_h, axis=1, keepdims=True))
        hh = num / jnp.maximum(jnp.abs(den), jnp.exp(-m_t))
        b_last = b_c[L - 1:L, :]
        lw_c = b_last - b_c + ig_c
        lw_r = b_last - b_r + ig_r
        m_new = jnp.maximum(b_last + m_prev, jnp.max(lw_r, axis=1, keepdims=True))
        decay = jnp.exp(b_last + m_prev - m_new)
        kw = k * jnp.exp(lw_c - m_new)
        c_ref[0, h] = decay * c_h + lax.dot_general(
            kw.astype(BF16), vb, (((0,), (0,)), ((), ())), preferred_element_type=F32)
        n_ref[0, h:h + 1, :] = decay * n_h + jnp.sum(kw, axis=0, keepdims=True)
        m_ref[0, :, h:h + 1] = m_new
        ms = jnp.mean(hh * hh, axis=-1, keepdims=True)
        hn = hh * lax.rsqrt(ms + RMS_EPS) * gn_ref[:, h * dv:(h + 1) * dv]
        hm_ref[:, h * dv:(h + 1) * dv] = (hn * _sigmoid(mo_ref[:, h * dv:(h + 1) * dv])).astype(BF16)


def _mlstm(rows, z, sec_qk, gates, conv0, c0, n0, m0, conv_w, conv_b, b_igate, b_fgate, g_mlstm):
    bs, nh, dqk, dv = c0.shape
    kq = nh * dqk
    mw = nh * dv
    L = CHUNK
    assert 2 * kq == mw, "q/k conv width must equal the value width for the column sections"
    gbias = jnp.zeros((1, GATE_LANES), F32)
    gbias = gbias.at[0, 0:nh].set(b_igate.astype(F32)).at[0, nh:2 * nh].set(b_fgate.astype(F32))

    def sseq(g):
        return jnp.maximum(rows.seq_of_group(g) - rows.bp, 0)

    outs = pl.pallas_call(
        functools.partial(_mlstm_kernel, rows, nh, dqk, dv),
        out_shape=(jax.ShapeDtypeStruct((rows.rows, mw), BF16),
                   jax.ShapeDtypeStruct((rows.nseq, CONV_K - 1, 2 * kq), F32),
                   jax.ShapeDtypeStruct((rows.nseq, nh, dqk, dv), F32),
                   jax.ShapeDtypeStruct((rows.nseq, nh, dqk), F32),
                   jax.ShapeDtypeStruct((rows.nseq, 1, nh), F32)),
        grid=(rows.groups,),
        in_specs=[pl.BlockSpec((L, mw), lambda g: (g, sec_qk)),
                  pl.BlockSpec((L, mw), lambda g: (g, sec_qk + 1)),
                  pl.BlockSpec((L, mw), lambda g: (g, sec_qk + 2)),
                  pl.BlockSpec((L, GATE_LANES), lambda g: (g, 0)),
                  pl.BlockSpec((1, CONV_K - 1, 2 * kq), lambda g: (sseq(g), 0, 0)),
                  pl.BlockSpec((1, nh, dqk, dv), lambda g: (sseq(g), 0, 0, 0)),
                  pl.BlockSpec((1, nh, dqk), lambda g: (sseq(g), 0, 0)),
                  pl.BlockSpec((1, 1, nh), lambda g: (sseq(g), 0, 0)),
                  pl.BlockSpec((CONV_K, 2 * kq), lambda g: (0, 0)),
                  pl.BlockSpec((1, 2 * kq), lambda g: (0, 0)),
                  pl.BlockSpec((1, GATE_LANES), lambda g: (0, 0)),
                  pl.BlockSpec((1, mw), lambda g: (0, 0))],
        out_specs=(pl.BlockSpec((L, mw), lambda g: (g, 0)),
                   pl.BlockSpec((1, CONV_K - 1, 2 * kq), lambda g: (rows.seq_of_group(g), 0, 0)),
                   pl.BlockSpec((1, nh, dqk, dv), lambda g: (rows.seq_of_group(g), 0, 0, 0)),
                   pl.BlockSpec((1, nh, dqk), lambda g: (rows.seq_of_group(g), 0, 0)),
                   pl.BlockSpec((1, 1, nh), lambda g: (rows.seq_of_group(g), 0, 0))),
        scratch_shapes=[pltpu.VMEM((8 + L, 2 * kq), F32)],
        compiler_params=_cparams(1),
        name="mlstm",
    )(z, z, z, gates, conv0, c0, n0, m0.reshape(bs, 1, nh), conv_w, conv_b.reshape(1, 2 * kq),
      gbias, g_mlstm.reshape(1, mw))
    return outs


def _outproj_kernel(rows, tm, d, hw, og_ref, hm_ref, w_ref, x_ref, mod_ref, o_ref, wb_ref):
    j, i = pl.program_id(0), pl.program_id(1)
    tn = o_ref.shape[1]

    @pl.when(i == 0)
    def _():
        wb_ref[...] = w_ref[...].astype(BF16)

    mix = (jnp.dot(og_ref[...], wb_ref[0:hw, :], preferred_element_type=F32)
           + jnp.dot(hm_ref[...], wb_ref[hw:, :], preferred_element_type=F32))
    for gi in range(tm // CHUNK):
        seq = rows.seq_of_group(i * (tm // CHUNK) + gi)
        gate = mod_ref[pl.ds(seq, 1), :]
        rs = slice(gi * CHUNK, (gi + 1) * CHUNK)
        o_ref[rs, :] = x_ref[rs, :] + gate * mix[rs, :]


def _outproj(rows, og, hm, w_out, x_all, mod, tm=1024, tn=512):
    r, d = x_all.shape
    hw = og.shape[1]
    tm, tn = min(tm, r), min(tn, d)
    assert r % tm == 0 and d % tn == 0
    gate_blk0 = 2 * d // tn
    return pl.pallas_call(
        functools.partial(_outproj_kernel, rows, tm, d, hw),
        out_shape=jax.ShapeDtypeStruct((r, d), F32),
        grid=(d // tn, r // tm),
        in_specs=[pl.BlockSpec((tm, hw), lambda j, i: (i, 0)),
                  pl.BlockSpec((tm, hm.shape[1]), lambda j, i: (i, 0)),
                  pl.BlockSpec((w_out.shape[0], tn), lambda j, i: (0, j)),
                  pl.BlockSpec((tm, tn), lambda j, i: (i, j)),
                  pl.BlockSpec((mod.shape[0], tn), lambda j, i: (0, gate_blk0 + j))],
        out_specs=pl.BlockSpec((tm, tn), lambda j, i: (i, j)),
        scratch_shapes=[pltpu.VMEM((w_out.shape[0], tn), BF16)],
        compiler_params=_cparams(2),
        name="out_proj",
    )(og, hm, w_out, x_all, mod)


def _router_kernel(rows, tm, d, ne, x_ref, g_ref, mod_ref, wr_ref, br_ref, ut_ref,
                   h2_ref, e_ref, gate_ref, rank_ref, cnt_ref, h2s_ref):
    i = pl.program_id(0)
    half = d // 2

    @pl.when(i == 0)
    def _():
        cnt_ref[...] = jnp.zeros_like(cnt_ref)

    g = g_ref[...]
    for gi in range(tm // CHUNK):
        seq = rows.seq_of_group(i * (tm // CHUNK) + gi)
        m = mod_ref[pl.ds(seq, 1), :]
        shift, scale = m[:, 3 * d:4 * d], m[:, 4 * d:5 * d]
        x = x_ref[gi * CHUNK:(gi + 1) * CHUNK, :]
        ms = jnp.mean(x * x, axis=-1, keepdims=True)
        h2s_ref[gi * CHUNK:(gi + 1) * CHUNK, :] = (x * lax.rsqrt(ms + RMS_EPS) * g) * (1.0 + scale) + shift
    h2 = h2s_ref[...]
    lo = pltpu.bitcast(h2[:, :half].astype(BF16).astype(F32), jnp.uint32)
    hi = pltpu.bitcast(h2[:, half:].astype(BF16).astype(F32), jnp.uint32)
    h2_ref[...] = (hi & jnp.uint32(0xFFFF0000)) | (lo >> 16)

    logits = jnp.dot(h2, wr_ref[...], precision=lax.Precision.HIGHEST,
                     preferred_element_type=F32)
    lt = logits.T[0:ne, :] + br_ref[...]
    eidx = lax.broadcasted_iota(jnp.int32, (ne, tm), 0)
    cur = lt
    tops, sels, hots = [], [], []
    for _ in range(TOP_K):
        mx = jnp.max(cur, axis=0, keepdims=True)
        sel = jnp.min(jnp.where(cur == mx, eidx, ne), axis=0, keepdims=True)
        hot = eidx == sel
        tops.append(mx)
        sels.append(sel)
        hots.append(hot)
        cur = jnp.where(hot, -jnp.inf, cur)
    ex = [jnp.exp(t - tops[0]) for t in tops]
    tot = ex[0] + ex[1] + ex[2] + ex[3]
    oh = jnp.zeros((ne, tm), F32)
    for hot in hots:
        oh = oh + hot.astype(F32)
    prefix = jnp.dot(oh.astype(BF16), ut_ref[...], preferred_element_type=F32)
    base = cnt_ref[:, 0:1]
    pos = base + prefix
    for kk in range(TOP_K):
        e_ref[kk:kk + 1, :] = sels[kk]
        gate_ref[kk:kk + 1, :] = ex[kk] / tot
        rank_ref[kk:kk + 1, :] = jnp.sum(jnp.where(hots[kk], pos, 0.0), axis=0,
                                         keepdims=True).astype(jnp.int32)
    cnt_ref[...] = cnt_ref[...] + jnp.sum(oh, axis=1, keepdims=True)


def _router(rows, x1, g_ffn, mod, w_router, b_router, tm=256):
    r, d = x1.shape
    ne = w_router.shape[1]
    tm = min(tm, r)
    assert r % tm == 0 and ne % 8 == 0 and ne <= LANES
    wr_pad = jnp.zeros((d, LANES), F32).at[:, :ne].set(w_router.astype(F32))
    tt = np.arange(tm)
    ut = jnp.asarray((tt[:, None] < tt[None, :]).astype(np.float32), BF16)
    return pl.pallas_call(
        functools.partial(_router_kernel, rows, tm, d, ne),
        out_shape=(jax.ShapeDtypeStruct((r, d // 2), jnp.uint32),
                   jax.ShapeDtypeStruct((TOP_K, r), jnp.int32),
                   jax.ShapeDtypeStruct((TOP_K, r), F32),
                   jax.ShapeDtypeStruct((TOP_K, r), jnp.int32),
                   jax.ShapeDtypeStruct((ne, LANES), F32)),
        grid=(r // tm,),
        in_specs=[pl.BlockSpec((tm, d), lambda i: (i, 0)),
                  pl.BlockSpec((1, d), lambda i: (0, 0)),
                  pl.BlockSpec(mod.shape, lambda i: (0, 0)),
                  pl.BlockSpec((d, LANES), lambda i: (0, 0)),
                  pl.BlockSpec((ne, 1), lambda i: (0, 0)),
                  pl.BlockSpec((tm, tm), lambda i: (0, 0))],
        out_specs=(pl.BlockSpec((tm, d // 2), lambda i: (i, 0)),
                   pl.BlockSpec((TOP_K, tm), lambda i: (0, i)),
                   pl.BlockSpec((TOP_K, tm), lambda i: (0, i)),
                   pl.BlockSpec((TOP_K, tm), lambda i: (0, i)),
                   pl.BlockSpec((ne, LANES), lambda i: (0, 0))),
        scratch_shapes=[pltpu.VMEM((tm, d), F32)],
        compiler_params=_cparams(1),
        name="router",
    )(x1, g_ffn.reshape(1, d), mod, wr_pad, b_router.reshape(ne, 1).astype(F32), ut)


def _dispatch_kernel(tm, ne, pstart_ref, npad_ref, e_ref, rank_ref, h2_ref, xs_ref, sem):
    i = pl.program_id(0)
    w = h2_ref.shape[1]

    def row_copy(src_row, dst_row):
        return pltpu.make_async_copy(h2_ref.at[pl.ds(src_row, 1), :],
                                     xs_ref.at[pl.ds(dst_row, 1), :], sem)

    def body(t, carry):
        for kk in range(TOP_K):
            dst = pstart_ref[e_ref[kk, t]] + rank_ref[kk, t]
            row_copy(i * tm + t, dst).start()
        return carry

    lax.fori_loop(0, tm, body, 0)

    def drain(t, carry):
        for kk in range(TOP_K):
            row_copy(0, 0).wait()
        return carry

    lax.fori_loop(0, tm, drain, 0)

    @pl.when(i == 0)
    def _():
        def fill(e, carry):
            npad = npad_ref[e]
            start = pstart_ref[e + 1] - npad

            def fill_start(p, c):
                row_copy(p, start + p).start()
                return c

            def fill_wait(p, c):
                row_copy(0, 0).wait()
                return c

            lax.fori_loop(0, npad, fill_start, 0)
            lax.fori_loop(0, npad, fill_wait, 0)
            return carry

        lax.fori_loop(0, ne, fill, 0)

        def tail(bk, carry):
            off = pl.multiple_of(bk * MOE_BLOCK, MOE_BLOCK)
            cp = pltpu.make_async_copy(h2_ref.at[pl.ds(0, MOE_BLOCK), :],
                                       xs_ref.at[pl.ds(off, MOE_BLOCK), :], sem)
            cp.start()
            cp.wait()
            return carry

        lax.fori_loop(pstart_ref[ne] // MOE_BLOCK, xs_ref.shape[0] // MOE_BLOCK, tail, 0)


def _dispatch(h2p, e_idx, rank, pstart, npad, n_rows, tm=256):
    r, w = h2p.shape
    ne = npad.shape[0]
    tm = min(tm, r)
    assert r % tm == 0 and r >= MOE_BLOCK
    return pl.pallas_call(
        functools.partial(_dispatch_kernel, tm, ne),
        out_shape=jax.ShapeDtypeStruct((n_rows, w), h2p.dtype),
        grid_spec=pltpu.PrefetchScalarGridSpec(
            num_scalar_prefetch=2,
            grid=(r // tm,),
            in_specs=[pl.BlockSpec((TOP_K, tm), lambda i, *_: (0, i), memory_space=pltpu.SMEM),
                      pl.BlockSpec((TOP_K, tm), lambda i, *_: (0, i), memory_space=pltpu.SMEM),
                      pl.BlockSpec(memory_space=pl.ANY)],
            out_specs=pl.BlockSpec(memory_space=pl.ANY),
            scratch_shapes=[pltpu.SemaphoreType.DMA]),
        compiler_params=pltpu.CompilerParams(dimension_semantics=("arbitrary",),
                                             has_side_effects=True),
        name="moe_dispatch",
    )(pstart, npad, e_idx, rank, h2p)


def _unpack_rows(xw):
    lo = pltpu.bitcast(xw << 16, F32).astype(BF16)
    hi = pltpu.bitcast(xw & jnp.uint32(0xFFFF0000), F32).astype(BF16)
    return lo, hi


def _moe_up_kernel(tf, we_ref, wj_ref, wb_ref_s, wob_ref, woj_ref, wfirst_ref, wvalid_ref,
                   x_ref, w_ref, b_ref, act_ref, wbf_ref, gt_ref):
    s = pl.program_id(0)
    half = x_ref.shape[1]
    nslab = MOE_BLOCK // LANES

    @pl.when(wfirst_ref[s] == 1)
    def _():
        wbf_ref[...] = w_ref[0].astype(BF16)

    @pl.when(wvalid_ref[s] == 0)
    def _():
        act_ref[...] = jnp.zeros_like(act_ref)

    @pl.when(wvalid_ref[s] == 1)
    def _():
        lo, hi = _unpack_rows(x_ref[...])
        gu = (jnp.dot(lo, wbf_ref[0:half, :], preferred_element_type=F32)
              + jnp.dot(hi, wbf_ref[half:, :], preferred_element_type=F32)
              + b_ref[0])
        gut = gu.T
        for sl in range(nslab):
            gt_ref[sl] = gut[:, sl * LANES:(sl + 1) * LANES]
        parts = []
        for sl in range(nslab):
            gate = jnp.minimum(gt_ref[sl, pl.ds(0, tf, stride=2), :], SWIGLU_LIMIT)
            up = jnp.clip(gt_ref[sl, pl.ds(1, tf, stride=2), :], -SWIGLU_LIMIT, SWIGLU_LIMIT)
            parts.append((up + 1.0) * gate * _sigmoid(SWIGLU_ALPHA * gate))
        act_t = jnp.concatenate(parts, axis=1)
        act_ref[...] = act_t.T.astype(BF16)


def _moe_up(xs, w_gu, b_gu, wl, tf=512):
    n_rows, half = xs.shape
    ne, d, f2 = w_gu.shape
    f = f2 // 2
    tf = min(tf, f)
    assert f % tf == 0 and d == 2 * half
    n_steps = wl[0].shape[0]
    return pl.pallas_call(
        functools.partial(_moe_up_kernel, tf),
        out_shape=jax.ShapeDtypeStruct((n_rows, f), BF16),
        grid_spec=pltpu.PrefetchScalarGridSpec(
            num_scalar_prefetch=len(wl),
            grid=(n_steps,),
            in_specs=[pl.BlockSpec((MOE_BLOCK, half), lambda s, we, wj, wb, *_: (wb[s], 0)),
                      pl.BlockSpec((1, d, 2 * tf), lambda s, we, wj, *_: (we[s], 0, wj[s])),
                      pl.BlockSpec((1, 1, 2 * tf), lambda s, we, wj, *_: (we[s], 0, wj[s]))],
            out_specs=pl.BlockSpec((MOE_BLOCK, tf), lambda s, we, wj, wb, wob, woj, *_: (wob[s], woj[s])),
            scratch_shapes=[pltpu.VMEM((d, 2 * tf), BF16),
                            pltpu.VMEM((MOE_BLOCK // LANES, 2 * tf, LANES), F32)]),
        compiler_params=_cparams(1),
        name="moe_up",
    )(*wl, xs, w_gu, b_gu.reshape(ne, 1, f2))


def _moe_down_kernel(we_ref, wj_ref, wb_ref_s, wob_ref, woj_ref, wfirst_ref, wvalid_ref,
                     a_ref, w_ref, b_ref, y_ref, wbf_ref):
    s = pl.program_id(0)

    @pl.when(wfirst_ref[s] == 1)
    def _():
        wbf_ref[...] = w_ref[0].astype(BF16)

    @pl.when(wvalid_ref[s] == 0)
    def _():
        y_ref[...] = jnp.zeros_like(y_ref)

    @pl.when(wvalid_ref[s] == 1)
    def _():
        y_ref[...] = jnp.dot(a_ref[...], wbf_ref[...], preferred_element_type=F32) + b_ref[0]


def _moe_down(act, w_d, b_d, wl, tn=1024):
    n_rows, f = act.shape
    ne, _, d = w_d.shape
    tn = min(tn, d)
    assert d % tn == 0
    n_steps = wl[0].shape[0]
    return pl.pallas_call(
        _moe_down_kernel,
        out_shape=jax.ShapeDtypeStruct((n_rows, d), F32),
        grid_spec=pltpu.PrefetchScalarGridSpec(
            num_scalar_prefetch=len(wl),
            grid=(n_steps,),
            in_specs=[pl.BlockSpec((MOE_BLOCK, f), lambda s, we, wj, wb, *_: (wb[s], 0)),
                      pl.BlockSpec((1, f, tn), lambda s, we, wj, *_: (we[s], 0, wj[s])),
                      pl.BlockSpec((1, 1, tn), lambda s, we, wj, *_: (we[s], 0, wj[s]))],
            out_specs=pl.BlockSpec((MOE_BLOCK, tn), lambda s, we, wj, wb, wob, woj, *_: (wob[s], woj[s])),
            scratch_shapes=[pltpu.VMEM((f, tn), BF16)]),
        compiler_params=_cparams(1),
        name="moe_down",
    )(*wl, act, w_d, b_d.reshape(ne, 1, d))


def _work_list(nblk_e, blk_start_e, n_tiles, n_blocks_max):
    ne = nblk_e.shape[0]
    n_steps = n_blocks_max * n_tiles
    steps_e = nblk_e * n_tiles
    cum = jnp.cumsum(steps_e)
    total = cum[-1]
    s = jnp.arange(n_steps, dtype=jnp.int32)
    sc = jnp.minimum(s, total - 1)
    e = jnp.minimum(jnp.searchsorted(cum, sc, side="right"), ne - 1).astype(jnp.int32)
    r = sc - (cum[e] - steps_e[e])
    nb = jnp.maximum(nblk_e[e], 1)
    j = (r // nb).astype(jnp.int32)
    ib = (r % nb).astype(jnp.int32)
    valid = (s < total).astype(jnp.int32)
    first = jnp.logical_and(ib == 0, s < total).astype(jnp.int32)
    blk = (blk_start_e[e] + ib).astype(jnp.int32)
    n_used = jnp.sum(nblk_e)
    idle = jnp.maximum(s - total, 0)
    oblk = jnp.where(s < total, blk, n_used + idle // n_tiles).astype(jnp.int32)
    oj = jnp.where(s < total, j, idle % n_tiles).astype(jnp.int32)
    return (e, j, blk, oblk, oj, first, valid)


def _combine_kernel(rows, tm, d, pstart_ref, e_ref, rank_ref, gt_ref, x_ref, mod_ref, gf_ref,
                    yb_ref, y_ref, buf_ref, sem):
    i = pl.program_id(0)

    def row_copy(dst_k, dst_t, src_row):
        return pltpu.make_async_copy(yb_ref.at[pl.ds(src_row, 1), :],
                                     buf_ref.at[dst_k, pl.ds(dst_t, 1), :], sem)

    def issue(t, carry):
        for kk in range(TOP_K):
            src = pstart_ref[e_ref[kk, t]] + rank_ref[kk, t]
            row_copy(kk, t, src).start()
        return carry

    lax.fori_loop(0, tm, issue, 0)

    def drain(t, carry):
        for kk in range(TOP_K):
            row_copy(0, 0, 0).wait()
        return carry

    lax.fori_loop(0, tm, drain, 0)

    gf = gf_ref[...]
    for gi in range(tm // CHUNK):
        rs = slice(gi * CHUNK, (gi + 1) * CHUNK)
        seq = rows.seq_of_group(i * (tm // CHUNK) + gi)
        gate2 = mod_ref[pl.ds(seq, 1), :]
        ff = jnp.zeros((CHUNK, d), F32)
        for kk in range(TOP_K):
            ff = ff + buf_ref[kk, rs, :] * gt_ref[rs, kk:kk + 1]
        x2 = x_ref[rs, :] + gate2 * ff
        ms = jnp.mean(x2 * x2, axis=-1, keepdims=True)
        y_ref[rs, :] = x2 * lax.rsqrt(ms + RMS_EPS) * gf


def _combine(rows, x1, mod, g_final, yb, e_idx, rank, gates_t, pstart, tm=128):
    r, d = x1.shape
    tm = min(tm, r)
    assert r % tm == 0
    gate2_blk = 5
    return pl.pallas_call(
        functools.partial(_combine_kernel, rows, tm, d),
        out_shape=jax.ShapeDtypeStruct((r, d), F32),
        grid_spec=pltpu.PrefetchScalarGridSpec(
            num_scalar_prefetch=1,
            grid=(r // tm,),
            in_specs=[pl.BlockSpec((TOP_K, tm), lambda i, *_: (0, i), memory_space=pltpu.SMEM),
                      pl.BlockSpec((TOP_K, tm), lambda i, *_: (0, i), memory_space=pltpu.SMEM),
                      pl.BlockSpec((tm, TOP_K), lambda i, *_: (i, 0)),
                      pl.BlockSpec((tm, d), lambda i, *_: (i, 0)),
                      pl.BlockSpec((mod.shape[0], d), lambda i, *_: (0, gate2_blk)),
                      pl.BlockSpec((1, d), lambda i, *_: (0, 0)),
                      pl.BlockSpec(memory_space=pl.ANY)],
            out_specs=pl.BlockSpec((tm, d), lambda i, *_: (i, 0)),
            scratch_shapes=[pltpu.VMEM((TOP_K, tm, d), F32),
                            pltpu.SemaphoreType.DMA]),
        compiler_params=_cparams(1),
        name="moe_combine",
    )(pstart, e_idx, rank, gates_t, x1, mod, g_final.reshape(1, d), yb)


def kernel(x_prompt, x_sample, c_prompt, c_sample, state_hgrn_S, state_conv, state_mlstm_C,
           state_mlstm_n, state_mlstm_m, w_ada, b_ada, g_mix, g_ffn, w_in, lb_logits, conv_w,
           conv_b, b_igate, b_fgate, g_hgrn, g_mlstm, w_out, w_router, b_router, w_gate_up,
           b_gate_up, w_down, b_down, g_final):
    depth = w_ada.shape[0]
    assert depth == 1, "single-layer trunk"
    bp, tp, d = x_prompt.shape
    bs, ts, _ = x_sample.shape
    rows = _Rows(bp, tp, bs, ts)
    _, _, nh_h, dk, dv = state_hgrn_S.shape
    _, _, nh_m, dqk, dvm = state_mlstm_C.shape
    hw = nh_h * dk
    mw = nh_m * dvm
    n_main = 4 * hw + 3 * mw
    assert w_in.shape[2] == n_main + 2 * nh_m and 2 * nh_m <= GATE_LANES
    assert mw == hw, "column sections of the input projection are addressed in hw-wide blocks"
    ne = w_router.shape[2]

    lb = jax.nn.softmax(lb_logits.astype(F32), axis=0)[0]

    x_all = jnp.concatenate([x_prompt.reshape(bp * tp, d), x_sample.reshape(bs * ts, d)], axis=0)
    n_c = bp + bs
    n_c_pad = -(-n_c // 8) * 8
    c_pad = jnp.zeros((n_c_pad, d), F32).at[:n_c].set(jnp.concatenate([c_prompt, c_sample], axis=0))
    mod = _ada(c_pad, w_ada[0], b_ada[0])

    w_in0 = w_in[0]
    wg_pad = jnp.zeros((d, GATE_LANES), F32).at[:, :2 * nh_m].set(w_in0[:, n_main:])
    h_all, gates = _prep(rows, x_all, g_mix[0], mod, wg_pad)
    z = _inproj(h_all, w_in0, n_main)

    og, st = _hgrn(rows, z, state_hgrn_S[0], lb, g_hgrn[0])
    hm, conv_new, c_new, n_new, m_new = _mlstm(
        rows, z, 4, gates, state_conv[0], state_mlstm_C[0], state_mlstm_n[0], state_mlstm_m[0],
        conv_w[0], conv_b[0], b_igate[0], b_fgate[0], g_mlstm[0])

    x1 = _outproj(rows, og, hm, w_out[0], x_all, mod)

    h2p, e_idx, gate_k, rank, cnt = _router(rows, x1, g_ffn[0], mod, w_router[0], b_router[0])

    counts = cnt[:, 0].astype(jnp.int32)
    nblk_e = (counts + MOE_BLOCK - 1) // MOE_BLOCK
    padded = nblk_e * MOE_BLOCK
    pad_end = jnp.cumsum(padded)
    pstart = jnp.concatenate([jnp.zeros((1,), jnp.int32), pad_end]).astype(jnp.int32)
    npad = (padded - counts).astype(jnp.int32)
    n_blocks_max = -(-(rows.rows * TOP_K) // MOE_BLOCK) + ne
    n_rows = n_blocks_max * MOE_BLOCK
    blk_start_e = (pstart[:ne] // MOE_BLOCK).astype(jnp.int32)

    xs = _dispatch(h2p, e_idx, rank, pstart, npad, n_rows)

    f = w_gate_up.shape[3] // 2
    tf = min(512, f)
    tn = min(1024, d)
    act = _moe_up(xs, w_gate_up[0], b_gate_up[0], _work_list(nblk_e, blk_start_e, f // tf, n_blocks_max), tf)
    yb = _moe_down(act, w_down[0], b_down[0], _work_list(nblk_e, blk_start_e, d // tn, n_blocks_max), tn)

    y_all = _combine(rows, x1, mod, g_final, yb, e_idx, rank, gate_k.T, pstart)

    y_prompt = y_all[:bp * tp].reshape(bp, tp, d)
    y_sample = y_all[bp * tp:].reshape(bs, ts, d)
    s_all = jnp.swapaxes(st, 2, 3)
    m_all = m_new.reshape(rows.nseq, nh_m)

    def split(a):
        return a[:bp][None], a[bp:][None]

    p_s, s_s = split(s_all)
    p_conv, s_conv = split(conv_new)
    p_c, s_c = split(c_new)
    p_n, s_n = split(n_new)
    p_m, s_m = split(m_all)
    return (y_prompt, y_sample, p_s, p_conv, p_c, p_n, p_m, s_s, s_conv, s_c, s_n, s_m)
```

```python
import functools

import numpy as np
import jax
import jax.numpy as jnp
from jax import lax
from jax.experimental import pallas as pl
from jax.experimental.pallas import tpu as pltpu

F32 = jnp.float32
BF16 = jnp.bfloat16

CHUNK = 64
TOP_K = 4
CONV_K = 4
RMS_EPS = 1e-6
SWIGLU_LIMIT = 7.0
SWIGLU_ALPHA = 1.702
MOE_BLOCK = 256
LANES = 128
GATE_LANES = 128
VMEM_LIMIT = 56 * 1024 * 1024
HGRN_LEVELS = (32, 16, 8, 4, 2, 1)


def _cparams(n_axes, vmem=VMEM_LIMIT):
    return pltpu.CompilerParams(dimension_semantics=("arbitrary",) * n_axes,
                                vmem_limit_bytes=vmem)


def _sigmoid(x):
    return 1.0 / (1.0 + jnp.exp(-x))


def _silu(x):
    return x * _sigmoid(x)


class _Rows:
    def __init__(self, bp, tp, bs, ts):
        assert tp % CHUNK == 0 and ts % CHUNK == 0
        self.bp, self.tp, self.bs, self.ts = bp, tp, bs, ts
        self.nblk_p, self.nblk_s = tp // CHUNK, ts // CHUNK
        self.groups_p = bp * self.nblk_p
        self.groups = self.groups_p + bs * self.nblk_s
        self.rows = self.groups * CHUNK
        self.nseq = bp + bs

    def seq_of_group(self, g):
        return jnp.where(g < self.groups_p, g // self.nblk_p,
                         self.bp + (g - self.groups_p) // self.nblk_s)

    def blk_in_seq(self, g):
        return jnp.where(g < self.groups_p, g % self.nblk_p, (g - self.groups_p) % self.nblk_s)

    def nblk_of_group(self, g):
        return jnp.where(g < self.groups_p, self.nblk_p, self.nblk_s)


def _ada_kernel(c_ref, w_ref, b_ref, o_ref):
    c = c_ref[...]
    s = _silu(c).astype(BF16)
    o_ref[...] = jnp.dot(s, w_ref[...].astype(BF16), preferred_element_type=F32) + b_ref[...]


def _ada(c_pad, w_ada, b_ada, tn=512):
    m, d = c_pad.shape
    n = w_ada.shape[1]
    tn = min(tn, n)
    assert n % tn == 0
    return pl.pallas_call(
        _ada_kernel,
        out_shape=jax.ShapeDtypeStruct((m, n), F32),
        grid=(n // tn,),
        in_specs=[pl.BlockSpec((m, d), lambda j: (0, 0)),
                  pl.BlockSpec((d, tn), lambda j: (0, j)),
                  pl.BlockSpec((1, tn), lambda j: (0, j))],
        out_specs=pl.BlockSpec((m, tn), lambda j: (0, j)),
        compiler_params=_cparams(1),
        name="ada_mod",
    )(c_pad, w_ada, b_ada.reshape(1, n))


def _prep_kernel(rows, tm, d, xp_ref, xs_ref, g_ref, mod_ref, wg_ref, h_ref, gates_ref):
    i = pl.program_id(0)
    is_prompt = i < rows.groups_p * CHUNK // tm
    g = g_ref[...]
    for gi in range(tm // CHUNK):
        seq = rows.seq_of_group(i * (tm // CHUNK) + gi)
        m = mod_ref[pl.ds(seq, 1), :]
        shift, scale = m[:, 0:d], m[:, d:2 * d]
        rs = slice(gi * CHUNK, (gi + 1) * CHUNK)
        x = jnp.where(is_prompt, xp_ref[rs, :], xs_ref[rs, :])
        ms = jnp.mean(x * x, axis=-1, keepdims=True)
        y = x * lax.rsqrt(ms + RMS_EPS) * g
        h = y * (1.0 + scale) + shift
        h_ref[gi * CHUNK:(gi + 1) * CHUNK, :] = h.astype(BF16)
        gates_ref[gi * CHUNK:(gi + 1) * CHUNK, :] = jnp.dot(
            h, wg_ref[...], precision=lax.Precision.HIGHEST, preferred_element_type=F32)


def _split_row_specs(rows, tm, tn, col_major=False):
    n_pt = rows.groups_p * CHUNK // tm
    n_st = rows.rows // tm - n_pt
    assert n_pt * tm == rows.groups_p * CHUNK and n_st >= 1

    def ij(args):
        return (args[1], args[0]) if col_major else (args[0], 0)

    def p_map(*args):
        i, j = ij(args)
        return (jnp.minimum(i, n_pt - 1), j)

    def s_map(*args):
        i, j = ij(args)
        return (jnp.maximum(i - n_pt, 0), j)

    return pl.BlockSpec((tm, tn), p_map), pl.BlockSpec((tm, tn), s_map)


def _prep(rows, x_p, x_s, g_mix, mod, wg_pad, tm=256):
    d = x_p.shape[1]
    r = rows.rows
    tm = min(tm, x_s.shape[0])
    assert r % tm == 0
    xp_spec, xs_spec = _split_row_specs(rows, tm, d)
    return pl.pallas_call(
        functools.partial(_prep_kernel, rows, tm, d),
        out_shape=(jax.ShapeDtypeStruct((r, d), BF16),
                   jax.ShapeDtypeStruct((r, GATE_LANES), F32)),
        grid=(r // tm,),
        in_specs=[xp_spec, xs_spec,
                  pl.BlockSpec((1, d), lambda i: (0, 0)),
                  pl.BlockSpec(mod.shape, lambda i: (0, 0)),
                  pl.BlockSpec(wg_pad.shape, lambda i: (0, 0))],
        out_specs=(pl.BlockSpec((tm, d), lambda i: (i, 0)),
                   pl.BlockSpec((tm, GATE_LANES), lambda i: (i, 0))),
        compiler_params=_cparams(1),
        name="prep_norm_mod",
    )(x_p, x_s, g_mix.reshape(1, d), mod, wg_pad)


def _inproj_kernel(h_ref, w_ref, z_ref, wb_ref):
    @pl.when(pl.program_id(1) == 0)
    def _():
        wb_ref[...] = w_ref[...].astype(BF16)

    z_ref[...] = jnp.dot(h_ref[...], wb_ref[...], preferred_element_type=F32)


def _inproj(h_all, w_in, n_main, tm=1024, tn=512):
    r, d = h_all.shape
    tm, tn = min(tm, r), min(tn, n_main)
    assert r % tm == 0 and n_main % tn == 0
    return pl.pallas_call(
        _inproj_kernel,
        out_shape=jax.ShapeDtypeStruct((r, n_main), F32),
        grid=(n_main // tn, r // tm),
        in_specs=[pl.BlockSpec((tm, d), lambda j, i: (i, 0)),
                  pl.BlockSpec((d, tn), lambda j, i: (0, j))],
        out_specs=pl.BlockSpec((tm, tn), lambda j, i: (i, j)),
        scratch_shapes=[pltpu.VMEM((d, tn), BF16)],
        compiler_params=_cparams(2),
        name="in_proj",
    )(h_all, w_in)


def _hgrn_consts():
    L = CHUNK
    t = np.arange(L)[:, None]
    s = np.arange(L)[None, :]
    mats = [(s <= t)]
    masks = []
    for m in HGRN_LEVELS:
        start = (t // (2 * m)) * (2 * m)
        mats.append(s <= start + m - 1)
        masks.append((t // (2 * m) == s // (2 * m)) & (t % (2 * m) >= m) & (s % (2 * m) < m))
    mstack = np.concatenate(mats, axis=0).astype(np.float32)
    masks = np.stack(masks, axis=0).astype(np.float32)
    return jnp.asarray(mstack, BF16), jnp.asarray(masks, F32)


def _split3(x):
    x1 = x.astype(BF16)
    r1 = x - x1.astype(F32)
    x2 = r1.astype(BF16)
    x3 = (r1 - x2.astype(F32)).astype(BF16)
    return x1, x2, x3


def _hgrn_kernel(rows, nh, dk, dv, hq_ref, hf_ref, hi_ref, hg_ref, s0_ref, lb_ref, gn_ref,
                 mstack_ref, masks_ref, og_ref, st_ref, r_ref, o_ref):
    L = CHUNK
    g = pl.program_id(0)
    blk = rows.blk_in_seq(g)
    is_prompt = g < rows.groups_p

    @pl.when(jnp.logical_and(blk == 0, is_prompt))
    def _():
        st_ref[...] = jnp.zeros_like(st_ref)

    @pl.when(jnp.logical_and(blk == 0, jnp.logical_not(is_prompt)))
    def _():
        for h in range(nh):
            st_ref[0, h] = s0_ref[0, h].T

    lb = lb_ref[...]
    f = lb + (1.0 - lb) * _sigmoid(hf_ref[...])
    lf = jnp.log(f)
    mstack = mstack_ref[...]
    p1, p2, p3 = _split3(lf)
    r_ref[...] = (jnp.dot(mstack, p1, preferred_element_type=F32)
                  + jnp.dot(mstack, p2, preferred_element_type=F32)
                  + jnp.dot(mstack, p3, preferred_element_type=F32))

    for h in range(nh):
        cs = slice(h * dk, (h + 1) * dk)
        vs = slice(h * dv, (h + 1) * dv)
        q = _silu(hq_ref[:, cs]) * (dk ** -0.5)
        k = 1.0 - f[:, cs]
        v = hi_ref[:, vs]
        vb = v.astype(BF16)
        b = r_ref[0:L, cs]
        st = st_ref[0, h]
        qe = (q * jnp.exp(b)).astype(BF16)
        o = lax.dot_general(qe, st.astype(BF16), (((1,), (1,)), ((), ())),
                            preferred_element_type=F32)
        a = jnp.zeros((L, L), F32)
        for li in range(len(HGRN_LEVELS)):
            rl = r_ref[(li + 1) * L:(li + 2) * L, cs]
            e = jnp.exp(-jnp.abs(b - rl))
            al = lax.dot_general((q * e).astype(BF16), (k * e).astype(BF16),
                                 (((1,), (1,)), ((), ())), preferred_element_type=F32)
            a = a + al * masks_ref[li]
        diag = jnp.sum(q * k, axis=-1, keepdims=True)
        o = o + jnp.dot(a.astype(BF16), vb, preferred_element_type=F32) + diag * v
        o_ref[:, vs] = o
        b_last = b[L - 1:L, :]
        kd = (k * jnp.exp(b_last - b)).astype(BF16)
        st_ref[0, h] = st * jnp.exp(b_last) + lax.dot_general(
            vb, kd, (((0,), (0,)), ((), ())), preferred_element_type=F32)

    o = o_ref[...]
    ms = jnp.mean(o * o, axis=-1, keepdims=True)
    og = o * lax.rsqrt(ms + RMS_EPS) * gn_ref[...] * _silu(hg_ref[...])
    og_ref[...] = og.astype(BF16)


def _hgrn(rows, z, s0, lb, g_hgrn):
    bs, nh, dk, dv = s0.shape
    hw = nh * dk
    assert nh * dv == hw and hw % LANES == 0
    mstack, masks = _hgrn_consts()
    n_lv = len(HGRN_LEVELS)
    L = CHUNK

    def zspec(sec):
        return pl.BlockSpec((L, hw), lambda g, sec=sec: (g, sec))

    og, st = pl.pallas_call(
        functools.partial(_hgrn_kernel, rows, nh, dk, dv),
        out_shape=(jax.ShapeDtypeStruct((rows.rows, hw), BF16),
                   jax.ShapeDtypeStruct((rows.nseq, nh, dv, dk), F32)),
        grid=(rows.groups,),
        in_specs=[zspec(0), zspec(1), zspec(2), zspec(3),
                  pl.BlockSpec((1, nh, dk, dv),
                               lambda g: (jnp.maximum(rows.seq_of_group(g) - rows.bp, 0), 0, 0, 0)),
                  pl.BlockSpec((1, hw), lambda g: (0, 0)),
                  pl.BlockSpec((1, hw), lambda g: (0, 0)),
                  pl.BlockSpec(mstack.shape, lambda g: (0, 0)),
                  pl.BlockSpec(masks.shape, lambda g: (0, 0, 0))],
        out_specs=(pl.BlockSpec((L, hw), lambda g: (g, 0)),
                   pl.BlockSpec((1, nh, dv, dk), lambda g: (rows.seq_of_group(g), 0, 0, 0))),
        scratch_shapes=[pltpu.VMEM(((n_lv + 1) * L, hw), F32),
                        pltpu.VMEM((L, hw), F32)],
        compiler_params=_cparams(1),
        name="hgrn2",
    )(z, z, z, z, s0, lb.reshape(1, hw), g_hgrn.reshape(1, hw), mstack, masks)
    return og, st


def _log_sigmoid(x):
    return jnp.minimum(x, 0.0) - jnp.log(1.0 + jnp.exp(-jnp.abs(x)))


def _mlstm_kernel(rows, nh, dqk, dv, mqk_ref, mv_ref, mo_ref, gates_ref, conv0_ref, c0_ref,
                  n0_ref, m0_ref, cw_ref, cb_ref, gbias_ref, gn_ref,
                  hm_ref, conv_ref, c_ref, n_ref, m_ref, ubuf_ref):
    L = CHUNK
    kq = nh * dqk
    pad = 8
    g = pl.program_id(0)
    blk = rows.blk_in_seq(g)
    is_prompt = g < rows.groups_p

    @pl.when(jnp.logical_and(blk == 0, is_prompt))
    def _():
        ubuf_ref[0:pad, :] = jnp.zeros((pad, 2 * kq), F32)
        c_ref[...] = jnp.zeros_like(c_ref)
        n_ref[...] = jnp.zeros_like(n_ref)
        m_ref[...] = jnp.zeros_like(m_ref)

    @pl.when(jnp.logical_and(blk == 0, jnp.logical_not(is_prompt)))
    def _():
        ubuf_ref[0:pad, :] = jnp.zeros((pad, 2 * kq), F32)
        ubuf_ref[pad - (CONV_K - 1):pad, :] = conv0_ref[0]
        c_ref[...] = c0_ref[...]
        n_ref[...] = n0_ref[...]
        m_ref[...] = m0_ref[...]

    ubuf_ref[pad:pad + L, :] = mqk_ref[...]
    acc = cb_ref[...] + jnp.zeros((L, 2 * kq), F32)
    for j in range(CONV_K):
        off = pad - (CONV_K - 1) + j
        acc = acc + ubuf_ref[off:off + L, :] * cw_ref[j:j + 1, :]
    new_tail = ubuf_ref[pad + L - (CONV_K - 1):pad + L, :]
    conv_ref[0] = new_tail
    ubuf_ref[pad - (CONV_K - 1):pad, :] = new_tail
    qk = _silu(acc)

    gt = gates_ref[...] + gbias_ref[...]
    lane = lax.broadcasted_iota(jnp.int32, gt.shape, 1)
    pg = jnp.where(lane < nh, gt, _log_sigmoid(gt))
    pgt = pg.T
    ti = lax.broadcasted_iota(jnp.int32, (L, L), 0)
    si = lax.broadcasted_iota(jnp.int32, (L, L), 1)
    tri = si <= ti

    for h in range(nh):
        q = qk[:, h * dqk:(h + 1) * dqk]
        k = qk[:, kq + h * dqk:kq + (h + 1) * dqk] * (dqk ** -0.5)
        v = mv_ref[:, h * dv:(h + 1) * dv]
        qb, kb, vb = q.astype(BF16), k.astype(BF16), v.astype(BF16)
        ig_c, lf_c = pg[:, h:h + 1], pg[:, nh + h:nh + h + 1]
        ig_r, lf_r = pgt[h:h + 1, :], pgt[nh + h:nh + h + 1, :]
        b_c = jnp.sum(jnp.where(tri, lf_r, 0.0), axis=1, keepdims=True)
        b_r = jnp.sum(jnp.where(ti <= si, lf_c, 0.0), axis=0, keepdims=True)
        m_prev = m_ref[0, :, h:h + 1]
        log_d = jnp.where(tri, b_c - b_r + ig_r, -jnp.inf)
        log_inter = b_c + m_prev
        m_t = jnp.maximum(jnp.max(log_d, axis=1, keepdims=True), log_inter)
        dm = jnp.exp(log_d - m_t)
        s_mat = lax.dot_general(qb, kb, (((1,), (1,)), ((), ())), preferred_element_type=F32) * dm
        w_inter = jnp.exp(log_inter - m_t)
        c_h = c_ref[0, h]
        n_h = n_ref[0, h:h + 1, :]
        num = (jnp.dot(s_mat.astype(BF16), vb, preferred_element_type=F32)
               + w_inter * jnp.dot(qb, c_h.astype(BF16), preferred_element_type=F32))
        den = (jnp.sum(s_mat, axis=1, keepdims=True)
               + w_inter * jnp.sum(q * n_h, axis=1, keepdims=True))
        hh = num / jnp.maximum(jnp.abs(den), jnp.exp(-m_t))
        b_last = b_c[L - 1:L, :]
        lw_c = b_last - b_c + ig_c
        lw_r = b_last - b_r + ig_r
        m_new = jnp.maximum(b_last + m_prev, jnp.max(lw_r, axis=1, keepdims=True))
        decay = jnp.exp(b_last + m_prev - m_new)
        kw = k * jnp.exp(lw_c - m_new)
        c_ref[0, h] = decay * c_h + lax.dot_general(
            kw.astype(BF16), vb, (((0,), (0,)), ((), ())), preferred_element_type=F32)
        n_ref[0, h:h + 1, :] = decay * n_h + jnp.sum(kw, axis=0, keepdims=True)
        m_ref[0, :, h:h + 1] = m_new
        ms = jnp.mean(hh * hh, axis=-1, keepdims=True)
        hn = hh * lax.rsqrt(ms + RMS_EPS) * gn_ref[:, h * dv:(h + 1) * dv]
        hm_ref[:, h * dv:(h + 1) * dv] = (hn * _sigmoid(mo_ref[:, h * dv:(h + 1) * dv])).astype(BF16)


def _mlstm(rows, z, sec_qk, gates, conv0, c0, n0, m0, conv_w, conv_b, b_igate, b_fgate, g_mlstm):
    bs, nh, dqk, dv = c0.shape
    kq = nh * dqk
    mw = nh * dv
    L = CHUNK
    assert 2 * kq == mw, "q/k conv width must equal the value width for the column sections"
    gbias = jnp.zeros((1, GATE_LANES), F32)
    gbias = gbias.at[0, 0:nh].set(b_igate.astype(F32)).at[0, nh:2 * nh].set(b_fgate.astype(F32))

    def sseq(g):
        return jnp.maximum(rows.seq_of_group(g) - rows.bp, 0)

    outs = pl.pallas_call(
        functools.partial(_mlstm_kernel, rows, nh, dqk, dv),
        out_shape=(jax.ShapeDtypeStruct((rows.rows, mw), BF16),
                   jax.ShapeDtypeStruct((rows.nseq, CONV_K - 1, 2 * kq), F32),
                   jax.ShapeDtypeStruct((rows.nseq, nh, dqk, dv), F32),
                   jax.ShapeDtypeStruct((rows.nseq, nh, dqk), F32),
                   jax.ShapeDtypeStruct((rows.nseq, 1, nh), F32)),
        grid=(rows.groups,),
        in_specs=[pl.BlockSpec((L, mw), lambda g: (g, sec_qk)),
                  pl.BlockSpec((L, mw), lambda g: (g, sec_qk + 1)),
                  pl.BlockSpec((L, mw), lambda g: (g, sec_qk + 2)),
                  pl.BlockSpec((L, GATE_LANES), lambda g: (g, 0)),
                  pl.BlockSpec((1, CONV_K - 1, 2 * kq), lambda g: (sseq(g), 0, 0)),
                  pl.BlockSpec((1, nh, dqk, dv), lambda g: (sseq(g), 0, 0, 0)),
                  pl.BlockSpec((1, nh, dqk), lambda g: (sseq(g), 0, 0)),
                  pl.BlockSpec((1, 1, nh), lambda g: (sseq(g), 0, 0)),
                  pl.BlockSpec((CONV_K, 2 * kq), lambda g: (0, 0)),
                  pl.BlockSpec((1, 2 * kq), lambda g: (0, 0)),
                  pl.BlockSpec((1, GATE_LANES), lambda g: (0, 0)),
                  pl.BlockSpec((1, mw), lambda g: (0, 0))],
        out_specs=(pl.BlockSpec((L, mw), lambda g: (g, 0)),
                   pl.BlockSpec((1, CONV_K - 1, 2 * kq), lambda g: (rows.seq_of_group(g), 0, 0)),
                   pl.BlockSpec((1, nh, dqk, dv), lambda g: (rows.seq_of_group(g), 0, 0, 0)),
                   pl.BlockSpec((1, nh, dqk), lambda g: (rows.seq_of_group(g), 0, 0)),
                   pl.BlockSpec((1, 1, nh), lambda g: (rows.seq_of_group(g), 0, 0))),
        scratch_shapes=[pltpu.VMEM((8 + L, 2 * kq), F32)],
        compiler_params=_cparams(1),
        name="mlstm",
    )(z, z, z, gates, conv0, c0, n0, m0.reshape(bs, 1, nh), conv_w, conv_b.reshape(1, 2 * kq),
      gbias, g_mlstm.reshape(1, mw))
    return outs


def _outproj_kernel(rows, tm, d, hw, og_ref, hm_ref, w_ref, xp_ref, xs_ref, mod_ref, o_ref, wb_ref):
    j, i = pl.program_id(0), pl.program_id(1)
    is_prompt = i < rows.groups_p * CHUNK // tm

    @pl.when(i == 0)
    def _():
        wb_ref[...] = w_ref[...].astype(BF16)

    mix = (jnp.dot(og_ref[...], wb_ref[0:hw, :], preferred_element_type=F32)
           + jnp.dot(hm_ref[...], wb_ref[hw:, :], preferred_element_type=F32))
    for gi in range(tm // CHUNK):
        seq = rows.seq_of_group(i * (tm // CHUNK) + gi)
        gate = mod_ref[pl.ds(seq, 1), :]
        rs = slice(gi * CHUNK, (gi + 1) * CHUNK)
        x = jnp.where(is_prompt, xp_ref[rs, :], xs_ref[rs, :])
        o_ref[rs, :] = x + gate * mix[rs, :]


def _outproj(rows, og, hm, w_out, x_p, x_s, mod, tm=1024, tn=512):
    d = x_p.shape[1]
    r = rows.rows
    hw = og.shape[1]
    tm, tn = min(tm, x_s.shape[0]), min(tn, d)
    assert r % tm == 0 and d % tn == 0
    xp_spec, xs_spec = _split_row_specs(rows, tm, tn, col_major=True)
    gate_blk0 = 2 * d // tn
    return pl.pallas_call(
        functools.partial(_outproj_kernel, rows, tm, d, hw),
        out_shape=jax.ShapeDtypeStruct((r, d), F32),
        grid=(d // tn, r // tm),
        in_specs=[pl.BlockSpec((tm, hw), lambda j, i: (i, 0)),
                  pl.BlockSpec((tm, hm.shape[1]), lambda j, i: (i, 0)),
                  pl.BlockSpec((w_out.shape[0], tn), lambda j, i: (0, j)),
                  xp_spec, xs_spec,
                  pl.BlockSpec((mod.shape[0], tn), lambda j, i: (0, gate_blk0 + j))],
        out_specs=pl.BlockSpec((tm, tn), lambda j, i: (i, j)),
        scratch_shapes=[pltpu.VMEM((w_out.shape[0], tn), BF16)],
        compiler_params=_cparams(2),
        name="out_proj",
    )(og, hm, w_out, x_p, x_s, mod)


def _router_kernel(rows, tm, d, ne, x_ref, g_ref, mod_ref, wr_ref, br_ref, ut_ref,
                   h2_ref, e_ref, gate_ref, rank_ref, cnt_ref, h2s_ref):
    i = pl.program_id(0)
    half = d // 2

    @pl.when(i == 0)
    def _():
        cnt_ref[...] = jnp.zeros_like(cnt_ref)

    g = g_ref[...]
    for gi in range(tm // CHUNK):
        seq = rows.seq_of_group(i * (tm // CHUNK) + gi)
        m = mod_ref[pl.ds(seq, 1), :]
        shift, scale = m[:, 3 * d:4 * d], m[:, 4 * d:5 * d]
        x = x_ref[gi * CHUNK:(gi + 1) * CHUNK, :]
        ms = jnp.mean(x * x, axis=-1, keepdims=True)
        h2s_ref[gi * CHUNK:(gi + 1) * CHUNK, :] = (x * lax.rsqrt(ms + RMS_EPS) * g) * (1.0 + scale) + shift
    h2 = h2s_ref[...]
    lo = pltpu.bitcast(h2[:, :half].astype(BF16).astype(F32), jnp.uint32)
    hi = pltpu.bitcast(h2[:, half:].astype(BF16).astype(F32), jnp.uint32)
    h2_ref[...] = (hi & jnp.uint32(0xFFFF0000)) | (lo >> 16)

    logits = jnp.dot(h2, wr_ref[...], precision=lax.Precision.HIGHEST,
                     preferred_element_type=F32)
    lt = logits.T[0:ne, :] + br_ref[...]
    eidx = lax.broadcasted_iota(jnp.int32, (ne, tm), 0)
    cur = lt
    tops, sels, hots = [], [], []
    for _ in range(TOP_K):
        mx = jnp.max(cur, axis=0, keepdims=True)
        sel = jnp.min(jnp.where(cur == mx, eidx, ne), axis=0, keepdims=True)
        hot = eidx == sel
        tops.append(mx)
        sels.append(sel)
        hots.append(hot)
        cur = jnp.where(hot, -jnp.inf, cur)
    ex = [jnp.exp(t - tops[0]) for t in tops]
    tot = ex[0] + ex[1] + ex[2] + ex[3]
    oh = jnp.zeros((ne, tm), F32)
    for hot in hots:
        oh = oh + hot.astype(F32)
    prefix = jnp.dot(oh.astype(BF16), ut_ref[...], preferred_element_type=F32)
    base = cnt_ref[:, 0:1]
    pos = base + prefix
    for kk in range(TOP_K):
        e_ref[kk:kk + 1, :] = sels[kk]
        gate_ref[kk:kk + 1, :] = ex[kk] / tot
        rank_ref[kk:kk + 1, :] = jnp.sum(jnp.where(hots[kk], pos, 0.0), axis=0,
                                         keepdims=True).astype(jnp.int32)
    cnt_ref[...] = cnt_ref[...] + jnp.sum(oh, axis=1, keepdims=True)


def _router(rows, x1, g_ffn, mod, w_router, b_router, tm=256):
    r, d = x1.shape
    ne = w_router.shape[1]
    tm = min(tm, r)
    assert r % tm == 0 and ne % 8 == 0 and ne <= LANES
    wr_pad = jnp.zeros((d, LANES), F32).at[:, :ne].set(w_router.astype(F32))
    tt = np.arange(tm)
    ut = jnp.asarray((tt[:, None] < tt[None, :]).astype(np.float32), BF16)
    return pl.pallas_call(
        functools.partial(_router_kernel, rows, tm, d, ne),
        out_shape=(jax.ShapeDtypeStruct((r, d // 2), jnp.uint32),
                   jax.ShapeDtypeStruct((TOP_K, r), jnp.int32),
                   jax.ShapeDtypeStruct((TOP_K, r), F32),
                   jax.ShapeDtypeStruct((TOP_K, r), jnp.int32),
                   jax.ShapeDtypeStruct((ne, LANES), F32)),
        grid=(r // tm,),
        in_specs=[pl.BlockSpec((tm, d), lambda i: (i, 0)),
                  pl.BlockSpec((1, d), lambda i: (0, 0)),
                  pl.BlockSpec(mod.shape, lambda i: (0, 0)),
                  pl.BlockSpec((d, LANES), lambda i: (0, 0)),
                  pl.BlockSpec((ne, 1), lambda i: (0, 0)),
                  pl.BlockSpec((tm, tm), lambda i: (0, 0))],
        out_specs=(pl.BlockSpec((tm, d // 2), lambda i: (i, 0)),
                   pl.BlockSpec((TOP_K, tm), lambda i: (0, i)),
                   pl.BlockSpec((TOP_K, tm), lambda i: (0, i)),
                   pl.BlockSpec((TOP_K, tm), lambda i: (0, i)),
                   pl.BlockSpec((ne, LANES), lambda i: (0, 0))),
        scratch_shapes=[pltpu.VMEM((tm, d), F32)],
        compiler_params=_cparams(1),
        name="router",
    )(x1, g_ffn.reshape(1, d), mod, wr_pad, b_router.reshape(ne, 1).astype(F32), ut)


def _invert_kernel(tm, pstart_ref, e_ref, rank_ref, rt_ref):
    i = pl.program_id(0)

    @pl.when(i == 0)
    def _():
        def clear(r, carry):
            rt_ref[r] = 0
            return carry

        lax.fori_loop(0, rt_ref.shape[0], clear, 0, unroll=8)

    def body(t, carry):
        for kk in range(TOP_K):
            rt_ref[pstart_ref[e_ref[kk, t]] + rank_ref[kk, t]] = i * tm + t
        return carry

    lax.fori_loop(0, tm, body, 0, unroll=4)


def _invert(e_idx, rank, pstart, n_rows, tm=256):
    r = e_idx.shape[1]
    tm = min(tm, r)
    assert r % tm == 0
    return pl.pallas_call(
        functools.partial(_invert_kernel, tm),
        out_shape=jax.ShapeDtypeStruct((n_rows,), jnp.int32),
        grid_spec=pltpu.PrefetchScalarGridSpec(
            num_scalar_prefetch=1,
            grid=(r // tm,),
            in_specs=[pl.BlockSpec((TOP_K, tm), lambda i, *_: (0, i), memory_space=pltpu.SMEM),
                      pl.BlockSpec((TOP_K, tm), lambda i, *_: (0, i), memory_space=pltpu.SMEM)],
            out_specs=pl.BlockSpec(memory_space=pltpu.SMEM)),
        compiler_params=_cparams(1),
        name="moe_invert",
    )(pstart, e_idx, rank)


def _unpack_rows(xw):
    lo = pltpu.bitcast(xw << 16, F32).astype(BF16)
    hi = pltpu.bitcast(xw & jnp.uint32(0xFFFF0000), F32).astype(BF16)
    return lo, hi


def _gather_kernel(nused_ref, rt_ref, h2_ref, xs_ref, buf_ref, sem):
    b = pl.program_id(0)
    half = buf_ref.shape[1]

    def row_copy(src_row, dst_row):
        return pltpu.make_async_copy(h2_ref.at[pl.ds(src_row, 1), :],
                                     buf_ref.at[pl.ds(dst_row, 1), :], sem)

    @pl.when(b < nused_ref[0])
    def _():
        def issue(t, carry):
            row_copy(rt_ref[0, 0, t], t).start()
            return carry

        lax.fori_loop(0, MOE_BLOCK, issue, 0, unroll=8)

        def drain(t, carry):
            row_copy(0, 0).wait()
            return carry

        lax.fori_loop(0, MOE_BLOCK, drain, 0, unroll=8)
        lo, hi = _unpack_rows(buf_ref[...])
        xs_ref[:, 0:half] = lo
        xs_ref[:, half:] = hi

    @pl.when(b >= nused_ref[0])
    def _():
        xs_ref[...] = jnp.zeros_like(xs_ref)


def _gather_rows(h2p, row_tok, n_used):
    r, half = h2p.shape
    n_blocks = row_tok.shape[0] // MOE_BLOCK
    return pl.pallas_call(
        _gather_kernel,
        out_shape=jax.ShapeDtypeStruct((n_blocks * MOE_BLOCK, 2 * half), BF16),
        grid_spec=pltpu.PrefetchScalarGridSpec(
            num_scalar_prefetch=1,
            grid=(n_blocks,),
            in_specs=[pl.BlockSpec((1, 1, MOE_BLOCK), lambda b, *_: (b, 0, 0), memory_space=pltpu.SMEM),
                      pl.BlockSpec(memory_space=pl.ANY)],
            out_specs=pl.BlockSpec((MOE_BLOCK, 2 * half), lambda b, *_: (b, 0)),
            scratch_shapes=[pltpu.VMEM((MOE_BLOCK, half), jnp.uint32),
                            pltpu.SemaphoreType.DMA]),
        compiler_params=_cparams(1),
        name="moe_gather",
    )(n_used.reshape(1), row_tok.reshape(n_blocks, 1, MOE_BLOCK), h2p)


MOE_CACHE_BLOCKS = 6


class _ExpertRows:
    def __init__(self, src_hbm, slots_ref, sems, b0, nb, first_tile):
        self.src, self.slots, self.sems = src_hbm, slots_ref, sems
        self.b0, self.nb, self.first_tile = b0, nb, first_tile

    def slot_of(self, i):
        return jnp.where(i < MOE_CACHE_BLOCKS, i, MOE_CACHE_BLOCKS + i % 2)

    def _needs_load(self, i):
        return jnp.logical_or(self.first_tile, i >= MOE_CACHE_BLOCKS)

    def _copy(self, i):
        row0 = pl.multiple_of((self.b0 + i) * MOE_BLOCK, MOE_BLOCK)
        return pltpu.make_async_copy(self.src.at[pl.ds(row0, MOE_BLOCK), :],
                                     self.slots.at[self.slot_of(i)], self.sems.at[i % 2])

    def request(self, i):
        @pl.when(jnp.logical_and(i < self.nb, self._needs_load(i)))
        def _():
            self._copy(i).start()

    def arrive(self, i):
        @pl.when(self._needs_load(i))
        def _():
            self._copy(i).wait()


def _block_window(dst_hbm, blk, col0, width):
    row0 = pl.multiple_of(blk * MOE_BLOCK, MOE_BLOCK)
    return dst_hbm.at[pl.ds(row0, MOE_BLOCK), pl.ds(pl.multiple_of(col0, width), width)]


def _zero_tail_blocks(stage_ref, dst_hbm, sem, first_blk, col0, width):
    n_blocks = dst_hbm.shape[0] // MOE_BLOCK
    stage_ref[0] = jnp.zeros(stage_ref.shape[1:], stage_ref.dtype)

    def cp(blk):
        return pltpu.make_async_copy(stage_ref.at[0], _block_window(dst_hbm, blk, col0, width), sem.at[0])

    def start(blk, carry):
        cp(blk).start()
        return carry

    def wait(blk, carry):
        cp(blk).wait()
        return carry

    lax.fori_loop(first_blk, n_blocks, start, 0)
    lax.fori_loop(first_blk, n_blocks, wait, 0)


def _moe_up_kernel(tf, nj, ne, bstart_ref, nblk_ref, xs_hbm, w_ref, b_ref, act_hbm,
                   wbf_ref, xb_ref, gt_ref, ostage_ref, xsem, osem):
    g = pl.program_id(0)
    e, j = g // nj, g % nj
    nb, b0 = nblk_ref[e], bstart_ref[e]
    nslab = MOE_BLOCK // LANES
    xrows = _ExpertRows(xs_hbm, xb_ref, xsem, b0, nb, j == 0)

    def out_copy(i):
        return pltpu.make_async_copy(ostage_ref.at[i % 2],
                                     _block_window(act_hbm, b0 + i, j * tf, tf), osem.at[i % 2])

    def up_dot(i):
        return jnp.dot(xb_ref[xrows.slot_of(i)], wbf_ref[...], preferred_element_type=F32) + b_ref[0]

    def swiglu(gu):
        gut = gu.T
        for sl in range(nslab):
            gt_ref[sl] = gut[:, sl * LANES:(sl + 1) * LANES]
        parts = []
        for sl in range(nslab):
            gate = jnp.minimum(gt_ref[sl, pl.ds(0, tf, stride=2), :], SWIGLU_LIMIT)
            up = jnp.clip(gt_ref[sl, pl.ds(1, tf, stride=2), :], -SWIGLU_LIMIT, SWIGLU_LIMIT)
            parts.append((up + 1.0) * gate * _sigmoid(SWIGLU_ALPHA * gate))
        return jnp.concatenate(parts, axis=1).T.astype(BF16)

    def finish(gu, i):
        act = swiglu(gu)
        ostage_ref[i % 2] = act
        out_copy(i).start()

    @pl.when(nb > 0)
    def _():
        xrows.request(0)
        wbf_ref[...] = w_ref[0].astype(BF16)
        xrows.arrive(0)
        xrows.request(1)
        gu0 = up_dot(0)

        def body(i, gu_prev):
            xrows.arrive(i)
            xrows.request(i + 1)

            @pl.when(i >= 3)
            def _():
                out_copy(i - 3).wait()

            gu = up_dot(i)
            finish(gu_prev, i - 1)
            return gu

        gu_last = lax.fori_loop(1, nb, body, gu0)

        @pl.when(nb >= 3)
        def _():
            out_copy(nb - 3).wait()

        finish(gu_last, nb - 1)

        @pl.when(nb >= 2)
        def _():
            out_copy(nb - 2).wait()

        out_copy(nb - 1).wait()

    @pl.when(e == ne - 1)
    def _():
        _zero_tail_blocks(ostage_ref, act_hbm, osem, b0 + nb, j * tf, tf)


def _moe_up(xs, w_gu, b_gu, bstart, nblk, tf=256):
    n_rows, d = xs.shape
    ne, _, f2 = w_gu.shape
    f = f2 // 2
    tf = min(tf, f)
    assert f % tf == 0 and w_gu.shape[1] == d
    nj = f // tf
    return pl.pallas_call(
        functools.partial(_moe_up_kernel, tf, nj, ne),
        out_shape=jax.ShapeDtypeStruct((n_rows, f), BF16),
        grid_spec=pltpu.PrefetchScalarGridSpec(
            num_scalar_prefetch=2,
            grid=(ne * nj,),
            in_specs=[pl.BlockSpec(memory_space=pl.ANY),
                      pl.BlockSpec((1, d, 2 * tf), lambda g, *_: (g // nj, 0, g % nj)),
                      pl.BlockSpec((1, 1, 2 * tf), lambda g, *_: (g // nj, 0, g % nj))],
            out_specs=pl.BlockSpec(memory_space=pl.ANY),
            scratch_shapes=[pltpu.VMEM((d, 2 * tf), BF16),
                            pltpu.VMEM((MOE_CACHE_BLOCKS + 2, MOE_BLOCK, d), BF16),
                            pltpu.VMEM((MOE_BLOCK // LANES, 2 * tf, LANES), F32),
                            pltpu.VMEM((2, MOE_BLOCK, tf), BF16),
                            pltpu.SemaphoreType.DMA((2,)),
                            pltpu.SemaphoreType.DMA((2,))]),
        compiler_params=_cparams(1),
        name="moe_up",
    )(bstart, nblk, xs, w_gu, b_gu.reshape(ne, 1, f2))


def _moe_down_kernel(tn, nj, ne, bstart_ref, nblk_ref, act_hbm, w_ref, b_ref, yb_hbm,
                     wbf_ref, ab_ref, ostage_ref, asem, osem):
    g = pl.program_id(0)
    e, j = g // nj, g % nj
    nb, b0 = nblk_ref[e], bstart_ref[e]
    arows = _ExpertRows(act_hbm, ab_ref, asem, b0, nb, j == 0)

    def out_copy(i):
        return pltpu.make_async_copy(ostage_ref.at[i % 2],
                                     _block_window(yb_hbm, b0 + i, j * tn, tn), osem.at[i % 2])

    @pl.when(nb > 0)
    def _():
        arows.request(0)
        wbf_ref[...] = w_ref[0].astype(BF16)

        def body(i, carry):
            arows.arrive(i)
            arows.request(i + 1)

            @pl.when(i >= 2)
            def _():
                out_copy(i - 2).wait()

            ostage_ref[i % 2] = jnp.dot(ab_ref[arows.slot_of(i)], wbf_ref[...],
                                        preferred_element_type=F32) + b_ref[0]
            out_copy(i).start()
            return carry

        lax.fori_loop(0, nb, body, 0)

        @pl.when(nb >= 2)
        def _():
            out_copy(nb - 2).wait()

        out_copy(nb - 1).wait()

    @pl.when(e == ne - 1)
    def _():
        _zero_tail_blocks(ostage_ref, yb_hbm, osem, b0 + nb, j * tn, tn)


def _moe_down(act, w_d, b_d, bstart, nblk, tn=512):
    n_rows, f = act.shape
    ne, _, d = w_d.shape
    tn = min(tn, d)
    assert d % tn == 0
    nj = d // tn
    return pl.pallas_call(
        functools.partial(_moe_down_kernel, tn, nj, ne),
        out_shape=jax.ShapeDtypeStruct((n_rows, d), F32),
        grid_spec=pltpu.PrefetchScalarGridSpec(
            num_scalar_prefetch=2,
            grid=(ne * nj,),
            in_specs=[pl.BlockSpec(memory_space=pl.ANY),
                      pl.BlockSpec((1, f, tn), lambda g, *_: (g // nj, 0, g % nj)),
                      pl.BlockSpec((1, 1, tn), lambda g, *_: (g // nj, 0, g % nj))],
            out_specs=pl.BlockSpec(memory_space=pl.ANY),
            scratch_shapes=[pltpu.VMEM((f, tn), BF16),
                            pltpu.VMEM((MOE_CACHE_BLOCKS + 2, MOE_BLOCK, f), BF16),
                            pltpu.VMEM((2, MOE_BLOCK, tn), F32),
                            pltpu.SemaphoreType.DMA((2,)),
                            pltpu.SemaphoreType.DMA((2,))]),
        compiler_params=_cparams(1),
        name="moe_down",
    )(bstart, nblk, act, w_d, b_d.reshape(ne, 1, d))


def _combine_kernel(rows, tm, d, pstart_ref, e_ref, rank_ref, gt_ref, x_ref, mod_ref, gf_ref,
                    yb_ref, yp_ref, ys_ref, buf_ref, sem):
    i = pl.program_id(0)
    is_prompt = i < rows.groups_p * CHUNK // tm

    def row_copy(dst_k, dst_t, src_row):
        return pltpu.make_async_copy(yb_ref.at[pl.ds(src_row, 1), :],
                                     buf_ref.at[dst_k, pl.ds(dst_t, 1), :], sem)

    def issue(t, carry):
        for kk in range(TOP_K):
            src = pstart_ref[e_ref[kk, t]] + rank_ref[kk, t]
            row_copy(kk, t, src).start()
        return carry

    lax.fori_loop(0, tm, issue, 0)

    def drain(t, carry):
        for kk in range(TOP_K):
            row_copy(0, 0, 0).wait()
        return carry

    lax.fori_loop(0, tm, drain, 0)

    gf = gf_ref[...]
    for gi in range(tm // CHUNK):
        rs = slice(gi * CHUNK, (gi + 1) * CHUNK)
        seq = rows.seq_of_group(i * (tm // CHUNK) + gi)
        gate2 = mod_ref[pl.ds(seq, 1), :]
        ff = jnp.zeros((CHUNK, d), F32)
        for kk in range(TOP_K):
            ff = ff + buf_ref[kk, rs, :] * gt_ref[rs, kk:kk + 1]
        x2 = x_ref[rs, :] + gate2 * ff
        ms = jnp.mean(x2 * x2, axis=-1, keepdims=True)
        y = x2 * lax.rsqrt(ms + RMS_EPS) * gf

        @pl.when(is_prompt)
        def _():
            yp_ref[rs, :] = y

        @pl.when(jnp.logical_not(is_prompt))
        def _():
            ys_ref[rs, :] = y


def _combine(rows, x1, mod, g_final, yb, e_idx, rank, gates_t, pstart, tm=128):
    r, d = x1.shape
    tm = min(tm, r)
    assert r % tm == 0
    gate2_blk = 5
    yp_spec, ys_spec = _split_row_specs(rows, tm, d)
    return pl.pallas_call(
        functools.partial(_combine_kernel, rows, tm, d),
        out_shape=(jax.ShapeDtypeStruct((rows.groups_p * CHUNK, d), F32),
                   jax.ShapeDtypeStruct((r - rows.groups_p * CHUNK, d), F32)),
        grid_spec=pltpu.PrefetchScalarGridSpec(
            num_scalar_prefetch=1,
            grid=(r // tm,),
            in_specs=[pl.BlockSpec((TOP_K, tm), lambda i, *_: (0, i), memory_space=pltpu.SMEM),
                      pl.BlockSpec((TOP_K, tm), lambda i, *_: (0, i), memory_space=pltpu.SMEM),
                      pl.BlockSpec((tm, TOP_K), lambda i, *_: (i, 0)),
                      pl.BlockSpec((tm, d), lambda i, *_: (i, 0)),
                      pl.BlockSpec((mod.shape[0], d), lambda i, *_: (0, gate2_blk)),
                      pl.BlockSpec((1, d), lambda i, *_: (0, 0)),
                      pl.BlockSpec(memory_space=pl.ANY)],
            out_specs=(yp_spec, ys_spec),
            scratch_shapes=[pltpu.VMEM((TOP_K, tm, d), F32),
                            pltpu.SemaphoreType.DMA]),
        compiler_params=_cparams(1),
        name="moe_combine",
    )(pstart, e_idx, rank, gates_t, x1, mod, g_final.reshape(1, d), yb)


def kernel(x_prompt, x_sample, c_prompt, c_sample, state_hgrn_S, state_conv, state_mlstm_C,
           state_mlstm_n, state_mlstm_m, w_ada, b_ada, g_mix, g_ffn, w_in, lb_logits, conv_w,
           conv_b, b_igate, b_fgate, g_hgrn, g_mlstm, w_out, w_router, b_router, w_gate_up,
           b_gate_up, w_down, b_down, g_final):
    depth = w_ada.shape[0]
    assert depth == 1, "single-layer trunk"
    bp, tp, d = x_prompt.shape
    bs, ts, _ = x_sample.shape
    rows = _Rows(bp, tp, bs, ts)
    _, _, nh_h, dk, dv = state_hgrn_S.shape
    _, _, nh_m, dqk, dvm = state_mlstm_C.shape
    hw = nh_h * dk
    mw = nh_m * dvm
    n_main = 4 * hw + 3 * mw
    assert w_in.shape[2] == n_main + 2 * nh_m and 2 * nh_m <= GATE_LANES
    assert mw == hw, "column sections of the input projection are addressed in hw-wide blocks"
    ne = w_router.shape[2]

    lb = jax.nn.softmax(lb_logits.astype(F32), axis=0)[0]

    x_p = x_prompt.reshape(bp * tp, d)
    x_s = x_sample.reshape(bs * ts, d)
    n_c = bp + bs
    n_c_pad = -(-n_c // 8) * 8
    c_pad = jnp.zeros((n_c_pad, d), F32).at[:n_c].set(jnp.concatenate([c_prompt, c_sample], axis=0))
    mod = _ada(c_pad, w_ada[0], b_ada[0])

    w_in0 = w_in[0]
    wg_pad = jnp.zeros((d, GATE_LANES), F32).at[:, :2 * nh_m].set(w_in0[:, n_main:])
    h_all, gates = _prep(rows, x_p, x_s, g_mix[0], mod, wg_pad)
    z = _inproj(h_all, w_in0, n_main)

    og, st = _hgrn(rows, z, state_hgrn_S[0], lb, g_hgrn[0])
    hm, conv_new, c_new, n_new, m_new = _mlstm(
        rows, z, 4, gates, state_conv[0], state_mlstm_C[0], state_mlstm_n[0], state_mlstm_m[0],
        conv_w[0], conv_b[0], b_igate[0], b_fgate[0], g_mlstm[0])

    x1 = _outproj(rows, og, hm, w_out[0], x_p, x_s, mod)

    h2p, e_idx, gate_k, rank, cnt = _router(rows, x1, g_ffn[0], mod, w_router[0], b_router[0])

    counts = cnt[:, 0].astype(jnp.int32)
    nblk_e = (counts + MOE_BLOCK - 1) // MOE_BLOCK
    padded = nblk_e * MOE_BLOCK
    pad_end = jnp.cumsum(padded)
    pstart = jnp.concatenate([jnp.zeros((1,), jnp.int32), pad_end]).astype(jnp.int32)
    n_blocks_max = -(-(rows.rows * TOP_K) // MOE_BLOCK) + ne
    n_rows = n_blocks_max * MOE_BLOCK
    blk_start_e = (pstart[:ne] // MOE_BLOCK).astype(jnp.int32)
    n_used = (pstart[ne] // MOE_BLOCK).astype(jnp.int32)

    row_tok = _invert(e_idx, rank, pstart, n_rows)
    xs = _gather_rows(h2p, row_tok, n_used)
    act = _moe_up(xs, w_gate_up[0], b_gate_up[0], blk_start_e, nblk_e.astype(jnp.int32))
    yb = _moe_down(act, w_down[0], b_down[0], blk_start_e, nblk_e.astype(jnp.int32))

    y_p, y_s = _combine(rows, x1, mod, g_final, yb, e_idx, rank, gate_k.T, pstart)

    y_prompt = y_p.reshape(bp, tp, d)
    y_sample = y_s.reshape(bs, ts, d)
    s_all = jnp.swapaxes(st, 2, 3)
    m_all = m_new.reshape(rows.nseq, nh_m)

    def split(a):
        return a[:bp][None], a[bp:][None]

    p_s, s_s = split(s_all)
    p_conv, s_conv = split(conv_new)
    p_c, s_c = split(c_new)
    p_n, s_n = split(n_new)
    p_m, s_m = split(m_all)
    return (y_prompt, y_sample, p_s, p_conv, p_c, p_n, p_m, s_s, s_conv, s_c, s_n, s_m)
```

```python
import functools

import numpy as np
import jax
import jax.numpy as jnp
from jax import lax
from jax.experimental import pallas as pl
from jax.experimental.pallas import tpu as pltpu

F32 = jnp.float32
BF16 = jnp.bfloat16

CHUNK = 64
TOP_K = 4
CONV_K = 4
RMS_EPS = 1e-6
SWIGLU_LIMIT = 7.0
SWIGLU_ALPHA = 1.702
MOE_BLOCK = 256
LANES = 128
GATE_LANES = 128
VMEM_LIMIT = 56 * 1024 * 1024
HGRN_LEVELS = (32, 16, 8, 4, 2, 1)


def _cparams(n_axes, vmem=VMEM_LIMIT):
    return pltpu.CompilerParams(dimension_semantics=("arbitrary",) * n_axes,
                                vmem_limit_bytes=vmem)


def _sigmoid(x):
    return 1.0 / (1.0 + jnp.exp(-x))


def _silu(x):
    return x * _sigmoid(x)


class _Rows:
    def __init__(self, bp, tp, bs, ts):
        assert tp % CHUNK == 0 and ts % CHUNK == 0
        self.bp, self.tp, self.bs, self.ts = bp, tp, bs, ts
        self.nblk_p, self.nblk_s = tp // CHUNK, ts // CHUNK
        self.groups_p = bp * self.nblk_p
        self.groups = self.groups_p + bs * self.nblk_s
        self.rows = self.groups * CHUNK
        self.nseq = bp + bs

    def seq_of_group(self, g):
        return jnp.where(g < self.groups_p, g // self.nblk_p,
                         self.bp + (g - self.groups_p) // self.nblk_s)

    def blk_in_seq(self, g):
        return jnp.where(g < self.groups_p, g % self.nblk_p, (g - self.groups_p) % self.nblk_s)

    def nblk_of_group(self, g):
        return jnp.where(g < self.groups_p, self.nblk_p, self.nblk_s)


def _ada_kernel(c_ref, w_ref, b_ref, o_ref):
    c = c_ref[...]
    s = _silu(c).astype(BF16)
    o_ref[...] = jnp.dot(s, w_ref[...].astype(BF16), preferred_element_type=F32) + b_ref[...]


def _ada(c_pad, w_ada, b_ada, tn=512):
    m, d = c_pad.shape
    n = w_ada.shape[1]
    tn = min(tn, n)
    assert n % tn == 0
    return pl.pallas_call(
        _ada_kernel,
        out_shape=jax.ShapeDtypeStruct((m, n), F32),
        grid=(n // tn,),
        in_specs=[pl.BlockSpec((m, d), lambda j: (0, 0)),
                  pl.BlockSpec((d, tn), lambda j: (0, j)),
                  pl.BlockSpec((1, tn), lambda j: (0, j))],
        out_specs=pl.BlockSpec((m, tn), lambda j: (0, j)),
        compiler_params=_cparams(1),
        name="ada_mod",
    )(c_pad, w_ada, b_ada.reshape(1, n))


def _prep_kernel(rows, tm, d, n_gate, xp_ref, xs_ref, g_ref, mod_ref, wtail_ref,
                 h_ref, gates_ref, wg_ref):
    i = pl.program_id(0)
    is_prompt = i < rows.groups_p * CHUNK // tm

    @pl.when(i == 0)
    def _():
        lane = lax.broadcasted_iota(jnp.int32, wg_ref.shape, 1)
        wg_ref[...] = jnp.where(lane < n_gate, wtail_ref[0], 0.0)

    g = g_ref[...]
    for gi in range(tm // CHUNK):
        seq = rows.seq_of_group(i * (tm // CHUNK) + gi)
        m = mod_ref[pl.ds(seq, 1), :]
        shift, scale = m[:, 0:d], m[:, d:2 * d]
        rs = slice(gi * CHUNK, (gi + 1) * CHUNK)
        x = jnp.where(is_prompt, xp_ref[rs, :], xs_ref[rs, :])
        ms = jnp.mean(x * x, axis=-1, keepdims=True)
        y = x * lax.rsqrt(ms + RMS_EPS) * g
        h = y * (1.0 + scale) + shift
        h_ref[gi * CHUNK:(gi + 1) * CHUNK, :] = h.astype(BF16)
        gates_ref[gi * CHUNK:(gi + 1) * CHUNK, :] = jnp.dot(
            h, wg_ref[...], precision=lax.Precision.HIGHEST, preferred_element_type=F32)


def _split_row_specs(rows, tm, tn, col_major=False):
    n_pt = rows.groups_p * CHUNK // tm
    n_st = rows.rows // tm - n_pt
    assert n_pt * tm == rows.groups_p * CHUNK and n_st >= 1

    def ij(args):
        return (args[1], args[0]) if col_major else (args[0], 0)

    def p_map(*args):
        i, j = ij(args)
        return (jnp.minimum(i, n_pt - 1), j)

    def s_map(*args):
        i, j = ij(args)
        return (jnp.maximum(i - n_pt, 0), j)

    return pl.BlockSpec((tm, tn), p_map), pl.BlockSpec((tm, tn), s_map)


def _prep(rows, x_p, x_s, g_mix, mod, w_in, n_main, tm=256):
    d = x_p.shape[1]
    r = rows.rows
    tm = min(tm, x_s.shape[0])
    n_gate = w_in.shape[2] - n_main
    assert r % tm == 0 and n_main % LANES == 0 and n_gate <= GATE_LANES
    xp_spec, xs_spec = _split_row_specs(rows, tm, d)
    return pl.pallas_call(
        functools.partial(_prep_kernel, rows, tm, d, n_gate),
        out_shape=(jax.ShapeDtypeStruct((r, d), BF16),
                   jax.ShapeDtypeStruct((r, GATE_LANES), F32)),
        grid=(r // tm,),
        in_specs=[xp_spec, xs_spec,
                  pl.BlockSpec((1, d), lambda i: (0, 0)),
                  pl.BlockSpec(mod.shape, lambda i: (0, 0)),
                  pl.BlockSpec((1, d, GATE_LANES), lambda i: (0, 0, n_main // GATE_LANES))],
        out_specs=(pl.BlockSpec((tm, d), lambda i: (i, 0)),
                   pl.BlockSpec((tm, GATE_LANES), lambda i: (i, 0))),
        scratch_shapes=[pltpu.VMEM((d, GATE_LANES), F32)],
        compiler_params=_cparams(1),
        name="prep_norm_mod",
    )(x_p, x_s, g_mix.reshape(1, d), mod, w_in)


def _inproj_kernel(h_ref, w_ref, z_ref, wb_ref):
    @pl.when(pl.program_id(1) == 0)
    def _():
        wb_ref[...] = w_ref[0].astype(BF16)

    z_ref[...] = jnp.dot(h_ref[...], wb_ref[...], preferred_element_type=F32)


def _inproj(h_all, w_in, n_main, tm=1024, tn=512):
    r, d = h_all.shape
    tm, tn = min(tm, r), min(tn, n_main)
    assert r % tm == 0 and n_main % tn == 0
    return pl.pallas_call(
        _inproj_kernel,
        out_shape=jax.ShapeDtypeStruct((r, n_main), F32),
        grid=(n_main // tn, r // tm),
        in_specs=[pl.BlockSpec((tm, d), lambda j, i: (i, 0)),
                  pl.BlockSpec((1, d, tn), lambda j, i: (0, 0, j))],
        out_specs=pl.BlockSpec((tm, tn), lambda j, i: (i, j)),
        scratch_shapes=[pltpu.VMEM((d, tn), BF16)],
        compiler_params=_cparams(2),
        name="in_proj",
    )(h_all, w_in)


def _hgrn_consts():
    L = CHUNK
    t = np.arange(L)[:, None]
    s = np.arange(L)[None, :]
    mats = [(s <= t)]
    masks = []
    for m in HGRN_LEVELS:
        start = (t // (2 * m)) * (2 * m)
        mats.append(s <= start + m - 1)
        masks.append((t // (2 * m) == s // (2 * m)) & (t % (2 * m) >= m) & (s % (2 * m) < m))
    mstack = np.concatenate(mats, axis=0).astype(np.float32)
    masks = np.stack(masks, axis=0).astype(np.float32)
    return jnp.asarray(mstack, BF16), jnp.asarray(masks, F32)


def _split3(x):
    x1 = x.astype(BF16)
    r1 = x - x1.astype(F32)
    x2 = r1.astype(BF16)
    x3 = (r1 - x2.astype(F32)).astype(BF16)
    return x1, x2, x3


def _hgrn_kernel(rows, nh, dk, dv, hq_ref, hf_ref, hi_ref, hg_ref, s0_ref, lb_ref, gn_ref,
                 mstack_ref, masks_ref, og_ref, st_ref, r_ref, o_ref):
    L = CHUNK
    g = pl.program_id(0)
    blk = rows.blk_in_seq(g)
    is_prompt = g < rows.groups_p

    @pl.when(jnp.logical_and(blk == 0, is_prompt))
    def _():
        st_ref[...] = jnp.zeros_like(st_ref)

    @pl.when(jnp.logical_and(blk == 0, jnp.logical_not(is_prompt)))
    def _():
        for h in range(nh):
            st_ref[0, h] = s0_ref[0, h].T

    lb = lb_ref[...]
    f = lb + (1.0 - lb) * _sigmoid(hf_ref[...])
    lf = jnp.log(f)
    mstack = mstack_ref[...]
    p1, p2, p3 = _split3(lf)
    r_ref[...] = (jnp.dot(mstack, p1, preferred_element_type=F32)
                  + jnp.dot(mstack, p2, preferred_element_type=F32)
                  + jnp.dot(mstack, p3, preferred_element_type=F32))

    for h in range(nh):
        cs = slice(h * dk, (h + 1) * dk)
        vs = slice(h * dv, (h + 1) * dv)
        q = _silu(hq_ref[:, cs]) * (dk ** -0.5)
        k = 1.0 - f[:, cs]
        v = hi_ref[:, vs]
        vb = v.astype(BF16)
        b = r_ref[0:L, cs]
        st = st_ref[0, h]
        qe = (q * jnp.exp(b)).astype(BF16)
        o = lax.dot_general(qe, st.astype(BF16), (((1,), (1,)), ((), ())),
                            preferred_element_type=F32)
        a = jnp.zeros((L, L), F32)
        for li in range(len(HGRN_LEVELS)):
            rl = r_ref[(li + 1) * L:(li + 2) * L, cs]
            e = jnp.exp(-jnp.abs(b - rl))
            al = lax.dot_general((q * e).astype(BF16), (k * e).astype(BF16),
                                 (((1,), (1,)), ((), ())), preferred_element_type=F32)
            a = a + al * masks_ref[li]
        diag = jnp.sum(q * k, axis=-1, keepdims=True)
        o = o + jnp.dot(a.astype(BF16), vb, preferred_element_type=F32) + diag * v
        o_ref[:, vs] = o
        b_last = b[L - 1:L, :]
        kd = (k * jnp.exp(b_last - b)).astype(BF16)
        st_ref[0, h] = st * jnp.exp(b_last) + lax.dot_general(
            vb, kd, (((0,), (0,)), ((), ())), preferred_element_type=F32)

    o = o_ref[...]
    ms = jnp.mean(o * o, axis=-1, keepdims=True)
    og = o * lax.rsqrt(ms + RMS_EPS) * gn_ref[...] * _silu(hg_ref[...])
    og_ref[...] = og.astype(BF16)


def _hgrn(rows, z, s0, lb, g_hgrn):
    bs, nh, dk, dv = s0.shape
    hw = nh * dk
    assert nh * dv == hw and hw % LANES == 0
    mstack, masks = _hgrn_consts()
    n_lv = len(HGRN_LEVELS)
    L = CHUNK

    def zspec(sec):
        return pl.BlockSpec((L, hw), lambda g, sec=sec: (g, sec))

    og, st = pl.pallas_call(
        functools.partial(_hgrn_kernel, rows, nh, dk, dv),
        out_shape=(jax.ShapeDtypeStruct((rows.rows, hw), BF16),
                   jax.ShapeDtypeStruct((rows.nseq, nh, dv, dk), F32)),
        grid=(rows.groups,),
        in_specs=[zspec(0), zspec(1), zspec(2), zspec(3),
                  pl.BlockSpec((1, nh, dk, dv),
                               lambda g: (jnp.maximum(rows.seq_of_group(g) - rows.bp, 0), 0, 0, 0)),
                  pl.BlockSpec((1, hw), lambda g: (0, 0)),
                  pl.BlockSpec((1, hw), lambda g: (0, 0)),
                  pl.BlockSpec(mstack.shape, lambda g: (0, 0)),
                  pl.BlockSpec(masks.shape, lambda g: (0, 0, 0))],
        out_specs=(pl.BlockSpec((L, hw), lambda g: (g, 0)),
                   pl.BlockSpec((1, nh, dv, dk), lambda g: (rows.seq_of_group(g), 0, 0, 0))),
        scratch_shapes=[pltpu.VMEM(((n_lv + 1) * L, hw), F32),
                        pltpu.VMEM((L, hw), F32)],
        compiler_params=_cparams(1),
        name="hgrn2",
    )(z, z, z, z, s0, lb.reshape(1, hw), g_hgrn.reshape(1, hw), mstack, masks)
    return og, st


def _log_sigmoid(x):
    return jnp.minimum(x, 0.0) - jnp.log(1.0 + jnp.exp(-jnp.abs(x)))


def _mlstm_kernel(rows, nh, dqk, dv, mqk_ref, mv_ref, mo_ref, gates_ref, conv0_ref, c0_ref,
                  n0_ref, m0_ref, cw_ref, cb_ref, gbias_ref, gn_ref,
                  hm_ref, conv_ref, c_ref, n_ref, m_ref, ubuf_ref):
    L = CHUNK
    kq = nh * dqk
    pad = 8
    g = pl.program_id(0)
    blk = rows.blk_in_seq(g)
    is_prompt = g < rows.groups_p

    @pl.when(jnp.logical_and(blk == 0, is_prompt))
    def _():
        ubuf_ref[0:pad, :] = jnp.zeros((pad, 2 * kq), F32)
        c_ref[...] = jnp.zeros_like(c_ref)
        n_ref[...] = jnp.zeros_like(n_ref)
        m_ref[...] = jnp.zeros_like(m_ref)

    @pl.when(jnp.logical_and(blk == 0, jnp.logical_not(is_prompt)))
    def _():
        ubuf_ref[0:pad, :] = jnp.zeros((pad, 2 * kq), F32)
        ubuf_ref[pad - (CONV_K - 1):pad, :] = conv0_ref[0]
        c_ref[...] = c0_ref[...]
        n_ref[...] = n0_ref[...]
        m_ref[...] = m0_ref[...]

    ubuf_ref[pad:pad + L, :] = mqk_ref[...]
    acc = cb_ref[...] + jnp.zeros((L, 2 * kq), F32)
    for j in range(CONV_K):
        off = pad - (CONV_K - 1) + j
        acc = acc + ubuf_ref[off:off + L, :] * cw_ref[j:j + 1, :]
    new_tail = ubuf_ref[pad + L - (CONV_K - 1):pad + L, :]
    conv_ref[0] = new_tail
    ubuf_ref[pad - (CONV_K - 1):pad, :] = new_tail
    qk = _silu(acc)

    gt = gates_ref[...] + gbias_ref[...]
    lane = lax.broadcasted_iota(jnp.int32, gt.shape, 1)
    pg = jnp.where(lane < nh, gt, _log_sigmoid(gt))
    pgt = pg.T
    ti = lax.broadcasted_iota(jnp.int32, (L, L), 0)
    si = lax.broadcasted_iota(jnp.int32, (L, L), 1)
    tri = si <= ti

    for h in range(nh):
        q = qk[:, h * dqk:(h + 1) * dqk]
        k = qk[:, kq + h * dqk:kq + (h + 1) * dqk] * (dqk ** -0.5)
        v = mv_ref[:, h * dv:(h + 1) * dv]
        qb, kb, vb = q.astype(BF16), k.astype(BF16), v.astype(BF16)
        ig_c, lf_c = pg[:, h:h + 1], pg[:, nh + h:nh + h + 1]
        ig_r, lf_r = pgt[h:h + 1, :], pgt[nh + h:nh + h + 1, :]
        b_c = jnp.sum(jnp.where(tri, lf_r, 0.0), axis=1, keepdims=True)
        b_r = jnp.sum(jnp.where(ti <= si, lf_c, 0.0), axis=0, keepdims=True)
        m_prev = m_ref[0, :, h:h + 1]
        log_d = jnp.where(tri, b_c - b_r + ig_r, -jnp.inf)
        log_inter = b_c + m_prev
        m_t = jnp.maximum(jnp.max(log_d, axis=1, keepdims=True), log_inter)
        dm = jnp.exp(log_d - m_t)
        s_mat = lax.dot_general(qb, kb, (((1,), (1,)), ((), ())), preferred_element_type=F32) * dm
        w_inter = jnp.exp(log_inter - m_t)
        c_h = c_ref[0, h]
        n_h = n_ref[0, h:h + 1, :]
        num = (jnp.dot(s_mat.astype(BF16), vb, preferred_element_type=F32)
               + w_inter * jnp.dot(qb, c_h.astype(BF16), preferred_element_type=F32))
        den = (jnp.sum(s_mat, axis=1, keepdims=True)
               + w_inter * jnp.sum(q * n_h, axis=1, keepdims=True))
        hh = num / jnp.maximum(jnp.abs(den), jnp.exp(-m_t))
        b_last = b_c[L - 1:L, :]
        lw_c = b_last - b_c + ig_c
        lw_r = b_last - b_r + ig_r
        m_new = jnp.maximum(b_last + m_prev, jnp.max(lw_r, axis=1, keepdims=True))
        decay = jnp.exp(b_last + m_prev - m_new)
        kw = k * jnp.exp(lw_c - m_new)
        c_ref[0, h] = decay * c_h + lax.dot_general(
            kw.astype(BF16), vb, (((0,), (0,)), ((), ())), preferred_element_type=F32)
        n_ref[0, h:h + 1, :] = decay * n_h + jnp.sum(kw, axis=0, keepdims=True)
        m_ref[0, :, h:h + 1] = m_new
        ms = jnp.mean(hh * hh, axis=-1, keepdims=True)
        hn = hh * lax.rsqrt(ms + RMS_EPS) * gn_ref[:, h * dv:(h + 1) * dv]
        hm_ref[:, h * dv:(h + 1) * dv] = (hn * _sigmoid(mo_ref[:, h * dv:(h + 1) * dv])).astype(BF16)


def _mlstm(rows, z, sec_qk, gates, conv0, c0, n0, m0, conv_w, conv_b, b_igate, b_fgate, g_mlstm):
    bs, nh, dqk, dv = c0.shape
    kq = nh * dqk
    mw = nh * dv
    L = CHUNK
    assert 2 * kq == mw, "q/k conv width must equal the value width for the column sections"
    gbias = jnp.zeros((1, GATE_LANES), F32)
    gbias = gbias.at[0, 0:nh].set(b_igate.astype(F32)).at[0, nh:2 * nh].set(b_fgate.astype(F32))

    def sseq(g):
        return jnp.maximum(rows.seq_of_group(g) - rows.bp, 0)

    outs = pl.pallas_call(
        functools.partial(_mlstm_kernel, rows, nh, dqk, dv),
        out_shape=(jax.ShapeDtypeStruct((rows.rows, mw), BF16),
                   jax.ShapeDtypeStruct((rows.nseq, CONV_K - 1, 2 * kq), F32),
                   jax.ShapeDtypeStruct((rows.nseq, nh, dqk, dv), F32),
                   jax.ShapeDtypeStruct((rows.nseq, nh, dqk), F32),
                   jax.ShapeDtypeStruct((rows.nseq, 1, nh), F32)),
        grid=(rows.groups,),
        in_specs=[pl.BlockSpec((L, mw), lambda g: (g, sec_qk)),
                  pl.BlockSpec((L, mw), lambda g: (g, sec_qk + 1)),
                  pl.BlockSpec((L, mw), lambda g: (g, sec_qk + 2)),
                  pl.BlockSpec((L, GATE_LANES), lambda g: (g, 0)),
                  pl.BlockSpec((1, CONV_K - 1, 2 * kq), lambda g: (sseq(g), 0, 0)),
                  pl.BlockSpec((1, nh, dqk, dv), lambda g: (sseq(g), 0, 0, 0)),
                  pl.BlockSpec((1, nh, dqk), lambda g: (sseq(g), 0, 0)),
                  pl.BlockSpec((1, 1, nh), lambda g: (sseq(g), 0, 0)),
                  pl.BlockSpec((CONV_K, 2 * kq), lambda g: (0, 0)),
                  pl.BlockSpec((1, 2 * kq), lambda g: (0, 0)),
                  pl.BlockSpec((1, GATE_LANES), lambda g: (0, 0)),
                  pl.BlockSpec((1, mw), lambda g: (0, 0))],
        out_specs=(pl.BlockSpec((L, mw), lambda g: (g, 0)),
                   pl.BlockSpec((1, CONV_K - 1, 2 * kq), lambda g: (rows.seq_of_group(g), 0, 0)),
                   pl.BlockSpec((1, nh, dqk, dv), lambda g: (rows.seq_of_group(g), 0, 0, 0)),
                   pl.BlockSpec((1, nh, dqk), lambda g: (rows.seq_of_group(g), 0, 0)),
                   pl.BlockSpec((1, 1, nh), lambda g: (rows.seq_of_group(g), 0, 0))),
        scratch_shapes=[pltpu.VMEM((8 + L, 2 * kq), F32)],
        compiler_params=_cparams(1),
        name="mlstm",
    )(z, z, z, gates, conv0, c0, n0, m0.reshape(bs, 1, nh), conv_w, conv_b.reshape(1, 2 * kq),
      gbias, g_mlstm.reshape(1, mw))
    return outs


def _outproj_kernel(rows, tm, d, hw, og_ref, hm_ref, w_ref, xp_ref, xs_ref, mod_ref, o_ref, wb_ref):
    j, i = pl.program_id(0), pl.program_id(1)
    is_prompt = i < rows.groups_p * CHUNK // tm

    @pl.when(i == 0)
    def _():
        wb_ref[...] = w_ref[...].astype(BF16)

    mix = (jnp.dot(og_ref[...], wb_ref[0:hw, :], preferred_element_type=F32)
           + jnp.dot(hm_ref[...], wb_ref[hw:, :], preferred_element_type=F32))
    for gi in range(tm // CHUNK):
        seq = rows.seq_of_group(i * (tm // CHUNK) + gi)
        gate = mod_ref[pl.ds(seq, 1), :]
        rs = slice(gi * CHUNK, (gi + 1) * CHUNK)
        x = jnp.where(is_prompt, xp_ref[rs, :], xs_ref[rs, :])
        o_ref[rs, :] = x + gate * mix[rs, :]


def _outproj(rows, og, hm, w_out, x_p, x_s, mod, tm=1024, tn=512):
    d = x_p.shape[1]
    r = rows.rows
    hw = og.shape[1]
    tm, tn = min(tm, x_s.shape[0]), min(tn, d)
    assert r % tm == 0 and d % tn == 0
    xp_spec, xs_spec = _split_row_specs(rows, tm, tn, col_major=True)
    gate_blk0 = 2 * d // tn
    return pl.pallas_call(
        functools.partial(_outproj_kernel, rows, tm, d, hw),
        out_shape=jax.ShapeDtypeStruct((r, d), F32),
        grid=(d // tn, r // tm),
        in_specs=[pl.BlockSpec((tm, hw), lambda j, i: (i, 0)),
                  pl.BlockSpec((tm, hm.shape[1]), lambda j, i: (i, 0)),
                  pl.BlockSpec((w_out.shape[0], tn), lambda j, i: (0, j)),
                  xp_spec, xs_spec,
                  pl.BlockSpec((mod.shape[0], tn), lambda j, i: (0, gate_blk0 + j))],
        out_specs=pl.BlockSpec((tm, tn), lambda j, i: (i, j)),
        scratch_shapes=[pltpu.VMEM((w_out.shape[0], tn), BF16)],
        compiler_params=_cparams(2),
        name="out_proj",
    )(og, hm, w_out, x_p, x_s, mod)


def _router_kernel(rows, tm, d, ne, x_ref, g_ref, mod_ref, wr_ref, br_ref, ut_ref,
                   h2_ref, e_ref, gate_ref, rank_ref, cnt_ref, h2s_ref):
    i = pl.program_id(0)
    half = d // 2

    @pl.when(i == 0)
    def _():
        cnt_ref[...] = jnp.zeros_like(cnt_ref)

    g = g_ref[...]
    for gi in range(tm // CHUNK):
        seq = rows.seq_of_group(i * (tm // CHUNK) + gi)
        m = mod_ref[pl.ds(seq, 1), :]
        shift, scale = m[:, 3 * d:4 * d], m[:, 4 * d:5 * d]
        x = x_ref[gi * CHUNK:(gi + 1) * CHUNK, :]
        ms = jnp.mean(x * x, axis=-1, keepdims=True)
        h2s_ref[gi * CHUNK:(gi + 1) * CHUNK, :] = (x * lax.rsqrt(ms + RMS_EPS) * g) * (1.0 + scale) + shift
    h2 = h2s_ref[...]
    lo = pltpu.bitcast(h2[:, :half].astype(BF16).astype(F32), jnp.uint32)
    hi = pltpu.bitcast(h2[:, half:].astype(BF16).astype(F32), jnp.uint32)
    h2_ref[...] = (hi & jnp.uint32(0xFFFF0000)) | (lo >> 16)

    logits = jnp.dot(h2, wr_ref[...], precision=lax.Precision.HIGHEST,
                     preferred_element_type=F32)
    lt = logits.T[0:ne, :] + br_ref[...]
    eidx = lax.broadcasted_iota(jnp.int32, (ne, tm), 0)
    cur = lt
    tops, sels, hots = [], [], []
    for _ in range(TOP_K):
        mx = jnp.max(cur, axis=0, keepdims=True)
        sel = jnp.min(jnp.where(cur == mx, eidx, ne), axis=0, keepdims=True)
        hot = eidx == sel
        tops.append(mx)
        sels.append(sel)
        hots.append(hot)
        cur = jnp.where(hot, -jnp.inf, cur)
    ex = [jnp.exp(t - tops[0]) for t in tops]
    tot = ex[0] + ex[1] + ex[2] + ex[3]
    oh = jnp.zeros((ne, tm), F32)
    for hot in hots:
        oh = oh + hot.astype(F32)
    prefix = jnp.dot(oh.astype(BF16), ut_ref[...], preferred_element_type=F32)
    base = cnt_ref[:, 0:1]
    pos = base + prefix
    for kk in range(TOP_K):
        e_ref[kk:kk + 1, :] = sels[kk]
        gate_ref[kk:kk + 1, :] = ex[kk] / tot
        rank_ref[kk:kk + 1, :] = jnp.sum(jnp.where(hots[kk], pos, 0.0), axis=0,
                                         keepdims=True).astype(jnp.int32)
    cnt_ref[...] = cnt_ref[...] + jnp.sum(oh, axis=1, keepdims=True)


def _router(rows, x1, g_ffn, mod, w_router, b_router, tm=256):
    r, d = x1.shape
    ne = w_router.shape[1]
    tm = min(tm, r)
    assert r % tm == 0 and ne % 8 == 0 and ne <= LANES
    wr_pad = jnp.zeros((d, LANES), F32).at[:, :ne].set(w_router.astype(F32))
    tt = np.arange(tm)
    ut = jnp.asarray((tt[:, None] < tt[None, :]).astype(np.float32), BF16)
    return pl.pallas_call(
        functools.partial(_router_kernel, rows, tm, d, ne),
        out_shape=(jax.ShapeDtypeStruct((r, d // 2), jnp.uint32),
                   jax.ShapeDtypeStruct((TOP_K, r), jnp.int32),
                   jax.ShapeDtypeStruct((TOP_K, r), F32),
                   jax.ShapeDtypeStruct((TOP_K, r), jnp.int32),
                   jax.ShapeDtypeStruct((ne, LANES), F32)),
        grid=(r // tm,),
        in_specs=[pl.BlockSpec((tm, d), lambda i: (i, 0)),
                  pl.BlockSpec((1, d), lambda i: (0, 0)),
                  pl.BlockSpec(mod.shape, lambda i: (0, 0)),
                  pl.BlockSpec((d, LANES), lambda i: (0, 0)),
                  pl.BlockSpec((ne, 1), lambda i: (0, 0)),
                  pl.BlockSpec((tm, tm), lambda i: (0, 0))],
        out_specs=(pl.BlockSpec((tm, d // 2), lambda i: (i, 0)),
                   pl.BlockSpec((TOP_K, tm), lambda i: (0, i)),
                   pl.BlockSpec((TOP_K, tm), lambda i: (0, i)),
                   pl.BlockSpec((TOP_K, tm), lambda i: (0, i)),
                   pl.BlockSpec((ne, LANES), lambda i: (0, 0))),
        scratch_shapes=[pltpu.VMEM((tm, d), F32)],
        compiler_params=_cparams(1),
        name="router",
    )(x1, g_ffn.reshape(1, d), mod, wr_pad, b_router.reshape(ne, 1).astype(F32), ut)


def _invert_kernel(tm, pstart_ref, e_ref, rank_ref, rt_ref):
    i = pl.program_id(0)

    @pl.when(i == 0)
    def _():
        def clear(r, carry):
            rt_ref[r] = 0
            return carry

        lax.fori_loop(0, rt_ref.shape[0], clear, 0, unroll=8)

    def body(t, carry):
        for kk in range(TOP_K):
            rt_ref[pstart_ref[e_ref[kk, t]] + rank_ref[kk, t]] = i * tm + t
        return carry

    lax.fori_loop(0, tm, body, 0, unroll=4)


def _invert(e_idx, rank, pstart, n_rows, tm=256):
    r = e_idx.shape[1]
    tm = min(tm, r)
    assert r % tm == 0
    return pl.pallas_call(
        functools.partial(_invert_kernel, tm),
        out_shape=jax.ShapeDtypeStruct((n_rows,), jnp.int32),
        grid_spec=pltpu.PrefetchScalarGridSpec(
            num_scalar_prefetch=1,
            grid=(r // tm,),
            in_specs=[pl.BlockSpec((TOP_K, tm), lambda i, *_: (0, i), memory_space=pltpu.SMEM),
                      pl.BlockSpec((TOP_K, tm), lambda i, *_: (0, i), memory_space=pltpu.SMEM)],
            out_specs=pl.BlockSpec(memory_space=pltpu.SMEM)),
        compiler_params=_cparams(1),
        name="moe_invert",
    )(pstart, e_idx, rank)


def _unpack_rows(xw):
    lo = pltpu.bitcast(xw << 16, F32).astype(BF16)
    hi = pltpu.bitcast(xw & jnp.uint32(0xFFFF0000), F32).astype(BF16)
    return lo, hi


def _gather_kernel(nused_ref, rt_cur_ref, rt_next_ref, h2_ref, xs_ref, buf_ref, sem):
    b = pl.program_id(0)
    n_used = nused_ref[0]
    half = buf_ref.shape[2]

    def row_copy(slot, src_row, dst_row):
        return pltpu.make_async_copy(h2_ref.at[pl.ds(src_row, 1), :],
                                     buf_ref.at[slot, pl.ds(dst_row, 1), :], sem.at[slot])

    def issue(rt_ref, slot):
        def one(t, carry):
            row_copy(slot, rt_ref[0, 0, t], t).start()
            return carry

        lax.fori_loop(0, MOE_BLOCK, one, 0, unroll=8)

    @pl.when(jnp.logical_and(b == 0, n_used > 0))
    def _():
        issue(rt_cur_ref, 0)

    @pl.when(b + 1 < n_used)
    def _():
        issue(rt_next_ref, (b + 1) % 2)

    @pl.when(b < n_used)
    def _():
        slot = b % 2

        def drain(t, carry):
            row_copy(slot, 0, 0).wait()
            return carry

        lax.fori_loop(0, MOE_BLOCK, drain, 0, unroll=8)
        lo, hi = _unpack_rows(buf_ref[slot])
        xs_ref[:, 0:half] = lo
        xs_ref[:, half:] = hi

    @pl.when(b >= n_used)
    def _():
        xs_ref[...] = jnp.zeros_like(xs_ref)


def _gather_rows(h2p, row_tok, n_used):
    r, half = h2p.shape
    n_blocks = row_tok.shape[0] // MOE_BLOCK
    rt3 = row_tok.reshape(n_blocks, 1, MOE_BLOCK)
    return pl.pallas_call(
        _gather_kernel,
        out_shape=jax.ShapeDtypeStruct((n_blocks * MOE_BLOCK, 2 * half), BF16),
        grid_spec=pltpu.PrefetchScalarGridSpec(
            num_scalar_prefetch=1,
            grid=(n_blocks,),
            in_specs=[pl.BlockSpec((1, 1, MOE_BLOCK), lambda b, *_: (b, 0, 0), memory_space=pltpu.SMEM),
                      pl.BlockSpec((1, 1, MOE_BLOCK), lambda b, *_: (jnp.minimum(b + 1, n_blocks - 1), 0, 0),
                                   memory_space=pltpu.SMEM),
                      pl.BlockSpec(memory_space=pl.ANY)],
            out_specs=pl.BlockSpec((MOE_BLOCK, 2 * half), lambda b, *_: (b, 0)),
            scratch_shapes=[pltpu.VMEM((2, MOE_BLOCK, half), jnp.uint32),
                            pltpu.SemaphoreType.DMA((2,))]),
        compiler_params=_cparams(1),
        name="moe_gather",
    )(n_used.reshape(1), rt3, rt3, h2p)


MOE_CACHE_BLOCKS = 6


class _ExpertRows:
    def __init__(self, src_hbm, slots_ref, sems, b0, nb, first_tile):
        self.src, self.slots, self.sems = src_hbm, slots_ref, sems
        self.b0, self.nb, self.first_tile = b0, nb, first_tile

    def slot_of(self, i):
        return jnp.where(i < MOE_CACHE_BLOCKS, i, MOE_CACHE_BLOCKS + i % 2)

    def _needs_load(self, i):
        return jnp.logical_or(self.first_tile, i >= MOE_CACHE_BLOCKS)

    def _copy(self, i):
        row0 = pl.multiple_of((self.b0 + i) * MOE_BLOCK, MOE_BLOCK)
        return pltpu.make_async_copy(self.src.at[pl.ds(row0, MOE_BLOCK), :],
                                     self.slots.at[self.slot_of(i)], self.sems.at[i % 2])

    def request(self, i):
        @pl.when(jnp.logical_and(i < self.nb, self._needs_load(i)))
        def _():
            self._copy(i).start()

    def arrive(self, i):
        @pl.when(self._needs_load(i))
        def _():
            self._copy(i).wait()


def _block_window(dst_hbm, blk, col0, width):
    row0 = pl.multiple_of(blk * MOE_BLOCK, MOE_BLOCK)
    return dst_hbm.at[pl.ds(row0, MOE_BLOCK), pl.ds(pl.multiple_of(col0, width), width)]


def _zero_tail_blocks(stage_ref, dst_hbm, sem, first_blk, col0, width):
    n_blocks = dst_hbm.shape[0] // MOE_BLOCK
    stage_ref[0] = jnp.zeros(stage_ref.shape[1:], stage_ref.dtype)

    def cp(blk):
        return pltpu.make_async_copy(stage_ref.at[0], _block_window(dst_hbm, blk, col0, width), sem.at[0])

    def start(blk, carry):
        cp(blk).start()
        return carry

    def wait(blk, carry):
        cp(blk).wait()
        return carry

    lax.fori_loop(first_blk, n_blocks, start, 0)
    lax.fori_loop(first_blk, n_blocks, wait, 0)


MOE_CAST_CHUNKS = 4


def _cast_chunk(w_ref, wbf_ref, c):
    kc = wbf_ref.shape[0] // MOE_CAST_CHUNKS
    wbf_ref[c * kc:(c + 1) * kc, :] = w_ref[0, c * kc:(c + 1) * kc, :].astype(BF16)


def _first_block_dot(x_blk_ref, w_ref, wbf_ref):
    kc = wbf_ref.shape[0] // MOE_CAST_CHUNKS
    acc = None
    for c in range(MOE_CAST_CHUNKS):
        if c + 1 < MOE_CAST_CHUNKS:
            _cast_chunk(w_ref, wbf_ref, c + 1)
        part = jnp.dot(x_blk_ref[:, c * kc:(c + 1) * kc], wbf_ref[c * kc:(c + 1) * kc, :],
                       preferred_element_type=F32)
        acc = part if acc is None else acc + part
    return acc


def _moe_up_kernel(tf, nj, ne, bstart_ref, nblk_ref, xs_hbm, w_ref, b_ref, act_hbm,
                   wbf_ref, xb_ref, gt_ref, ostage_ref, xsem, osem):
    g = pl.program_id(0)
    e, j = g // nj, g % nj
    nb, b0 = nblk_ref[e], bstart_ref[e]
    nslab = MOE_BLOCK // LANES
    xrows = _ExpertRows(xs_hbm, xb_ref, xsem, b0, nb, j == 0)

    def out_copy(i):
        return pltpu.make_async_copy(ostage_ref.at[i % 2],
                                     _block_window(act_hbm, b0 + i, j * tf, tf), osem.at[i % 2])

    def up_dot(i):
        return jnp.dot(xb_ref[xrows.slot_of(i)], wbf_ref[...], preferred_element_type=F32) + b_ref[0]

    def swiglu(gu):
        gut = gu.T
        for sl in range(nslab):
            gt_ref[sl] = gut[:, sl * LANES:(sl + 1) * LANES]
        parts = []
        for sl in range(nslab):
            gate = jnp.minimum(gt_ref[sl, pl.ds(0, tf, stride=2), :], SWIGLU_LIMIT)
            up = jnp.clip(gt_ref[sl, pl.ds(1, tf, stride=2), :], -SWIGLU_LIMIT, SWIGLU_LIMIT)
            parts.append((up + 1.0) * gate * _sigmoid(SWIGLU_ALPHA * gate))
        return jnp.concatenate(parts, axis=1).T.astype(BF16)

    def finish(gu, i):
        act = swiglu(gu)
        ostage_ref[i % 2] = act
        out_copy(i).start()

    @pl.when(nb > 0)
    def _():
        xrows.request(0)
        _cast_chunk(w_ref, wbf_ref, 0)
        xrows.arrive(0)
        xrows.request(1)
        gu0 = _first_block_dot(xb_ref.at[0], w_ref, wbf_ref) + b_ref[0]

        def body(i, gu_prev):
            xrows.arrive(i)
            xrows.request(i + 1)

            @pl.when(i >= 3)
            def _():
                out_copy(i - 3).wait()

            gu = up_dot(i)
            finish(gu_prev, i - 1)
            return gu

        gu_last = lax.fori_loop(1, nb, body, gu0)

        @pl.when(nb >= 3)
        def _():
            out_copy(nb - 3).wait()

        finish(gu_last, nb - 1)

        @pl.when(nb >= 2)
        def _():
            out_copy(nb - 2).wait()

        out_copy(nb - 1).wait()

    @pl.when(e == ne - 1)
    def _():
        _zero_tail_blocks(ostage_ref, act_hbm, osem, b0 + nb, j * tf, tf)


def _moe_up(xs, w_gu, b_gu, bstart, nblk, tf=256):
    n_rows, d = xs.shape
    ne, _, f2 = w_gu.shape
    f = f2 // 2
    tf = min(tf, f)
    assert f % tf == 0 and w_gu.shape[1] == d
    nj = f // tf
    return pl.pallas_call(
        functools.partial(_moe_up_kernel, tf, nj, ne),
        out_shape=jax.ShapeDtypeStruct((n_rows, f), BF16),
        grid_spec=pltpu.PrefetchScalarGridSpec(
            num_scalar_prefetch=2,
            grid=(ne * nj,),
            in_specs=[pl.BlockSpec(memory_space=pl.ANY),
                      pl.BlockSpec((1, d, 2 * tf), lambda g, *_: (g // nj, 0, g % nj)),
                      pl.BlockSpec((1, 1, 2 * tf), lambda g, *_: (g // nj, 0, g % nj))],
            out_specs=pl.BlockSpec(memory_space=pl.ANY),
            scratch_shapes=[pltpu.VMEM((d, 2 * tf), BF16),
                            pltpu.VMEM((MOE_CACHE_BLOCKS + 2, MOE_BLOCK, d), BF16),
                            pltpu.VMEM((MOE_BLOCK // LANES, 2 * tf, LANES), F32),
                            pltpu.VMEM((2, MOE_BLOCK, tf), BF16),
                            pltpu.SemaphoreType.DMA((2,)),
                            pltpu.SemaphoreType.DMA((2,))]),
        compiler_params=_cparams(1),
        name="moe_up",
    )(bstart, nblk, xs, w_gu, b_gu.reshape(ne, 1, f2))


def _moe_down_kernel(tn, nj, ne, bstart_ref, nblk_ref, act_hbm, w_ref, b_ref, yb_hbm,
                     wbf_ref, ab_ref, ostage_ref, asem, osem):
    g = pl.program_id(0)
    e, j = g // nj, g % nj
    nb, b0 = nblk_ref[e], bstart_ref[e]
    arows = _ExpertRows(act_hbm, ab_ref, asem, b0, nb, j == 0)

    def out_copy(i):
        return pltpu.make_async_copy(ostage_ref.at[i % 2],
                                     _block_window(yb_hbm, b0 + i, j * tn, tn), osem.at[i % 2])

    @pl.when(nb > 0)
    def _():
        arows.request(0)
        _cast_chunk(w_ref, wbf_ref, 0)
        arows.arrive(0)
        arows.request(1)
        ostage_ref[0] = _first_block_dot(ab_ref.at[0], w_ref, wbf_ref) + b_ref[0]
        out_copy(0).start()

        def body(i, carry):
            arows.arrive(i)
            arows.request(i + 1)

            @pl.when(i >= 2)
            def _():
                out_copy(i - 2).wait()

            ostage_ref[i % 2] = jnp.dot(ab_ref[arows.slot_of(i)], wbf_ref[...],
                                        preferred_element_type=F32) + b_ref[0]
            out_copy(i).start()
            return carry

        lax.fori_loop(1, nb, body, 0)

        @pl.when(nb >= 2)
        def _():
            out_copy(nb - 2).wait()

        out_copy(nb - 1).wait()

    @pl.when(e == ne - 1)
    def _():
        _zero_tail_blocks(ostage_ref, yb_hbm, osem, b0 + nb, j * tn, tn)


def _moe_down(act, w_d, b_d, bstart, nblk, tn=512):
    n_rows, f = act.shape
    ne, _, d = w_d.shape
    tn = min(tn, d)
    assert d % tn == 0
    nj = d // tn
    return pl.pallas_call(
        functools.partial(_moe_down_kernel, tn, nj, ne),
        out_shape=jax.ShapeDtypeStruct((n_rows, d), F32),
        grid_spec=pltpu.PrefetchScalarGridSpec(
            num_scalar_prefetch=2,
            grid=(ne * nj,),
            in_specs=[pl.BlockSpec(memory_space=pl.ANY),
                      pl.BlockSpec((1, f, tn), lambda g, *_: (g // nj, 0, g % nj)),
                      pl.BlockSpec((1, 1, tn), lambda g, *_: (g // nj, 0, g % nj))],
            out_specs=pl.BlockSpec(memory_space=pl.ANY),
            scratch_shapes=[pltpu.VMEM((f, tn), BF16),
                            pltpu.VMEM((MOE_CACHE_BLOCKS + 2, MOE_BLOCK, f), BF16),
                            pltpu.VMEM((2, MOE_BLOCK, tn), F32),
                            pltpu.SemaphoreType.DMA((2,)),
                            pltpu.SemaphoreType.DMA((2,))]),
        compiler_params=_cparams(1),
        name="moe_down",
    )(bstart, nblk, act, w_d, b_d.reshape(ne, 1, d))


def _combine_kernel(rows, tm, d, n_tiles, pstart_ref, e_ref, rank_ref, en_ref, rankn_ref, gt_ref,
                    x_ref, mod_ref, gf_ref, yb_ref, yp_ref, ys_ref, buf_ref, sem):
    i = pl.program_id(0)
    is_prompt = i < rows.groups_p * CHUNK // tm
    slot = i % 2

    def row_copy(s, dst_k, dst_t, src_row):
        return pltpu.make_async_copy(yb_ref.at[pl.ds(src_row, 1), :],
                                     buf_ref.at[s, dst_k, pl.ds(dst_t, 1), :], sem.at[s])

    def issue(eref, rref, s):
        def one(t, carry):
            for kk in range(TOP_K):
                src = pstart_ref[eref[kk, t]] + rref[kk, t]
                row_copy(s, kk, t, src).start()
            return carry

        lax.fori_loop(0, tm, one, 0, unroll=2)

    @pl.when(i == 0)
    def _():
        issue(e_ref, rank_ref, 0)

    @pl.when(i + 1 < n_tiles)
    def _():
        issue(en_ref, rankn_ref, (i + 1) % 2)

    def drain(t, carry):
        for kk in range(TOP_K):
            row_copy(slot, 0, 0, 0).wait()
        return carry

    lax.fori_loop(0, tm, drain, 0, unroll=2)

    gf = gf_ref[...]
    for gi in range(tm // CHUNK):
        rs = slice(gi * CHUNK, (gi + 1) * CHUNK)
        seq = rows.seq_of_group(i * (tm // CHUNK) + gi)
        gate2 = mod_ref[pl.ds(seq, 1), :]
        ff = jnp.zeros((CHUNK, d), F32)
        for kk in range(TOP_K):
            ff = ff + buf_ref[slot, kk, rs, :] * gt_ref[rs, kk:kk + 1]
        x2 = x_ref[rs, :] + gate2 * ff
        ms = jnp.mean(x2 * x2, axis=-1, keepdims=True)
        y = x2 * lax.rsqrt(ms + RMS_EPS) * gf

        @pl.when(is_prompt)
        def _():
            yp_ref[rs, :] = y

        @pl.when(jnp.logical_not(is_prompt))
        def _():
            ys_ref[rs, :] = y


def _combine(rows, x1, mod, g_final, yb, e_idx, rank, gates_t, pstart, tm=128):
    r, d = x1.shape
    tm = min(tm, r)
    assert r % tm == 0
    gate2_blk = 5
    yp_spec, ys_spec = _split_row_specs(rows, tm, d)
    n_tiles = r // tm

    def nxt(i, *_):
        return (0, jnp.minimum(i + 1, n_tiles - 1))

    return pl.pallas_call(
        functools.partial(_combine_kernel, rows, tm, d, n_tiles),
        out_shape=(jax.ShapeDtypeStruct((rows.groups_p * CHUNK, d), F32),
                   jax.ShapeDtypeStruct((r - rows.groups_p * CHUNK, d), F32)),
        grid_spec=pltpu.PrefetchScalarGridSpec(
            num_scalar_prefetch=1,
            grid=(n_tiles,),
            in_specs=[pl.BlockSpec((TOP_K, tm), lambda i, *_: (0, i), memory_space=pltpu.SMEM),
                      pl.BlockSpec((TOP_K, tm), lambda i, *_: (0, i), memory_space=pltpu.SMEM),
                      pl.BlockSpec((TOP_K, tm), nxt, memory_space=pltpu.SMEM),
                      pl.BlockSpec((TOP_K, tm), nxt, memory_space=pltpu.SMEM),
                      pl.BlockSpec((tm, TOP_K), lambda i, *_: (i, 0)),
                      pl.BlockSpec((tm, d), lambda i, *_: (i, 0)),
                      pl.BlockSpec((mod.shape[0], d), lambda i, *_: (0, gate2_blk)),
                      pl.BlockSpec((1, d), lambda i, *_: (0, 0)),
                      pl.BlockSpec(memory_space=pl.ANY)],
            out_specs=(yp_spec, ys_spec),
            scratch_shapes=[pltpu.VMEM((2, TOP_K, tm, d), F32),
                            pltpu.SemaphoreType.DMA((2,))]),
        compiler_params=_cparams(1),
        name="moe_combine",
    )(pstart, e_idx, rank, e_idx, rank, gates_t, x1, mod, g_final.reshape(1, d), yb)


def kernel(x_prompt, x_sample, c_prompt, c_sample, state_hgrn_S, state_conv, state_mlstm_C,
           state_mlstm_n, state_mlstm_m, w_ada, b_ada, g_mix, g_ffn, w_in, lb_logits, conv_w,
           conv_b, b_igate, b_fgate, g_hgrn, g_mlstm, w_out, w_router, b_router, w_gate_up,
           b_gate_up, w_down, b_down, g_final):
    depth = w_ada.shape[0]
    assert depth == 1, "single-layer trunk"
    bp, tp, d = x_prompt.shape
    bs, ts, _ = x_sample.shape
    rows = _Rows(bp, tp, bs, ts)
    _, _, nh_h, dk, dv = state_hgrn_S.shape
    _, _, nh_m, dqk, dvm = state_mlstm_C.shape
    hw = nh_h * dk
    mw = nh_m * dvm
    n_main = 4 * hw + 3 * mw
    assert w_in.shape[2] == n_main + 2 * nh_m and 2 * nh_m <= GATE_LANES
    assert mw == hw, "column sections of the input projection are addressed in hw-wide blocks"
    ne = w_router.shape[2]

    lb = jax.nn.softmax(lb_logits.astype(F32), axis=0)[0]

    x_p = x_prompt.reshape(bp * tp, d)
    x_s = x_sample.reshape(bs * ts, d)
    n_c = bp + bs
    n_c_pad = -(-n_c // 8) * 8
    c_pad = jnp.zeros((n_c_pad, d), F32).at[:n_c].set(jnp.concatenate([c_prompt, c_sample], axis=0))
    mod = _ada(c_pad, w_ada[0], b_ada[0])

    h_all, gates = _prep(rows, x_p, x_s, g_mix[0], mod, w_in, n_main)
    z = _inproj(h_all, w_in, n_main)

    og, st = _hgrn(rows, z, state_hgrn_S[0], lb, g_hgrn[0])
    hm, conv_new, c_new, n_new, m_new = _mlstm(
        rows, z, 4, gates, state_conv[0], state_mlstm_C[0], state_mlstm_n[0], state_mlstm_m[0],
        conv_w[0], conv_b[0], b_igate[0], b_fgate[0], g_mlstm[0])

    x1 = _outproj(rows, og, hm, w_out[0], x_p, x_s, mod)

    h2p, e_idx, gate_k, rank, cnt = _router(rows, x1, g_ffn[0], mod, w_router[0], b_router[0])

    counts = cnt[:, 0].astype(jnp.int32)
    nblk_e = (counts + MOE_BLOCK - 1) // MOE_BLOCK
    padded = nblk_e * MOE_BLOCK
    pad_end = jnp.cumsum(padded)
    pstart = jnp.concatenate([jnp.zeros((1,), jnp.int32), pad_end]).astype(jnp.int32)
    n_blocks_max = -(-(rows.rows * TOP_K) // MOE_BLOCK) + ne
    n_rows = n_blocks_max * MOE_BLOCK
    blk_start_e = (pstart[:ne] // MOE_BLOCK).astype(jnp.int32)
    n_used = (pstart[ne] // MOE_BLOCK).astype(jnp.int32)

    row_tok = _invert(e_idx, rank, pstart, n_rows)
    xs = _gather_rows(h2p, row_tok, n_used)
    act = _moe_up(xs, w_gate_up[0], b_gate_up[0], blk_start_e, nblk_e.astype(jnp.int32))
    yb = _moe_down(act, w_down[0], b_down[0], blk_start_e, nblk_e.astype(jnp.int32))

    y_p, y_s = _combine(rows, x1, mod, g_final, yb, e_idx, rank, gate_k.T, pstart)

    y_prompt = y_p.reshape(bp, tp, d)
    y_sample = y_s.reshape(bs, ts, d)
    s_all = jnp.swapaxes(st, 2, 3)
    m_all = m_new.reshape(rows.nseq, nh_m)

    def split(a):
        return a[:bp][None], a[bp:][None]

    p_s, s_s = split(s_all)
    p_conv, s_conv = split(conv_new)
    p_c, s_c = split(c_new)
    p_n, s_n = split(n_new)
    p_m, s_m = split(m_all)
    return (y_prompt, y_sample, p_s, p_conv, p_c, p_n, p_m, s_s, s_conv, s_c, s_n, s_m)
```

```python
import functools

import numpy as np
import jax
import jax.numpy as jnp
from jax import lax
from jax.experimental import pallas as pl
from jax.experimental.pallas import tpu as pltpu

F32 = jnp.float32
BF16 = jnp.bfloat16

CHUNK = 64
TOP_K = 4
CONV_K = 4
RMS_EPS = 1e-6
SWIGLU_LIMIT = 7.0
SWIGLU_ALPHA = 1.702
MOE_BLOCK = 256
LANES = 128
GATE_LANES = 128
VMEM_LIMIT = 56 * 1024 * 1024
HGRN_LEVELS = (32, 16, 8, 4, 2, 1)


def _cparams(n_axes, vmem=VMEM_LIMIT):
    return pltpu.CompilerParams(dimension_semantics=("arbitrary",) * n_axes,
                                vmem_limit_bytes=vmem)


def _sigmoid(x):
    return 1.0 / (1.0 + jnp.exp(-x))


def _silu(x):
    return x * _sigmoid(x)


class _Rows:
    def __init__(self, bp, tp, bs, ts):
        assert tp % CHUNK == 0 and ts % CHUNK == 0
        self.bp, self.tp, self.bs, self.ts = bp, tp, bs, ts
        self.nblk_p, self.nblk_s = tp // CHUNK, ts // CHUNK
        self.groups_p = bp * self.nblk_p
        self.groups = self.groups_p + bs * self.nblk_s
        self.rows = self.groups * CHUNK
        self.nseq = bp + bs

    def seq_of_group(self, g):
        return jnp.where(g < self.groups_p, g // self.nblk_p,
                         self.bp + (g - self.groups_p) // self.nblk_s)

    def blk_in_seq(self, g):
        return jnp.where(g < self.groups_p, g % self.nblk_p, (g - self.groups_p) % self.nblk_s)

    def nblk_of_group(self, g):
        return jnp.where(g < self.groups_p, self.nblk_p, self.nblk_s)


def _ada_kernel(c_ref, w_ref, b_ref, o_ref):
    c = c_ref[...]
    s = _silu(c).astype(BF16)
    o_ref[...] = jnp.dot(s, w_ref[...].astype(BF16), preferred_element_type=F32) + b_ref[...]


def _ada(c_pad, w_ada, b_ada, tn=512):
    m, d = c_pad.shape
    n = w_ada.shape[1]
    tn = min(tn, n)
    assert n % tn == 0
    return pl.pallas_call(
        _ada_kernel,
        out_shape=jax.ShapeDtypeStruct((m, n), F32),
        grid=(n // tn,),
        in_specs=[pl.BlockSpec((m, d), lambda j: (0, 0)),
                  pl.BlockSpec((d, tn), lambda j: (0, j)),
                  pl.BlockSpec((1, tn), lambda j: (0, j))],
        out_specs=pl.BlockSpec((m, tn), lambda j: (0, j)),
        compiler_params=_cparams(1),
        name="ada_mod",
    )(c_pad, w_ada, b_ada.reshape(1, n))


def _prep_kernel(rows, tm, d, n_gate, xp_ref, xs_ref, g_ref, mod_ref, wtail_ref,
                 h_ref, gates_ref, wg_ref):
    i = pl.program_id(0)
    is_prompt = i < rows.groups_p * CHUNK // tm

    @pl.when(i == 0)
    def _():
        wg_ref[...] = jnp.zeros_like(wg_ref)
        wg_ref[0:n_gate, :] = wtail_ref[0]

    g = g_ref[...]
    for gi in range(tm // CHUNK):
        seq = rows.seq_of_group(i * (tm // CHUNK) + gi)
        m = mod_ref[pl.ds(seq, 1), :]
        shift, scale = m[:, 0:d], m[:, d:2 * d]
        rs = slice(gi * CHUNK, (gi + 1) * CHUNK)
        x = jnp.where(is_prompt, xp_ref[rs, :], xs_ref[rs, :])
        ms = jnp.mean(x * x, axis=-1, keepdims=True)
        y = x * lax.rsqrt(ms + RMS_EPS) * g
        h = y * (1.0 + scale) + shift
        h_ref[gi * CHUNK:(gi + 1) * CHUNK, :] = h.astype(BF16)
        gates_ref[gi * CHUNK:(gi + 1) * CHUNK, :] = lax.dot_general(
            h, wg_ref[...], (((1,), (1,)), ((), ())), precision=lax.Precision.HIGHEST,
            preferred_element_type=F32)


def _split_row_specs(rows, tm, tn, col_major=False):
    n_pt = rows.groups_p * CHUNK // tm
    n_st = rows.rows // tm - n_pt
    assert n_pt * tm == rows.groups_p * CHUNK and n_st >= 1

    def ij(args):
        return (args[1], args[0]) if col_major else (args[0], 0)

    def p_map(*args):
        i, j = ij(args)
        return (jnp.minimum(i, n_pt - 1), j)

    def s_map(*args):
        i, j = ij(args)
        return (jnp.maximum(i - n_pt, 0), j)

    return pl.BlockSpec((tm, tn), p_map), pl.BlockSpec((tm, tn), s_map)


def _prep(rows, x_p, x_s, g_mix, mod, w_in_t, n_main, tm=256):
    d = x_p.shape[1]
    r = rows.rows
    tm = min(tm, x_s.shape[0])
    n_gate = w_in_t.shape[1] - n_main
    assert r % tm == 0 and n_gate % 8 == 0 and n_main % n_gate == 0 and n_gate <= GATE_LANES
    xp_spec, xs_spec = _split_row_specs(rows, tm, d)
    return pl.pallas_call(
        functools.partial(_prep_kernel, rows, tm, d, n_gate),
        out_shape=(jax.ShapeDtypeStruct((r, d), BF16),
                   jax.ShapeDtypeStruct((r, GATE_LANES), F32)),
        grid=(r // tm,),
        in_specs=[xp_spec, xs_spec,
                  pl.BlockSpec((1, d), lambda i: (0, 0)),
                  pl.BlockSpec(mod.shape, lambda i: (0, 0)),
                  pl.BlockSpec((1, n_gate, d), lambda i: (0, n_main // n_gate, 0))],
        out_specs=(pl.BlockSpec((tm, d), lambda i: (i, 0)),
                   pl.BlockSpec((tm, GATE_LANES), lambda i: (i, 0))),
        scratch_shapes=[pltpu.VMEM((GATE_LANES, d), F32)],
        compiler_params=_cparams(1),
        name="prep_norm_mod",
    )(x_p, x_s, g_mix.reshape(1, d), mod, w_in_t)


def _inproj_kernel(h_ref, w_ref, z_ref, wb_ref):
    @pl.when(pl.program_id(1) == 0)
    def _():
        wb_ref[...] = w_ref[0].astype(BF16)

    z_ref[...] = lax.dot_general(h_ref[...], wb_ref[...], (((1,), (1,)), ((), ())),
                                 preferred_element_type=F32)


def _inproj(h_all, w_in_t, n_main, tm=1024, tn=512):
    r, d = h_all.shape
    tm, tn = min(tm, r), min(tn, n_main)
    assert r % tm == 0 and n_main % tn == 0
    return pl.pallas_call(
        _inproj_kernel,
        out_shape=jax.ShapeDtypeStruct((r, n_main), F32),
        grid=(n_main // tn, r // tm),
        in_specs=[pl.BlockSpec((tm, d), lambda j, i: (i, 0)),
                  pl.BlockSpec((1, tn, d), lambda j, i: (0, j, 0))],
        out_specs=pl.BlockSpec((tm, tn), lambda j, i: (i, j)),
        scratch_shapes=[pltpu.VMEM((tn, d), BF16)],
        compiler_params=_cparams(2),
        name="in_proj",
    )(h_all, w_in_t)


def _hgrn_consts():
    L = CHUNK
    t = np.arange(L)[:, None]
    s = np.arange(L)[None, :]
    mats = [(s <= t)]
    masks = []
    for m in HGRN_LEVELS:
        start = (t // (2 * m)) * (2 * m)
        mats.append(s <= start + m - 1)
        masks.append((t // (2 * m) == s // (2 * m)) & (t % (2 * m) >= m) & (s % (2 * m) < m))
    mstack = np.concatenate(mats, axis=0).astype(np.float32)
    masks = np.stack(masks, axis=0).astype(np.float32)
    return jnp.asarray(mstack, BF16), jnp.asarray(masks, F32)


def _split3(x):
    x1 = x.astype(BF16)
    r1 = x - x1.astype(F32)
    x2 = r1.astype(BF16)
    x3 = (r1 - x2.astype(F32)).astype(BF16)
    return x1, x2, x3


def _hgrn_kernel(rows, nh, dk, dv, hq_ref, hf_ref, hi_ref, hg_ref, s0_ref, lb_ref, gn_ref,
                 mstack_ref, masks_ref, og_ref, st_ref, r_ref, o_ref):
    L = CHUNK
    g = pl.program_id(0)
    blk = rows.blk_in_seq(g)
    is_prompt = g < rows.groups_p

    @pl.when(jnp.logical_and(blk == 0, is_prompt))
    def _():
        st_ref[...] = jnp.zeros_like(st_ref)

    @pl.when(jnp.logical_and(blk == 0, jnp.logical_not(is_prompt)))
    def _():
        for h in range(nh):
            st_ref[0, h] = s0_ref[0, h].T

    lb = lb_ref[...]
    f = lb + (1.0 - lb) * _sigmoid(hf_ref[...])
    lf = jnp.log(f)
    mstack = mstack_ref[...]
    p1, p2, p3 = _split3(lf)
    r_ref[...] = (jnp.dot(mstack, p1, preferred_element_type=F32)
                  + jnp.dot(mstack, p2, preferred_element_type=F32)
                  + jnp.dot(mstack, p3, preferred_element_type=F32))

    for h in range(nh):
        cs = slice(h * dk, (h + 1) * dk)
        vs = slice(h * dv, (h + 1) * dv)
        q = _silu(hq_ref[:, cs]) * (dk ** -0.5)
        k = 1.0 - f[:, cs]
        v = hi_ref[:, vs]
        vb = v.astype(BF16)
        b = r_ref[0:L, cs]
        st = st_ref[0, h]
        qe = (q * jnp.exp(b)).astype(BF16)
        o = lax.dot_general(qe, st.astype(BF16), (((1,), (1,)), ((), ())),
                            preferred_element_type=F32)
        a = jnp.zeros((L, L), F32)
        for li in range(len(HGRN_LEVELS)):
            rl = r_ref[(li + 1) * L:(li + 2) * L, cs]
            e = jnp.exp(-jnp.abs(b - rl))
            al = lax.dot_general((q * e).astype(BF16), (k * e).astype(BF16),
                                 (((1,), (1,)), ((), ())), preferred_element_type=F32)
            a = a + al * masks_ref[li]
        diag = jnp.sum(q * k, axis=-1, keepdims=True)
        o = o + jnp.dot(a.astype(BF16), vb, preferred_element_type=F32) + diag * v
        o_ref[:, vs] = o
        b_last = b[L - 1:L, :]
        kd = (k * jnp.exp(b_last - b)).astype(BF16)
        st_ref[0, h] = st * jnp.exp(b_last) + lax.dot_general(
            vb, kd, (((0,), (0,)), ((), ())), preferred_element_type=F32)

    o = o_ref[...]
    ms = jnp.mean(o * o, axis=-1, keepdims=True)
    og = o * lax.rsqrt(ms + RMS_EPS) * gn_ref[...] * _silu(hg_ref[...])
    og_ref[...] = og.astype(BF16)


def _hgrn(rows, z, s0, lb, g_hgrn):
    bs, nh, dk, dv = s0.shape
    hw = nh * dk
    assert nh * dv == hw and hw % LANES == 0
    mstack, masks = _hgrn_consts()
    n_lv = len(HGRN_LEVELS)
    L = CHUNK

    def zspec(sec):
        return pl.BlockSpec((L, hw), lambda g, sec=sec: (g, sec))

    og, st = pl.pallas_call(
        functools.partial(_hgrn_kernel, rows, nh, dk, dv),
        out_shape=(jax.ShapeDtypeStruct((rows.rows, hw), BF16),
                   jax.ShapeDtypeStruct((rows.nseq, nh, dv, dk), F32)),
        grid=(rows.groups,),
        in_specs=[zspec(0), zspec(1), zspec(2), zspec(3),
                  pl.BlockSpec((1, nh, dk, dv),
                               lambda g: (jnp.maximum(rows.seq_of_group(g) - rows.bp, 0), 0, 0, 0)),
                  pl.BlockSpec((1, hw), lambda g: (0, 0)),
                  pl.BlockSpec((1, hw), lambda g: (0, 0)),
                  pl.BlockSpec(mstack.shape, lambda g: (0, 0)),
                  pl.BlockSpec(masks.shape, lambda g: (0, 0, 0))],
        out_specs=(pl.BlockSpec((L, hw), lambda g: (g, 0)),
                   pl.BlockSpec((1, nh, dv, dk), lambda g: (rows.seq_of_group(g), 0, 0, 0))),
        scratch_shapes=[pltpu.VMEM(((n_lv + 1) * L, hw), F32),
                        pltpu.VMEM((L, hw), F32)],
        compiler_params=_cparams(1),
        name="hgrn2",
    )(z, z, z, z, s0, lb.reshape(1, hw), g_hgrn.reshape(1, hw), mstack, masks)
    return og, st


def _log_sigmoid(x):
    return jnp.minimum(x, 0.0) - jnp.log(1.0 + jnp.exp(-jnp.abs(x)))


def _mlstm_kernel(rows, nh, dqk, dv, mqk_ref, mv_ref, mo_ref, gates_ref, conv0_ref, c0_ref,
                  n0_ref, m0_ref, cw_ref, cb_ref, gbias_ref, gn_ref,
                  hm_ref, conv_ref, c_ref, n_ref, m_ref, ubuf_ref):
    L = CHUNK
    kq = nh * dqk
    pad = 8
    g = pl.program_id(0)
    blk = rows.blk_in_seq(g)
    is_prompt = g < rows.groups_p

    @pl.when(jnp.logical_and(blk == 0, is_prompt))
    def _():
        ubuf_ref[0:pad, :] = jnp.zeros((pad, 2 * kq), F32)
        c_ref[...] = jnp.zeros_like(c_ref)
        n_ref[...] = jnp.zeros_like(n_ref)
        m_ref[...] = jnp.zeros_like(m_ref)

    @pl.when(jnp.logical_and(blk == 0, jnp.logical_not(is_prompt)))
    def _():
        ubuf_ref[0:pad, :] = jnp.zeros((pad, 2 * kq), F32)
        ubuf_ref[pad - (CONV_K - 1):pad, :] = conv0_ref[0]
        c_ref[...] = c0_ref[...]
        n_ref[...] = n0_ref[...]
        m_ref[...] = m0_ref[...]

    ubuf_ref[pad:pad + L, :] = mqk_ref[...]
    acc = cb_ref[...] + jnp.zeros((L, 2 * kq), F32)
    for j in range(CONV_K):
        off = pad - (CONV_K - 1) + j
        acc = acc + ubuf_ref[off:off + L, :] * cw_ref[j:j + 1, :]
    new_tail = ubuf_ref[pad + L - (CONV_K - 1):pad + L, :]
    conv_ref[0] = new_tail
    ubuf_ref[pad - (CONV_K - 1):pad, :] = new_tail
    qk = _silu(acc)

    gt = gates_ref[...] + gbias_ref[...]
    lane = lax.broadcasted_iota(jnp.int32, gt.shape, 1)
    pg = jnp.where(lane < nh, gt, _log_sigmoid(gt))
    pgt = pg.T
    ti = lax.broadcasted_iota(jnp.int32, (L, L), 0)
    si = lax.broadcasted_iota(jnp.int32, (L, L), 1)
    tri = si <= ti

    for h in range(nh):
        q = qk[:, h * dqk:(h + 1) * dqk]
        k = qk[:, kq + h * dqk:kq + (h + 1) * dqk] * (dqk ** -0.5)
        v = mv_ref[:, h * dv:(h + 1) * dv]
        qb, kb, vb = q.astype(BF16), k.astype(BF16), v.astype(BF16)
        ig_c, lf_c = pg[:, h:h + 1], pg[:, nh + h:nh + h + 1]
        ig_r, lf_r = pgt[h:h + 1, :], pgt[nh + h:nh + h + 1, :]
        b_c = jnp.sum(jnp.where(tri, lf_r, 0.0), axis=1, keepdims=True)
        b_r = jnp.sum(jnp.where(ti <= si, lf_c, 0.0), axis=0, keepdims=True)
        m_prev = m_ref[0, :, h:h + 1]
        log_d = jnp.where(tri, b_c - b_r + ig_r, -jnp.inf)
        log_inter = b_c + m_prev
        m_t = jnp.maximum(jnp.max(log_d, axis=1, keepdims=True), log_inter)
        dm = jnp.exp(log_d - m_t)
        s_mat = lax.dot_general(qb, kb, (((1,), (1,)), ((), ())), preferred_element_type=F32) * dm
        w_inter = jnp.exp(log_inter - m_t)
        c_h = c_ref[0, h]
        n_h = n_ref[0, h:h + 1, :]
        num = (jnp.dot(s_mat.astype(BF16), vb, preferred_element_type=F32)
               + w_inter * jnp.dot(qb, c_h.astype(BF16), preferred_element_type=F32))
        den = (jnp.sum(s_mat, axis=1, keepdims=True)
               + w_inter * jnp.sum(q * n_h, axis=1, keepdims=True))
        hh = num / jnp.maximum(jnp.abs(den), jnp.exp(-m_t))
        b_last = b_c[L - 1:L, :]
        lw_c = b_last - b_c + ig_c
        lw_r = b_last - b_r + ig_r
        m_new = jnp.maximum(b_last + m_prev, jnp.max(lw_r, axis=1, keepdims=True))
        decay = jnp.exp(b_last + m_prev - m_new)
        kw = k * jnp.exp(lw_c - m_new)
        c_ref[0, h] = decay * c_h + lax.dot_general(
            kw.astype(BF16), vb, (((0,), (0,)), ((), ())), preferred_element_type=F32)
        n_ref[0, h:h + 1, :] = decay * n_h + jnp.sum(kw, axis=0, keepdims=True)
        m_ref[0, :, h:h + 1] = m_new
        ms = jnp.mean(hh * hh, axis=-1, keepdims=True)
        hn = hh * lax.rsqrt(ms + RMS_EPS) * gn_ref[:, h * dv:(h + 1) * dv]
        hm_ref[:, h * dv:(h + 1) * dv] = (hn * _sigmoid(mo_ref[:, h * dv:(h + 1) * dv])).astype(BF16)


def _mlstm(rows, z, sec_qk, gates, conv0, c0, n0, m0, conv_w, conv_b, b_igate, b_fgate, g_mlstm):
    bs, nh, dqk, dv = c0.shape
    kq = nh * dqk
    mw = nh * dv
    L = CHUNK
    assert 2 * kq == mw, "q/k conv width must equal the value width for the column sections"
    gbias = jnp.zeros((1, GATE_LANES), F32)
    gbias = gbias.at[0, 0:nh].set(b_igate.astype(F32)).at[0, nh:2 * nh].set(b_fgate.astype(F32))

    def sseq(g):
        return jnp.maximum(rows.seq_of_group(g) - rows.bp, 0)

    outs = pl.pallas_call(
        functools.partial(_mlstm_kernel, rows, nh, dqk, dv),
        out_shape=(jax.ShapeDtypeStruct((rows.rows, mw), BF16),
                   jax.ShapeDtypeStruct((rows.nseq, CONV_K - 1, 2 * kq), F32),
                   jax.ShapeDtypeStruct((rows.nseq, nh, dqk, dv), F32),
                   jax.ShapeDtypeStruct((rows.nseq, nh, dqk), F32),
                   jax.ShapeDtypeStruct((rows.nseq, 1, nh), F32)),
        grid=(rows.groups,),
        in_specs=[pl.BlockSpec((L, mw), lambda g: (g, sec_qk)),
                  pl.BlockSpec((L, mw), lambda g: (g, sec_qk + 1)),
                  pl.BlockSpec((L, mw), lambda g: (g, sec_qk + 2)),
                  pl.BlockSpec((L, GATE_LANES), lambda g: (g, 0)),
                  pl.BlockSpec((1, CONV_K - 1, 2 * kq), lambda g: (sseq(g), 0, 0)),
                  pl.BlockSpec((1, nh, dqk, dv), lambda g: (sseq(g), 0, 0, 0)),
                  pl.BlockSpec((1, nh, dqk), lambda g: (sseq(g), 0, 0)),
                  pl.BlockSpec((1, 1, nh), lambda g: (sseq(g), 0, 0)),
                  pl.BlockSpec((CONV_K, 2 * kq), lambda g: (0, 0)),
                  pl.BlockSpec((1, 2 * kq), lambda g: (0, 0)),
                  pl.BlockSpec((1, GATE_LANES), lambda g: (0, 0)),
                  pl.BlockSpec((1, mw), lambda g: (0, 0))],
        out_specs=(pl.BlockSpec((L, mw), lambda g: (g, 0)),
                   pl.BlockSpec((1, CONV_K - 1, 2 * kq), lambda g: (rows.seq_of_group(g), 0, 0)),
                   pl.BlockSpec((1, nh, dqk, dv), lambda g: (rows.seq_of_group(g), 0, 0, 0)),
                   pl.BlockSpec((1, nh, dqk), lambda g: (rows.seq_of_group(g), 0, 0)),
                   pl.BlockSpec((1, 1, nh), lambda g: (rows.seq_of_group(g), 0, 0))),
        scratch_shapes=[pltpu.VMEM((8 + L, 2 * kq), F32)],
        compiler_params=_cparams(1),
        name="mlstm",
    )(z, z, z, gates, conv0, c0, n0, m0.reshape(bs, 1, nh), conv_w, conv_b.reshape(1, 2 * kq),
      gbias, g_mlstm.reshape(1, mw))
    return outs


def _outproj_kernel(rows, tm, d, hw, og_ref, hm_ref, w_ref, xp_ref, xs_ref, mod_ref, o_ref, wb_ref):
    j, i = pl.program_id(0), pl.program_id(1)
    is_prompt = i < rows.groups_p * CHUNK // tm

    @pl.when(i == 0)
    def _():
        wb_ref[...] = w_ref[...].astype(BF16)

    mix = (jnp.dot(og_ref[...], wb_ref[0:hw, :], preferred_element_type=F32)
           + jnp.dot(hm_ref[...], wb_ref[hw:, :], preferred_element_type=F32))
    for gi in range(tm // CHUNK):
        seq = rows.seq_of_group(i * (tm // CHUNK) + gi)
        gate = mod_ref[pl.ds(seq, 1), :]
        rs = slice(gi * CHUNK, (gi + 1) * CHUNK)
        x = jnp.where(is_prompt, xp_ref[rs, :], xs_ref[rs, :])
        o_ref[rs, :] = x + gate * mix[rs, :]


def _outproj(rows, og, hm, w_out, x_p, x_s, mod, tm=1024, tn=512):
    d = x_p.shape[1]
    r = rows.rows
    hw = og.shape[1]
    tm, tn = min(tm, x_s.shape[0]), min(tn, d)
    assert r % tm == 0 and d % tn == 0
    xp_spec, xs_spec = _split_row_specs(rows, tm, tn, col_major=True)
    gate_blk0 = 2 * d // tn
    return pl.pallas_call(
        functools.partial(_outproj_kernel, rows, tm, d, hw),
        out_shape=jax.ShapeDtypeStruct((r, d), F32),
        grid=(d // tn, r // tm),
        in_specs=[pl.BlockSpec((tm, hw), lambda j, i: (i, 0)),
                  pl.BlockSpec((tm, hm.shape[1]), lambda j, i: (i, 0)),
                  pl.BlockSpec((w_out.shape[0], tn), lambda j, i: (0, j)),
                  xp_spec, xs_spec,
                  pl.BlockSpec((mod.shape[0], tn), lambda j, i: (0, gate_blk0 + j))],
        out_specs=pl.BlockSpec((tm, tn), lambda j, i: (i, j)),
        scratch_shapes=[pltpu.VMEM((w_out.shape[0], tn), BF16)],
        compiler_params=_cparams(2),
        name="out_proj",
    )(og, hm, w_out, x_p, x_s, mod)


def _router_kernel(rows, tm, d, ne, x_ref, g_ref, mod_ref, wr_ref, br_ref, ut_ref,
                   h2_ref, e_ref, gate_ref, rank_ref, cnt_ref, h2s_ref):
    i = pl.program_id(0)
    half = d // 2

    @pl.when(i == 0)
    def _():
        cnt_ref[...] = jnp.zeros_like(cnt_ref)

    g = g_ref[...]
    for gi in range(tm // CHUNK):
        seq = rows.seq_of_group(i * (tm // CHUNK) + gi)
        m = mod_ref[pl.ds(seq, 1), :]
        shift, scale = m[:, 3 * d:4 * d], m[:, 4 * d:5 * d]
        x = x_ref[gi * CHUNK:(gi + 1) * CHUNK, :]
        ms = jnp.mean(x * x, axis=-1, keepdims=True)
        h2s_ref[gi * CHUNK:(gi + 1) * CHUNK, :] = (x * lax.rsqrt(ms + RMS_EPS) * g) * (1.0 + scale) + shift
    h2 = h2s_ref[...]
    lo = pltpu.bitcast(h2[:, :half].astype(BF16).astype(F32), jnp.uint32)
    hi = pltpu.bitcast(h2[:, half:].astype(BF16).astype(F32), jnp.uint32)
    h2_ref[...] = (hi & jnp.uint32(0xFFFF0000)) | (lo >> 16)

    logits = jnp.dot(h2, wr_ref[...], precision=lax.Precision.HIGHEST,
                     preferred_element_type=F32)
    lt = logits.T[0:ne, :] + br_ref[...]
    eidx = lax.broadcasted_iota(jnp.int32, (ne, tm), 0)
    cur = lt
    tops, sels, hots = [], [], []
    for _ in range(TOP_K):
        mx = jnp.max(cur, axis=0, keepdims=True)
        sel = jnp.min(jnp.where(cur == mx, eidx, ne), axis=0, keepdims=True)
        hot = eidx == sel
        tops.append(mx)
        sels.append(sel)
        hots.append(hot)
        cur = jnp.where(hot, -jnp.inf, cur)
    ex = [jnp.exp(t - tops[0]) for t in tops]
    tot = ex[0] + ex[1] + ex[2] + ex[3]
    oh = jnp.zeros((ne, tm), F32)
    for hot in hots:
        oh = oh + hot.astype(F32)
    prefix = jnp.dot(oh.astype(BF16), ut_ref[...], preferred_element_type=F32)
    base = cnt_ref[:, 0:1]
    pos = base + prefix
    for kk in range(TOP_K):
        e_ref[kk:kk + 1, :] = sels[kk]
        gate_ref[kk:kk + 1, :] = ex[kk] / tot
        rank_ref[kk:kk + 1, :] = jnp.sum(jnp.where(hots[kk], pos, 0.0), axis=0,
                                         keepdims=True).astype(jnp.int32)
    cnt_ref[...] = cnt_ref[...] + jnp.sum(oh, axis=1, keepdims=True)


def _router(rows, x1, g_ffn, mod, w_router, b_router, tm=256):
    r, d = x1.shape
    ne = w_router.shape[1]
    tm = min(tm, r)
    assert r % tm == 0 and ne % 8 == 0 and ne <= LANES
    wr_pad = jnp.zeros((d, LANES), F32).at[:, :ne].set(w_router.astype(F32))
    tt = np.arange(tm)
    ut = jnp.asarray((tt[:, None] < tt[None, :]).astype(np.float32), BF16)
    return pl.pallas_call(
        functools.partial(_router_kernel, rows, tm, d, ne),
        out_shape=(jax.ShapeDtypeStruct((r, d // 2), jnp.uint32),
                   jax.ShapeDtypeStruct((TOP_K, r), jnp.int32),
                   jax.ShapeDtypeStruct((TOP_K, r), F32),
                   jax.ShapeDtypeStruct((TOP_K, r), jnp.int32),
                   jax.ShapeDtypeStruct((ne, LANES), F32)),
        grid=(r // tm,),
        in_specs=[pl.BlockSpec((tm, d), lambda i: (i, 0)),
                  pl.BlockSpec((1, d), lambda i: (0, 0)),
                  pl.BlockSpec(mod.shape, lambda i: (0, 0)),
                  pl.BlockSpec((d, LANES), lambda i: (0, 0)),
                  pl.BlockSpec((ne, 1), lambda i: (0, 0)),
                  pl.BlockSpec((tm, tm), lambda i: (0, 0))],
        out_specs=(pl.BlockSpec((tm, d // 2), lambda i: (i, 0)),
                   pl.BlockSpec((TOP_K, tm), lambda i: (0, i)),
                   pl.BlockSpec((TOP_K, tm), lambda i: (0, i)),
                   pl.BlockSpec((TOP_K, tm), lambda i: (0, i)),
                   pl.BlockSpec((ne, LANES), lambda i: (0, 0))),
        scratch_shapes=[pltpu.VMEM((tm, d), F32)],
        compiler_params=_cparams(1),
        name="router",
    )(x1, g_ffn.reshape(1, d), mod, wr_pad, b_router.reshape(ne, 1).astype(F32), ut)


def _invert_kernel(tm, pstart_ref, e_ref, rank_ref, rt_ref):
    i = pl.program_id(0)

    @pl.when(i == 0)
    def _():
        def clear(r, carry):
            rt_ref[r] = 0
            return carry

        lax.fori_loop(0, rt_ref.shape[0], clear, 0, unroll=8)

    def body(t, carry):
        for kk in range(TOP_K):
            rt_ref[pstart_ref[e_ref[kk, t]] + rank_ref[kk, t]] = i * tm + t
        return carry

    lax.fori_loop(0, tm, body, 0, unroll=4)


def _invert(e_idx, rank, pstart, n_rows, tm=256):
    r = e_idx.shape[1]
    tm = min(tm, r)
    assert r % tm == 0
    return pl.pallas_call(
        functools.partial(_invert_kernel, tm),
        out_shape=jax.ShapeDtypeStruct((n_rows,), jnp.int32),
        grid_spec=pltpu.PrefetchScalarGridSpec(
            num_scalar_prefetch=1,
            grid=(r // tm,),
            in_specs=[pl.BlockSpec((TOP_K, tm), lambda i, *_: (0, i), memory_space=pltpu.SMEM),
                      pl.BlockSpec((TOP_K, tm), lambda i, *_: (0, i), memory_space=pltpu.SMEM)],
            out_specs=pl.BlockSpec(memory_space=pltpu.SMEM)),
        compiler_params=_cparams(1),
        name="moe_invert",
    )(pstart, e_idx, rank)


def _unpack_rows(xw):
    lo = pltpu.bitcast(xw << 16, F32).astype(BF16)
    hi = pltpu.bitcast(xw & jnp.uint32(0xFFFF0000), F32).astype(BF16)
    return lo, hi


def _gather_kernel(nused_ref, rt_cur_ref, rt_next_ref, h2_ref, xs_ref, buf_ref, sem):
    b = pl.program_id(0)
    n_used = nused_ref[0]
    half = buf_ref.shape[2]

    def row_copy(slot, src_row, dst_row):
        return pltpu.make_async_copy(h2_ref.at[pl.ds(src_row, 1), :],
                                     buf_ref.at[slot, pl.ds(dst_row, 1), :], sem.at[slot])

    def issue(rt_ref, slot):
        def one(t, carry):
            row_copy(slot, rt_ref[0, 0, t], t).start()
            return carry

        lax.fori_loop(0, MOE_BLOCK, one, 0, unroll=8)

    @pl.when(jnp.logical_and(b == 0, n_used > 0))
    def _():
        issue(rt_cur_ref, 0)

    @pl.when(b + 1 < n_used)
    def _():
        issue(rt_next_ref, (b + 1) % 2)

    @pl.when(b < n_used)
    def _():
        slot = b % 2

        def drain(t, carry):
            row_copy(slot, 0, 0).wait()
            return carry

        lax.fori_loop(0, MOE_BLOCK, drain, 0, unroll=8)
        lo, hi = _unpack_rows(buf_ref[slot])
        xs_ref[:, 0:half] = lo
        xs_ref[:, half:] = hi

    @pl.when(b >= n_used)
    def _():
        xs_ref[...] = jnp.zeros_like(xs_ref)


def _gather_rows(h2p, row_tok, n_used):
    r, half = h2p.shape
    n_blocks = row_tok.shape[0] // MOE_BLOCK
    rt3 = row_tok.reshape(n_blocks, 1, MOE_BLOCK)
    return pl.pallas_call(
        _gather_kernel,
        out_shape=jax.ShapeDtypeStruct((n_blocks * MOE_BLOCK, 2 * half), BF16),
        grid_spec=pltpu.PrefetchScalarGridSpec(
            num_scalar_prefetch=1,
            grid=(n_blocks,),
            in_specs=[pl.BlockSpec((1, 1, MOE_BLOCK), lambda b, *_: (b, 0, 0), memory_space=pltpu.SMEM),
                      pl.BlockSpec((1, 1, MOE_BLOCK), lambda b, *_: (jnp.minimum(b + 1, n_blocks - 1), 0, 0),
                                   memory_space=pltpu.SMEM),
                      pl.BlockSpec(memory_space=pl.ANY)],
            out_specs=pl.BlockSpec((MOE_BLOCK, 2 * half), lambda b, *_: (b, 0)),
            scratch_shapes=[pltpu.VMEM((2, MOE_BLOCK, half), jnp.uint32),
                            pltpu.SemaphoreType.DMA((2,))]),
        compiler_params=_cparams(1),
        name="moe_gather",
    )(n_used.reshape(1), rt3, rt3, h2p)


MOE_CACHE_BLOCKS = 6


class _ExpertRows:
    def __init__(self, src_hbm, slots_ref, sems, b0, nb, first_tile):
        self.src, self.slots, self.sems = src_hbm, slots_ref, sems
        self.b0, self.nb, self.first_tile = b0, nb, first_tile

    def slot_of(self, i):
        return jnp.where(i < MOE_CACHE_BLOCKS, i, MOE_CACHE_BLOCKS + i % 2)

    def _needs_load(self, i):
        return jnp.logical_or(self.first_tile, i >= MOE_CACHE_BLOCKS)

    def _copy(self, i):
        row0 = pl.multiple_of((self.b0 + i) * MOE_BLOCK, MOE_BLOCK)
        return pltpu.make_async_copy(self.src.at[pl.ds(row0, MOE_BLOCK), :],
                                     self.slots.at[self.slot_of(i)], self.sems.at[i % 2])

    def request(self, i):
        @pl.when(jnp.logical_and(i < self.nb, self._needs_load(i)))
        def _():
            self._copy(i).start()

    def arrive(self, i):
        @pl.when(self._needs_load(i))
        def _():
            self._copy(i).wait()

    def request_first(self, e):
        @pl.when(e == 0)
        def _():
            self.request(0)

    def request_next_expert(self, e, ne, last_tile, bstart_ref, nblk_ref):
        @pl.when(jnp.logical_and(last_tile, e + 1 < ne))
        def _():
            @pl.when(nblk_ref[e + 1] > 0)
            def _():
                row0 = pl.multiple_of(bstart_ref[e + 1] * MOE_BLOCK, MOE_BLOCK)
                pltpu.make_async_copy(self.src.at[pl.ds(row0, MOE_BLOCK), :],
                                      self.slots.at[0], self.sems.at[0]).start()


def _block_window(dst_hbm, blk, col0, width):
    row0 = pl.multiple_of(blk * MOE_BLOCK, MOE_BLOCK)
    return dst_hbm.at[pl.ds(row0, MOE_BLOCK), pl.ds(pl.multiple_of(col0, width), width)]


def _zero_tail_blocks(stage_ref, dst_hbm, sem, first_blk, col0, width):
    n_blocks = dst_hbm.shape[0] // MOE_BLOCK
    stage_ref[0] = jnp.zeros(stage_ref.shape[1:], stage_ref.dtype)

    def cp(blk):
        return pltpu.make_async_copy(stage_ref.at[0], _block_window(dst_hbm, blk, col0, width), sem.at[0])

    def start(blk, carry):
        cp(blk).start()
        return carry

    def wait(blk, carry):
        cp(blk).wait()
        return carry

    lax.fori_loop(first_blk, n_blocks, start, 0)
    lax.fori_loop(first_blk, n_blocks, wait, 0)


class _OutRing:
    def __init__(self, stage_ref, sems, cnt_ref, dst_hbm, width):
        self.stage, self.sems, self.cnt, self.dst, self.width = stage_ref, sems, cnt_ref, dst_hbm, width

    def init(self):
        self.cnt[0] = 0
        self.cnt[1] = 0

    def _await_next(self, limit):
        w = self.cnt[1]

        @pl.when(w < limit)
        def _():
            pltpu.make_async_copy(self.stage.at[w % 2], _block_window(self.dst, 0, 0, self.width),
                                  self.sems.at[w % 2]).wait()
            self.cnt[1] = w + 1

    def reserve(self):
        self._await_next(self.cnt[0] - 1)

    def push(self, value, blk, col0):
        c = self.cnt[0]
        self.stage[c % 2] = value
        pltpu.make_async_copy(self.stage.at[c % 2], _block_window(self.dst, blk, col0, self.width),
                              self.sems.at[c % 2]).start()
        self.cnt[0] = c + 1

    def drain(self):
        for _ in range(2):
            self._await_next(self.cnt[0])


MOE_CAST_CHUNKS = 4


def _cast_chunk(w_ref, wbf_ref, c):
    kc = wbf_ref.shape[0] // MOE_CAST_CHUNKS
    wbf_ref[c * kc:(c + 1) * kc, :] = w_ref[0, c * kc:(c + 1) * kc, :].astype(BF16)


def _first_block_dot(x_blk_ref, w_ref, wbf_ref):
    kc = wbf_ref.shape[0] // MOE_CAST_CHUNKS
    acc = None
    for c in range(MOE_CAST_CHUNKS):
        if c + 1 < MOE_CAST_CHUNKS:
            _cast_chunk(w_ref, wbf_ref, c + 1)
        part = jnp.dot(x_blk_ref[:, c * kc:(c + 1) * kc], wbf_ref[c * kc:(c + 1) * kc, :],
                       preferred_element_type=F32)
        acc = part if acc is None else acc + part
    return acc


def _moe_up_kernel(tf, nj, ne, bstart_ref, nblk_ref, xs_hbm, w_ref, b_ref, act_hbm,
                   wbf_ref, xb_ref, gt_ref, ostage_ref, cnt_ref, xsem, osem):
    g = pl.program_id(0)
    e, j = g // nj, g % nj
    nb, b0 = nblk_ref[e], bstart_ref[e]
    nslab = MOE_BLOCK // LANES
    xrows = _ExpertRows(xs_hbm, xb_ref, xsem, b0, nb, j == 0)
    ring = _OutRing(ostage_ref, osem, cnt_ref, act_hbm, tf)

    @pl.when(g == 0)
    def _():
        ring.init()

    def up_dot(i):
        return jnp.dot(xb_ref[xrows.slot_of(i)], wbf_ref[...], preferred_element_type=F32) + b_ref[0]

    def swiglu(gu):
        gut = gu.T
        for sl in range(nslab):
            gt_ref[sl] = gut[:, sl * LANES:(sl + 1) * LANES]
        parts = []
        for sl in range(nslab):
            gate = jnp.minimum(gt_ref[sl, pl.ds(0, tf, stride=2), :], SWIGLU_LIMIT)
            up = jnp.clip(gt_ref[sl, pl.ds(1, tf, stride=2), :], -SWIGLU_LIMIT, SWIGLU_LIMIT)
            parts.append((up + 1.0) * gate * _sigmoid(SWIGLU_ALPHA * gate))
        return jnp.concatenate(parts, axis=1).T.astype(BF16)

    @pl.when(nb > 0)
    def _():
        xrows.request_first(e)
        _cast_chunk(w_ref, wbf_ref, 0)
        xrows.arrive(0)
        xrows.request(1)
        gu0 = _first_block_dot(xb_ref.at[0], w_ref, wbf_ref) + b_ref[0]

        def body(i, gu_prev):
            xrows.arrive(i)
            xrows.request(i + 1)
            ring.reserve()
            gu = up_dot(i)
            ring.push(swiglu(gu_prev), b0 + i - 1, j * tf)
            return gu

        gu_last = lax.fori_loop(1, nb, body, gu0)
        ring.reserve()
        ring.push(swiglu(gu_last), b0 + nb - 1, j * tf)

    xrows.request_next_expert(e, ne, j == nj - 1, bstart_ref, nblk_ref)

    @pl.when(e == ne - 1)
    def _():
        ring.drain()
        _zero_tail_blocks(ostage_ref, act_hbm, osem, b0 + nb, j * tf, tf)


def _moe_up(xs, w_gu, b_gu, bstart, nblk, tf=256):
    n_rows, d = xs.shape
    ne, _, f2 = w_gu.shape
    f = f2 // 2
    tf = min(tf, f)
    assert f % tf == 0 and w_gu.shape[1] == d
    nj = f // tf
    return pl.pallas_call(
        functools.partial(_moe_up_kernel, tf, nj, ne),
        out_shape=jax.ShapeDtypeStruct((n_rows, f), BF16),
        grid_spec=pltpu.PrefetchScalarGridSpec(
            num_scalar_prefetch=2,
            grid=(ne * nj,),
            in_specs=[pl.BlockSpec(memory_space=pl.ANY),
                      pl.BlockSpec((1, d, 2 * tf), lambda g, *_: (g // nj, 0, g % nj)),
                      pl.BlockSpec((1, 1, 2 * tf), lambda g, *_: (g // nj, 0, g % nj))],
            out_specs=pl.BlockSpec(memory_space=pl.ANY),
            scratch_shapes=[pltpu.VMEM((d, 2 * tf), BF16),
                            pltpu.VMEM((MOE_CACHE_BLOCKS + 2, MOE_BLOCK, d), BF16),
                            pltpu.VMEM((MOE_BLOCK // LANES, 2 * tf, LANES), F32),
                            pltpu.VMEM((2, MOE_BLOCK, tf), BF16),
                            pltpu.SMEM((2,), jnp.int32),
                            pltpu.SemaphoreType.DMA((2,)),
                            pltpu.SemaphoreType.DMA((2,))]),
        compiler_params=_cparams(1),
        name="moe_up",
    )(bstart, nblk, xs, w_gu, b_gu.reshape(ne, 1, f2))


def _moe_down_kernel(tn, nj, ne, bstart_ref, nblk_ref, act_hbm, w_ref, b_ref, yb_hbm,
                     wbf_ref, ab_ref, ostage_ref, cnt_ref, asem, osem):
    g = pl.program_id(0)
    e, j = g // nj, g % nj
    nb, b0 = nblk_ref[e], bstart_ref[e]
    arows = _ExpertRows(act_hbm, ab_ref, asem, b0, nb, j == 0)
    ring = _OutRing(ostage_ref, osem, cnt_ref, yb_hbm, tn)

    @pl.when(g == 0)
    def _():
        ring.init()

    @pl.when(nb > 0)
    def _():
        arows.request_first(e)
        _cast_chunk(w_ref, wbf_ref, 0)
        arows.arrive(0)
        arows.request(1)
        ring.reserve()
        ring.push(_first_block_dot(ab_ref.at[0], w_ref, wbf_ref) + b_ref[0], b0, j * tn)

        def body(i, carry):
            arows.arrive(i)
            arows.request(i + 1)
            ring.reserve()
            y = jnp.dot(ab_ref[arows.slot_of(i)], wbf_ref[...], preferred_element_type=F32) + b_ref[0]
            ring.push(y, b0 + i, j * tn)
            return carry

        lax.fori_loop(1, nb, body, 0)

    arows.request_next_expert(e, ne, j == nj - 1, bstart_ref, nblk_ref)

    @pl.when(e == ne - 1)
    def _():
        ring.drain()
        _zero_tail_blocks(ostage_ref, yb_hbm, osem, b0 + nb, j * tn, tn)


def _moe_down(act, w_d, b_d, bstart, nblk, tn=512):
    n_rows, f = act.shape
    ne, _, d = w_d.shape
    tn = min(tn, d)
    assert d % tn == 0
    nj = d // tn
    return pl.pallas_call(
        functools.partial(_moe_down_kernel, tn, nj, ne),
        out_shape=jax.ShapeDtypeStruct((n_rows, d), F32),
        grid_spec=pltpu.PrefetchScalarGridSpec(
            num_scalar_prefetch=2,
            grid=(ne * nj,),
            in_specs=[pl.BlockSpec(memory_space=pl.ANY),
                      pl.BlockSpec((1, f, tn), lambda g, *_: (g // nj, 0, g % nj)),
                      pl.BlockSpec((1, 1, tn), lambda g, *_: (g // nj, 0, g % nj))],
            out_specs=pl.BlockSpec(memory_space=pl.ANY),
            scratch_shapes=[pltpu.VMEM((f, tn), BF16),
                            pltpu.VMEM((MOE_CACHE_BLOCKS + 2, MOE_BLOCK, f), BF16),
                            pltpu.VMEM((2, MOE_BLOCK, tn), F32),
                            pltpu.SMEM((2,), jnp.int32),
                            pltpu.SemaphoreType.DMA((2,)),
                            pltpu.SemaphoreType.DMA((2,))]),
        compiler_params=_cparams(1),
        name="moe_down",
    )(bstart, nblk, act, w_d, b_d.reshape(ne, 1, d))


def _combine_kernel(rows, tm, d, n_tiles, pstart_ref, e_ref, rank_ref, en_ref, rankn_ref, gt_ref,
                    x_ref, mod_ref, gf_ref, yb_ref, yp_ref, ys_ref, buf_ref, sem):
    i = pl.program_id(0)
    is_prompt = i < rows.groups_p * CHUNK // tm
    slot = i % 2

    def row_copy(s, dst_k, dst_t, src_row):
        return pltpu.make_async_copy(yb_ref.at[pl.ds(src_row, 1), :],
                                     buf_ref.at[s, dst_k, pl.ds(dst_t, 1), :], sem.at[s])

    def issue(eref, rref, s):
        def one(t, carry):
            for kk in range(TOP_K):
                src = pstart_ref[eref[kk, t]] + rref[kk, t]
                row_copy(s, kk, t, src).start()
            return carry

        lax.fori_loop(0, tm, one, 0, unroll=2)

    @pl.when(i == 0)
    def _():
        issue(e_ref, rank_ref, 0)

    @pl.when(i + 1 < n_tiles)
    def _():
        issue(en_ref, rankn_ref, (i + 1) % 2)

    def drain(t, carry):
        for kk in range(TOP_K):
            row_copy(slot, 0, 0, 0).wait()
        return carry

    lax.fori_loop(0, tm, drain, 0, unroll=2)

    gf = gf_ref[...]
    for gi in range(tm // CHUNK):
        rs = slice(gi * CHUNK, (gi + 1) * CHUNK)
        seq = rows.seq_of_group(i * (tm // CHUNK) + gi)
        gate2 = mod_ref[pl.ds(seq, 1), :]
        ff = jnp.zeros((CHUNK, d), F32)
        for kk in range(TOP_K):
            ff = ff + buf_ref[slot, kk, rs, :] * gt_ref[rs, kk:kk + 1]
        x2 = x_ref[rs, :] + gate2 * ff
        ms = jnp.mean(x2 * x2, axis=-1, keepdims=True)
        y = x2 * lax.rsqrt(ms + RMS_EPS) * gf

        @pl.when(is_prompt)
        def _():
            yp_ref[rs, :] = y

        @pl.when(jnp.logical_not(is_prompt))
        def _():
            ys_ref[rs, :] = y


def _combine(rows, x1, mod, g_final, yb, e_idx, rank, gates_t, pstart, tm=128):
    r, d = x1.shape
    tm = min(tm, r)
    assert r % tm == 0
    gate2_blk = 5
    yp_spec, ys_spec = _split_row_specs(rows, tm, d)
    n_tiles = r // tm

    def nxt(i, *_):
        return (0, jnp.minimum(i + 1, n_tiles - 1))

    return pl.pallas_call(
        functools.partial(_combine_kernel, rows, tm, d, n_tiles),
        out_shape=(jax.ShapeDtypeStruct((rows.groups_p * CHUNK, d), F32),
                   jax.ShapeDtypeStruct((r - rows.groups_p * CHUNK, d), F32)),
        grid_spec=pltpu.PrefetchScalarGridSpec(
            num_scalar_prefetch=1,
            grid=(n_tiles,),
            in_specs=[pl.BlockSpec((TOP_K, tm), lambda i, *_: (0, i), memory_space=pltpu.SMEM),
                      pl.BlockSpec((TOP_K, tm), lambda i, *_: (0, i), memory_space=pltpu.SMEM),
                      pl.BlockSpec((TOP_K, tm), nxt, memory_space=pltpu.SMEM),
                      pl.BlockSpec((TOP_K, tm), nxt, memory_space=pltpu.SMEM),
                      pl.BlockSpec((tm, TOP_K), lambda i, *_: (i, 0)),
                      pl.BlockSpec((tm, d), lambda i, *_: (i, 0)),
                      pl.BlockSpec((mod.shape[0], d), lambda i, *_: (0, gate2_blk)),
                      pl.BlockSpec((1, d), lambda i, *_: (0, 0)),
                      pl.BlockSpec(memory_space=pl.ANY)],
            out_specs=(yp_spec, ys_spec),
            scratch_shapes=[pltpu.VMEM((2, TOP_K, tm, d), F32),
                            pltpu.SemaphoreType.DMA((2,))]),
        compiler_params=_cparams(1),
        name="moe_combine",
    )(pstart, e_idx, rank, e_idx, rank, gates_t, x1, mod, g_final.reshape(1, d), yb)


def kernel(x_prompt, x_sample, c_prompt, c_sample, state_hgrn_S, state_conv, state_mlstm_C,
           state_mlstm_n, state_mlstm_m, w_ada, b_ada, g_mix, g_ffn, w_in, lb_logits, conv_w,
           conv_b, b_igate, b_fgate, g_hgrn, g_mlstm, w_out, w_router, b_router, w_gate_up,
           b_gate_up, w_down, b_down, g_final):
    depth = w_ada.shape[0]
    assert depth == 1, "single-layer trunk"
    bp, tp, d = x_prompt.shape
    bs, ts, _ = x_sample.shape
    rows = _Rows(bp, tp, bs, ts)
    _, _, nh_h, dk, dv = state_hgrn_S.shape
    _, _, nh_m, dqk, dvm = state_mlstm_C.shape
    hw = nh_h * dk
    mw = nh_m * dvm
    n_main = 4 * hw + 3 * mw
    assert w_in.shape[2] == n_main + 2 * nh_m and 2 * nh_m <= GATE_LANES
    assert mw == hw, "column sections of the input projection are addressed in hw-wide blocks"
    ne = w_router.shape[2]

    lb = jax.nn.softmax(lb_logits.astype(F32), axis=0)[0]

    x_p = x_prompt.reshape(bp * tp, d)
    x_s = x_sample.reshape(bs * ts, d)
    n_c = bp + bs
    n_c_pad = -(-n_c // 8) * 8
    c_pad = jnp.zeros((n_c_pad, d), F32).at[:n_c].set(jnp.concatenate([c_prompt, c_sample], axis=0))
    mod = _ada(c_pad, w_ada[0], b_ada[0])

    w_in_t = jnp.swapaxes(w_in, 1, 2)
    h_all, gates = _prep(rows, x_p, x_s, g_mix[0], mod, w_in_t, n_main)
    z = _inproj(h_all, w_in_t, n_main)

    og, st = _hgrn(rows, z, state_hgrn_S[0], lb, g_hgrn[0])
    hm, conv_new, c_new, n_new, m_new = _mlstm(
        rows, z, 4, gates, state_conv[0], state_mlstm_C[0], state_mlstm_n[0], state_mlstm_m[0],
        conv_w[0], conv_b[0], b_igate[0], b_fgate[0], g_mlstm[0])

    x1 = _outproj(rows, og, hm, w_out[0], x_p, x_s, mod)

    h2p, e_idx, gate_k, rank, cnt = _router(rows, x1, g_ffn[0], mod, w_router[0], b_router[0])

    counts = cnt[:, 0].astype(jnp.int32)
    nblk_e = (counts + MOE_BLOCK - 1) // MOE_BLOCK
    padded = nblk_e * MOE_BLOCK
    pad_end = jnp.cumsum(padded)
    pstart = jnp.concatenate([jnp.zeros((1,), jnp.int32), pad_end]).astype(jnp.int32)
    n_blocks_max = -(-(rows.rows * TOP_K) // MOE_BLOCK) + ne
    n_rows = n_blocks_max * MOE_BLOCK
    blk_start_e = (pstart[:ne] // MOE_BLOCK).astype(jnp.int32)
    n_used = (pstart[ne] // MOE_BLOCK).astype(jnp.int32)

    row_tok = _invert(e_idx, rank, pstart, n_rows)
    xs = _gather_rows(h2p, row_tok, n_used)
    act = _moe_up(xs, w_gate_up[0], b_gate_up[0], blk_start_e, nblk_e.astype(jnp.int32))
    yb = _moe_down(act, w_down[0], b_down[0], blk_start_e, nblk_e.astype(jnp.int32))

    y_p, y_s = _combine(rows, x1, mod, g_final, yb, e_idx, rank, gate_k.T, pstart)

    y_prompt = y_p.reshape(bp, tp, d)
    y_sample = y_s.reshape(bs, ts, d)
    s_all = jnp.swapaxes(st, 2, 3)
    m_all = m_new.reshape(rows.nseq, nh_m)

    def split(a):
        return a[:bp][None], a[bp:][None]

    p_s, s_s = split(s_all)
    p_conv, s_conv = split(conv_new)
    p_c, s_c = split(c_new)
    p_n, s_n = split(n_new)
    p_m, s_m = split(m_all)
    return (y_prompt, y_sample, p_s, p_conv, p_c, p_n, p_m, s_s, s_conv, s_c, s_n, s_m)
```

```python
import functools

import numpy as np
import jax
import jax.numpy as jnp
from jax import lax
from jax.experimental import pallas as pl
from jax.experimental.pallas import tpu as pltpu

F32 = jnp.float32
BF16 = jnp.bfloat16

CHUNK = 64
TOP_K = 4
CONV_K = 4
RMS_EPS = 1e-6
SWIGLU_LIMIT = 7.0
SWIGLU_ALPHA = 1.702
MOE_BLOCK = 256
LANES = 128
GATE_LANES = 128
VMEM_LIMIT = 60 * 1024 * 1024
HGRN_LEVELS = (32, 16, 8, 4, 2, 1)


def _cparams(n_axes, vmem=VMEM_LIMIT):
    return pltpu.CompilerParams(dimension_semantics=("arbitrary",) * n_axes,
                                vmem_limit_bytes=vmem)


def _sigmoid(x):
    return 1.0 / (1.0 + jnp.exp(-x))


def _silu(x):
    return x * _sigmoid(x)


class _Rows:
    def __init__(self, bp, tp, bs, ts):
        assert tp % CHUNK == 0 and ts % CHUNK == 0
        self.bp, self.tp, self.bs, self.ts = bp, tp, bs, ts
        self.nblk_p, self.nblk_s = tp // CHUNK, ts // CHUNK
        self.groups_p = bp * self.nblk_p
        self.groups = self.groups_p + bs * self.nblk_s
        self.rows = self.groups * CHUNK
        self.nseq = bp + bs

    def seq_of_group(self, g):
        return jnp.where(g < self.groups_p, g // self.nblk_p,
                         self.bp + (g - self.groups_p) // self.nblk_s)

    def blk_in_seq(self, g):
        return jnp.where(g < self.groups_p, g % self.nblk_p, (g - self.groups_p) % self.nblk_s)

    def nblk_of_group(self, g):
        return jnp.where(g < self.groups_p, self.nblk_p, self.nblk_s)


def _ada_kernel(c_ref, w_ref, b_ref, o_ref):
    c = c_ref[...]
    s = _silu(c).astype(BF16)
    o_ref[...] = jnp.dot(s, w_ref[...].astype(BF16), preferred_element_type=F32) + b_ref[...]


def _ada(c_pad, w_ada, b_ada, tn=512):
    m, d = c_pad.shape
    n = w_ada.shape[1]
    tn = min(tn, n)
    assert n % tn == 0
    return pl.pallas_call(
        _ada_kernel,
        out_shape=jax.ShapeDtypeStruct((m, n), F32),
        grid=(n // tn,),
        in_specs=[pl.BlockSpec((m, d), lambda j: (0, 0)),
                  pl.BlockSpec((d, tn), lambda j: (0, j)),
                  pl.BlockSpec((1, tn), lambda j: (0, j))],
        out_specs=pl.BlockSpec((m, tn), lambda j: (0, j)),
        compiler_params=_cparams(1),
        name="ada_mod",
    )(c_pad, w_ada, b_ada.reshape(1, n))


def _prep_kernel(rows, tm, d, n_gate, xp_ref, xs_ref, g_ref, mod_ref, wtail_ref,
                 h_ref, gates_ref, wg_ref, hf_ref):
    i = pl.program_id(0)
    is_prompt = i < rows.groups_p * CHUNK // tm

    @pl.when(i == 0)
    def _():
        lanes = wg_ref.shape[1]
        padded = jnp.concatenate([wtail_ref[0], jnp.zeros((lanes - n_gate, d), F32)], axis=0)
        wg_ref[...] = padded.T

    g = g_ref[...]
    for gi in range(tm // CHUNK):
        seq = rows.seq_of_group(i * (tm // CHUNK) + gi)
        m = mod_ref[pl.ds(seq, 1), :]
        shift, scale = m[:, 0:d], m[:, d:2 * d]
        rs = slice(gi * CHUNK, (gi + 1) * CHUNK)
        x = jnp.where(is_prompt, xp_ref[rs, :], xs_ref[rs, :])
        ms = jnp.mean(x * x, axis=-1, keepdims=True)
        y = x * lax.rsqrt(ms + RMS_EPS) * g
        h = y * (1.0 + scale) + shift
        h_ref[rs, :] = h.astype(BF16)
        hf_ref[rs, :] = h
    gates_ref[...] = jnp.dot(hf_ref[...], wg_ref[...], precision=lax.Precision.HIGHEST,
                             preferred_element_type=F32)


def _split_row_specs(rows, tm, tn, col_major=False):
    n_pt = rows.groups_p * CHUNK // tm
    n_st = rows.rows // tm - n_pt
    assert n_pt * tm == rows.groups_p * CHUNK and n_st >= 1

    def ij(args):
        return (args[1], args[0]) if col_major else (args[0], 0)

    def p_map(*args):
        i, j = ij(args)
        return (jnp.minimum(i, n_pt - 1), j)

    def s_map(*args):
        i, j = ij(args)
        return (jnp.maximum(i - n_pt, 0), j)

    return pl.BlockSpec((tm, tn), p_map), pl.BlockSpec((tm, tn), s_map)


def _prep(rows, x_p, x_s, g_mix, mod, w_in_t, n_main, tm=256):
    d = x_p.shape[1]
    r = rows.rows
    tm = min(tm, x_s.shape[0])
    n_gate = w_in_t.shape[1] - n_main
    assert r % tm == 0 and n_gate % 8 == 0 and n_main % n_gate == 0 and n_gate <= GATE_LANES
    xp_spec, xs_spec = _split_row_specs(rows, tm, d)
    return pl.pallas_call(
        functools.partial(_prep_kernel, rows, tm, d, n_gate),
        out_shape=(jax.ShapeDtypeStruct((r, d), BF16),
                   jax.ShapeDtypeStruct((r, GATE_LANES), F32)),
        grid=(r // tm,),
        in_specs=[xp_spec, xs_spec,
                  pl.BlockSpec((1, d), lambda i: (0, 0)),
                  pl.BlockSpec(mod.shape, lambda i: (0, 0)),
                  pl.BlockSpec((1, n_gate, d), lambda i: (0, n_main // n_gate, 0))],
        out_specs=(pl.BlockSpec((tm, d), lambda i: (i, 0)),
                   pl.BlockSpec((tm, GATE_LANES), lambda i: (i, 0))),
        scratch_shapes=[pltpu.VMEM((d, GATE_LANES), F32),
                        pltpu.VMEM((tm, d), F32)],
        compiler_params=_cparams(1),
        name="prep_norm_mod",
    )(x_p, x_s, g_mix.reshape(1, d), mod, w_in_t)


def _inproj_kernel(h_ref, w_ref, z_ref, wb_ref):
    @pl.when(pl.program_id(1) == 0)
    def _():
        wb_ref[...] = w_ref[0].astype(BF16)

    z_ref[...] = lax.dot_general(h_ref[...], wb_ref[...], (((1,), (1,)), ((), ())),
                                 preferred_element_type=F32)


def _inproj(h_all, w_in_t, n_main, tm=1024, tn=512):
    r, d = h_all.shape
    tm, tn = min(tm, r), min(tn, n_main)
    assert r % tm == 0 and n_main % tn == 0
    return pl.pallas_call(
        _inproj_kernel,
        out_shape=jax.ShapeDtypeStruct((r, n_main), F32),
        grid=(n_main // tn, r // tm),
        in_specs=[pl.BlockSpec((tm, d), lambda j, i: (i, 0)),
                  pl.BlockSpec((1, tn, d), lambda j, i: (0, j, 0))],
        out_specs=pl.BlockSpec((tm, tn), lambda j, i: (i, j)),
        scratch_shapes=[pltpu.VMEM((tn, d), BF16)],
        compiler_params=_cparams(2),
        name="in_proj",
    )(h_all, w_in_t)


def _hgrn_consts():
    L = CHUNK
    t = np.arange(L)[:, None]
    s = np.arange(L)[None, :]
    mats = [(s <= t)]
    masks = []
    for m in HGRN_LEVELS:
        start = (t // (2 * m)) * (2 * m)
        mats.append(s <= start + m - 1)
        masks.append((t // (2 * m) == s // (2 * m)) & (t % (2 * m) >= m) & (s % (2 * m) < m))
    mstack = np.concatenate(mats, axis=0).astype(np.float32)
    masks = np.stack(masks, axis=0).astype(np.float32)
    return jnp.asarray(mstack, BF16), jnp.asarray(masks, F32)


def _split3(x):
    x1 = x.astype(BF16)
    r1 = x - x1.astype(F32)
    x2 = r1.astype(BF16)
    x3 = (r1 - x2.astype(F32)).astype(BF16)
    return x1, x2, x3


def _hgrn_kernel(rows, nh, dk, dv, hq_ref, hf_ref, hi_ref, hg_ref, s0_ref, lb_ref, gn_ref,
                 mstack_ref, masks_ref, og_ref, st_ref, r_ref, o_ref):
    L = CHUNK
    g = pl.program_id(0)
    blk = rows.blk_in_seq(g)
    is_prompt = g < rows.groups_p

    @pl.when(jnp.logical_and(blk == 0, is_prompt))
    def _():
        st_ref[...] = jnp.zeros_like(st_ref)

    @pl.when(jnp.logical_and(blk == 0, jnp.logical_not(is_prompt)))
    def _():
        for h in range(nh):
            st_ref[0, h] = s0_ref[0, h].T

    lb = lb_ref[...]
    f = lb + (1.0 - lb) * _sigmoid(hf_ref[...])
    lf = jnp.log(f)
    mstack = mstack_ref[...]
    p1, p2, p3 = _split3(lf)
    r_ref[...] = (jnp.dot(mstack, p1, preferred_element_type=F32)
                  + jnp.dot(mstack, p2, preferred_element_type=F32)
                  + jnp.dot(mstack, p3, preferred_element_type=F32))

    for h in range(nh):
        cs = slice(h * dk, (h + 1) * dk)
        vs = slice(h * dv, (h + 1) * dv)
        q = _silu(hq_ref[:, cs]) * (dk ** -0.5)
        k = 1.0 - f[:, cs]
        v = hi_ref[:, vs]
        vb = v.astype(BF16)
        b = r_ref[0:L, cs]
        st = st_ref[0, h]
        qe = (q * jnp.exp(b)).astype(BF16)
        o = lax.dot_general(qe, st.astype(BF16), (((1,), (1,)), ((), ())),
                            preferred_element_type=F32)
        a = jnp.zeros((L, L), F32)
        for li in range(len(HGRN_LEVELS)):
            rl = r_ref[(li + 1) * L:(li + 2) * L, cs]
            e = jnp.exp(-jnp.abs(b - rl))
            al = lax.dot_general((q * e).astype(BF16), (k * e).astype(BF16),
                                 (((1,), (1,)), ((), ())), preferred_element_type=F32)
            a = a + al * masks_ref[li]
        diag = jnp.sum(q * k, axis=-1, keepdims=True)
        o = o + jnp.dot(a.astype(BF16), vb, preferred_element_type=F32) + diag * v
        o_ref[:, vs] = o
        b_last = b[L - 1:L, :]
        kd = (k * jnp.exp(b_last - b)).astype(BF16)
        st_ref[0, h] = st * jnp.exp(b_last) + lax.dot_general(
            vb, kd, (((0,), (0,)), ((), ())), preferred_element_type=F32)

    o = o_ref[...]
    ms = jnp.mean(o * o, axis=-1, keepdims=True)
    og = o * lax.rsqrt(ms + RMS_EPS) * gn_ref[...] * _silu(hg_ref[...])
    og_ref[...] = og.astype(BF16)


def _hgrn(rows, z, s0, lb, g_hgrn):
    bs, nh, dk, dv = s0.shape
    hw = nh * dk
    assert nh * dv == hw and hw % LANES == 0
    mstack, masks = _hgrn_consts()
    n_lv = len(HGRN_LEVELS)
    L = CHUNK

    def zspec(sec):
        return pl.BlockSpec((L, hw), lambda g, sec=sec: (g, sec))

    og, st = pl.pallas_call(
        functools.partial(_hgrn_kernel, rows, nh, dk, dv),
        out_shape=(jax.ShapeDtypeStruct((rows.rows, hw), BF16),
                   jax.ShapeDtypeStruct((rows.nseq, nh, dv, dk), F32)),
        grid=(rows.groups,),
        in_specs=[zspec(0), zspec(1), zspec(2), zspec(3),
                  pl.BlockSpec((1, nh, dk, dv),
                               lambda g: (jnp.maximum(rows.seq_of_group(g) - rows.bp, 0), 0, 0, 0)),
                  pl.BlockSpec((1, hw), lambda g: (0, 0)),
                  pl.BlockSpec((1, hw), lambda g: (0, 0)),
                  pl.BlockSpec(mstack.shape, lambda g: (0, 0)),
                  pl.BlockSpec(masks.shape, lambda g: (0, 0, 0))],
        out_specs=(pl.BlockSpec((L, hw), lambda g: (g, 0)),
                   pl.BlockSpec((1, nh, dv, dk), lambda g: (rows.seq_of_group(g), 0, 0, 0))),
        scratch_shapes=[pltpu.VMEM(((n_lv + 1) * L, hw), F32),
                        pltpu.VMEM((L, hw), F32)],
        compiler_params=_cparams(1),
        name="hgrn2",
    )(z, z, z, z, s0, lb.reshape(1, hw), g_hgrn.reshape(1, hw), mstack, masks)
    return og, st


def _log_sigmoid(x):
    return jnp.minimum(x, 0.0) - jnp.log(1.0 + jnp.exp(-jnp.abs(x)))


def _mlstm_kernel(rows, nh, dqk, dv, mqk_ref, mv_ref, mo_ref, gates_ref, conv0_ref, c0_ref,
                  n0_ref, m0_ref, cw_ref, cb_ref, gbias_ref, gn_ref,
                  hm_ref, conv_ref, c_ref, n_ref, m_ref, ubuf_ref):
    L = CHUNK
    kq = nh * dqk
    pad = 8
    g = pl.program_id(0)
    blk = rows.blk_in_seq(g)
    is_prompt = g < rows.groups_p

    @pl.when(jnp.logical_and(blk == 0, is_prompt))
    def _():
        ubuf_ref[0:pad, :] = jnp.zeros((pad, 2 * kq), F32)
        c_ref[...] = jnp.zeros_like(c_ref)
        n_ref[...] = jnp.zeros_like(n_ref)
        m_ref[...] = jnp.zeros_like(m_ref)

    @pl.when(jnp.logical_and(blk == 0, jnp.logical_not(is_prompt)))
    def _():
        ubuf_ref[0:pad, :] = jnp.zeros((pad, 2 * kq), F32)
        ubuf_ref[pad - (CONV_K - 1):pad, :] = conv0_ref[0]
        c_ref[...] = c0_ref[...]
        n_ref[...] = n0_ref[...]
        m_ref[...] = m0_ref[...]

    ubuf_ref[pad:pad + L, :] = mqk_ref[...]
    acc = cb_ref[...] + jnp.zeros((L, 2 * kq), F32)
    for j in range(CONV_K):
        off = pad - (CONV_K - 1) + j
        acc = acc + ubuf_ref[off:off + L, :] * cw_ref[j:j + 1, :]
    new_tail = ubuf_ref[pad + L - (CONV_K - 1):pad + L, :]
    conv_ref[0] = new_tail
    ubuf_ref[pad - (CONV_K - 1):pad, :] = new_tail
    qk = _silu(acc)

    gt = gates_ref[...] + gbias_ref[...]
    lane = lax.broadcasted_iota(jnp.int32, gt.shape, 1)
    pg = jnp.where(lane < nh, gt, _log_sigmoid(gt))
    pgt = pg.T
    ti = lax.broadcasted_iota(jnp.int32, (L, L), 0)
    si = lax.broadcasted_iota(jnp.int32, (L, L), 1)
    tri = si <= ti

    for h in range(nh):
        q = qk[:, h * dqk:(h + 1) * dqk]
        k = qk[:, kq + h * dqk:kq + (h + 1) * dqk] * (dqk ** -0.5)
        v = mv_ref[:, h * dv:(h + 1) * dv]
        qb, kb, vb = q.astype(BF16), k.astype(BF16), v.astype(BF16)
        ig_c, lf_c = pg[:, h:h + 1], pg[:, nh + h:nh + h + 1]
        ig_r, lf_r = pgt[h:h + 1, :], pgt[nh + h:nh + h + 1, :]
        b_c = jnp.sum(jnp.where(tri, lf_r, 0.0), axis=1, keepdims=True)
        b_r = jnp.sum(jnp.where(ti <= si, lf_c, 0.0), axis=0, keepdims=True)
        m_prev = m_ref[0, :, h:h + 1]
        log_d = jnp.where(tri, b_c - b_r + ig_r, -jnp.inf)
        log_inter = b_c + m_prev
        m_t = jnp.maximum(jnp.max(log_d, axis=1, keepdims=True), log_inter)
        dm = jnp.exp(log_d - m_t)
        s_mat = lax.dot_general(qb, kb, (((1,), (1,)), ((), ())), preferred_element_type=F32) * dm
        w_inter = jnp.exp(log_inter - m_t)
        c_h = c_ref[0, h]
        n_h = n_ref[0, h:h + 1, :]
        num = (jnp.dot(s_mat.astype(BF16), vb, preferred_element_type=F32)
               + w_inter * jnp.dot(qb, c_h.astype(BF16), preferred_element_type=F32))
        den = (jnp.sum(s_mat, axis=1, keepdims=True)
               + w_inter * jnp.sum(q * n_h, axis=1, keepdims=True))
        hh = num / jnp.maximum(jnp.abs(den), jnp.exp(-m_t))
        b_last = b_c[L - 1:L, :]
        lw_c = b_last - b_c + ig_c
        lw_r = b_last - b_r + ig_r
        m_new = jnp.maximum(b_last + m_prev, jnp.max(lw_r, axis=1, keepdims=True))
        decay = jnp.exp(b_last + m_prev - m_new)
        kw = k * jnp.exp(lw_c - m_new)
        c_ref[0, h] = decay * c_h + lax.dot_general(
            kw.astype(BF16), vb, (((0,), (0,)), ((), ())), preferred_element_type=F32)
        n_ref[0, h:h + 1, :] = decay * n_h + jnp.sum(kw, axis=0, keepdims=True)
        m_ref[0, :, h:h + 1] = m_new
        ms = jnp.mean(hh * hh, axis=-1, keepdims=True)
        hn = hh * lax.rsqrt(ms + RMS_EPS) * gn_ref[:, h * dv:(h + 1) * dv]
        hm_ref[:, h * dv:(h + 1) * dv] = (hn * _sigmoid(mo_ref[:, h * dv:(h + 1) * dv])).astype(BF16)


def _mlstm(rows, z, sec_qk, gates, conv0, c0, n0, m0, conv_w, conv_b, b_igate, b_fgate, g_mlstm):
    bs, nh, dqk, dv = c0.shape
    kq = nh * dqk
    mw = nh * dv
    L = CHUNK
    assert 2 * kq == mw, "q/k conv width must equal the value width for the column sections"
    gbias = jnp.zeros((1, GATE_LANES), F32)
    gbias = gbias.at[0, 0:nh].set(b_igate.astype(F32)).at[0, nh:2 * nh].set(b_fgate.astype(F32))

    def sseq(g):
        return jnp.maximum(rows.seq_of_group(g) - rows.bp, 0)

    outs = pl.pallas_call(
        functools.partial(_mlstm_kernel, rows, nh, dqk, dv),
        out_shape=(jax.ShapeDtypeStruct((rows.rows, mw), BF16),
                   jax.ShapeDtypeStruct((rows.nseq, CONV_K - 1, 2 * kq), F32),
                   jax.ShapeDtypeStruct((rows.nseq, nh, dqk, dv), F32),
                   jax.ShapeDtypeStruct((rows.nseq, nh, dqk), F32),
                   jax.ShapeDtypeStruct((rows.nseq, 1, nh), F32)),
        grid=(rows.groups,),
        in_specs=[pl.BlockSpec((L, mw), lambda g: (g, sec_qk)),
                  pl.BlockSpec((L, mw), lambda g: (g, sec_qk + 1)),
                  pl.BlockSpec((L, mw), lambda g: (g, sec_qk + 2)),
                  pl.BlockSpec((L, GATE_LANES), lambda g: (g, 0)),
                  pl.BlockSpec((1, CONV_K - 1, 2 * kq), lambda g: (sseq(g), 0, 0)),
                  pl.BlockSpec((1, nh, dqk, dv), lambda g: (sseq(g), 0, 0, 0)),
                  pl.BlockSpec((1, nh, dqk), lambda g: (sseq(g), 0, 0)),
                  pl.BlockSpec((1, 1, nh), lambda g: (sseq(g), 0, 0)),
                  pl.BlockSpec((CONV_K, 2 * kq), lambda g: (0, 0)),
                  pl.BlockSpec((1, 2 * kq), lambda g: (0, 0)),
                  pl.BlockSpec((1, GATE_LANES), lambda g: (0, 0)),
                  pl.BlockSpec((1, mw), lambda g: (0, 0))],
        out_specs=(pl.BlockSpec((L, mw), lambda g: (g, 0)),
                   pl.BlockSpec((1, CONV_K - 1, 2 * kq), lambda g: (rows.seq_of_group(g), 0, 0)),
                   pl.BlockSpec((1, nh, dqk, dv), lambda g: (rows.seq_of_group(g), 0, 0, 0)),
                   pl.BlockSpec((1, nh, dqk), lambda g: (rows.seq_of_group(g), 0, 0)),
                   pl.BlockSpec((1, 1, nh), lambda g: (rows.seq_of_group(g), 0, 0))),
        scratch_shapes=[pltpu.VMEM((8 + L, 2 * kq), F32)],
        compiler_params=_cparams(1),
        name="mlstm",
    )(z, z, z, gates, conv0, c0, n0, m0.reshape(bs, 1, nh), conv_w, conv_b.reshape(1, 2 * kq),
      gbias, g_mlstm.reshape(1, mw))
    return outs


def _outproj_kernel(rows, tm, d, hw, og_ref, hm_ref, w_ref, xp_ref, xs_ref, mod_ref, o_ref, wb_ref):
    j, i = pl.program_id(0), pl.program_id(1)
    is_prompt = i < rows.groups_p * CHUNK // tm

    @pl.when(i == 0)
    def _():
        wb_ref[...] = w_ref[...].astype(BF16)

    mix = (jnp.dot(og_ref[...], wb_ref[0:hw, :], preferred_element_type=F32)
           + jnp.dot(hm_ref[...], wb_ref[hw:, :], preferred_element_type=F32))
    for gi in range(tm // CHUNK):
        seq = rows.seq_of_group(i * (tm // CHUNK) + gi)
        gate = mod_ref[pl.ds(seq, 1), :]
        rs = slice(gi * CHUNK, (gi + 1) * CHUNK)
        x = jnp.where(is_prompt, xp_ref[rs, :], xs_ref[rs, :])
        o_ref[rs, :] = x + gate * mix[rs, :]


def _outproj(rows, og, hm, w_out, x_p, x_s, mod, tm=1024, tn=512):
    d = x_p.shape[1]
    r = rows.rows
    hw = og.shape[1]
    tm, tn = min(tm, x_s.shape[0]), min(tn, d)
    assert r % tm == 0 and d % tn == 0
    xp_spec, xs_spec = _split_row_specs(rows, tm, tn, col_major=True)
    gate_blk0 = 2 * d // tn
    return pl.pallas_call(
        functools.partial(_outproj_kernel, rows, tm, d, hw),
        out_shape=jax.ShapeDtypeStruct((r, d), F32),
        grid=(d // tn, r // tm),
        in_specs=[pl.BlockSpec((tm, hw), lambda j, i: (i, 0)),
                  pl.BlockSpec((tm, hm.shape[1]), lambda j, i: (i, 0)),
                  pl.BlockSpec((w_out.shape[0], tn), lambda j, i: (0, j)),
                  xp_spec, xs_spec,
                  pl.BlockSpec((mod.shape[0], tn), lambda j, i: (0, gate_blk0 + j))],
        out_specs=pl.BlockSpec((tm, tn), lambda j, i: (i, j)),
        scratch_shapes=[pltpu.VMEM((w_out.shape[0], tn), BF16)],
        compiler_params=_cparams(2),
        name="out_proj",
    )(og, hm, w_out, x_p, x_s, mod)


def _router_kernel(rows, tm, d, ne, x_ref, g_ref, mod_ref, wr_ref, br_ref, ut_ref,
                   h2_ref, e_ref, gate_ref, rank_ref, cnt_ref, h2s_ref):
    i = pl.program_id(0)
    half = d // 2

    @pl.when(i == 0)
    def _():
        cnt_ref[...] = jnp.zeros_like(cnt_ref)

    g = g_ref[...]
    for gi in range(tm // CHUNK):
        seq = rows.seq_of_group(i * (tm // CHUNK) + gi)
        m = mod_ref[pl.ds(seq, 1), :]
        shift, scale = m[:, 3 * d:4 * d], m[:, 4 * d:5 * d]
        x = x_ref[gi * CHUNK:(gi + 1) * CHUNK, :]
        ms = jnp.mean(x * x, axis=-1, keepdims=True)
        h2s_ref[gi * CHUNK:(gi + 1) * CHUNK, :] = (x * lax.rsqrt(ms + RMS_EPS) * g) * (1.0 + scale) + shift
    h2 = h2s_ref[...]
    lo = pltpu.bitcast(h2[:, :half].astype(BF16).astype(F32), jnp.uint32)
    hi = pltpu.bitcast(h2[:, half:].astype(BF16).astype(F32), jnp.uint32)
    h2_ref[...] = (hi & jnp.uint32(0xFFFF0000)) | (lo >> 16)

    logits = jnp.dot(h2, wr_ref[...], precision=lax.Precision.HIGHEST,
                     preferred_element_type=F32)
    lt = logits.T[0:ne, :] + br_ref[...]
    eidx = lax.broadcasted_iota(jnp.int32, (ne, tm), 0)
    cur = lt
    tops, sels, hots = [], [], []
    for _ in range(TOP_K):
        mx = jnp.max(cur, axis=0, keepdims=True)
        sel = jnp.min(jnp.where(cur == mx, eidx, ne), axis=0, keepdims=True)
        hot = eidx == sel
        tops.append(mx)
        sels.append(sel)
        hots.append(hot)
        cur = jnp.where(hot, -jnp.inf, cur)
    ex = [jnp.exp(t - tops[0]) for t in tops]
    tot = ex[0] + ex[1] + ex[2] + ex[3]
    oh = jnp.zeros((ne, tm), F32)
    for hot in hots:
        oh = oh + hot.astype(F32)
    prefix = jnp.dot(oh.astype(BF16), ut_ref[...], preferred_element_type=F32)
    base = cnt_ref[:, 0:1]
    pos = base + prefix
    for kk in range(TOP_K):
        e_ref[kk:kk + 1, :] = sels[kk]
        gate_ref[kk:kk + 1, :] = ex[kk] / tot
        rank_ref[kk:kk + 1, :] = jnp.sum(jnp.where(hots[kk], pos, 0.0), axis=0,
                                         keepdims=True).astype(jnp.int32)
    cnt_ref[...] = cnt_ref[...] + jnp.sum(oh, axis=1, keepdims=True)


def _router(rows, x1, g_ffn, mod, w_router, b_router, tm=256):
    r, d = x1.shape
    ne = w_router.shape[1]
    tm = min(tm, r)
    assert r % tm == 0 and ne % 8 == 0 and ne <= LANES
    wr_pad = jnp.zeros((d, LANES), F32).at[:, :ne].set(w_router.astype(F32))
    tt = np.arange(tm)
    ut = jnp.asarray((tt[:, None] < tt[None, :]).astype(np.float32), BF16)
    return pl.pallas_call(
        functools.partial(_router_kernel, rows, tm, d, ne),
        out_shape=(jax.ShapeDtypeStruct((r, d // 2), jnp.uint32),
                   jax.ShapeDtypeStruct((TOP_K, r), jnp.int32),
                   jax.ShapeDtypeStruct((TOP_K, r), F32),
                   jax.ShapeDtypeStruct((TOP_K, r), jnp.int32),
                   jax.ShapeDtypeStruct((ne, LANES), F32)),
        grid=(r // tm,),
        in_specs=[pl.BlockSpec((tm, d), lambda i: (i, 0)),
                  pl.BlockSpec((1, d), lambda i: (0, 0)),
                  pl.BlockSpec(mod.shape, lambda i: (0, 0)),
                  pl.BlockSpec((d, LANES), lambda i: (0, 0)),
                  pl.BlockSpec((ne, 1), lambda i: (0, 0)),
                  pl.BlockSpec((tm, tm), lambda i: (0, 0))],
        out_specs=(pl.BlockSpec((tm, d // 2), lambda i: (i, 0)),
                   pl.BlockSpec((TOP_K, tm), lambda i: (0, i)),
                   pl.BlockSpec((TOP_K, tm), lambda i: (0, i)),
                   pl.BlockSpec((TOP_K, tm), lambda i: (0, i)),
                   pl.BlockSpec((ne, LANES), lambda i: (0, 0))),
        scratch_shapes=[pltpu.VMEM((tm, d), F32)],
        compiler_params=_cparams(1),
        name="router",
    )(x1, g_ffn.reshape(1, d), mod, wr_pad, b_router.reshape(ne, 1).astype(F32), ut)


def _invert_kernel(tm, pstart_ref, e_ref, rank_ref, rt_ref):
    i = pl.program_id(0)

    @pl.when(i == 0)
    def _():
        def clear(r, carry):
            rt_ref[r] = 0
            return carry

        lax.fori_loop(0, rt_ref.shape[0], clear, 0, unroll=8)

    def body(t, carry):
        for kk in range(TOP_K):
            rt_ref[pstart_ref[e_ref[kk, t]] + rank_ref[kk, t]] = i * tm + t
        return carry

    lax.fori_loop(0, tm, body, 0, unroll=4)


def _invert(e_idx, rank, pstart, n_rows, tm=256):
    r = e_idx.shape[1]
    tm = min(tm, r)
    assert r % tm == 0
    return pl.pallas_call(
        functools.partial(_invert_kernel, tm),
        out_shape=jax.ShapeDtypeStruct((n_rows,), jnp.int32),
        grid_spec=pltpu.PrefetchScalarGridSpec(
            num_scalar_prefetch=1,
            grid=(r // tm,),
            in_specs=[pl.BlockSpec((TOP_K, tm), lambda i, *_: (0, i), memory_space=pltpu.SMEM),
                      pl.BlockSpec((TOP_K, tm), lambda i, *_: (0, i), memory_space=pltpu.SMEM)],
            out_specs=pl.BlockSpec(memory_space=pltpu.SMEM)),
        compiler_params=_cparams(1),
        name="moe_invert",
    )(pstart, e_idx, rank)


def _unpack_rows(xw):
    lo = pltpu.bitcast(xw << 16, F32).astype(BF16)
    hi = pltpu.bitcast(xw & jnp.uint32(0xFFFF0000), F32).astype(BF16)
    return lo, hi


def _gather_kernel(nused_ref, rt_cur_ref, rt_next_ref, h2_ref, xs_ref, buf_ref, sem):
    b = pl.program_id(0)
    n_used = nused_ref[0]
    half = buf_ref.shape[2]

    def row_copy(slot, src_row, dst_row):
        return pltpu.make_async_copy(h2_ref.at[pl.ds(src_row, 1), :],
                                     buf_ref.at[slot, pl.ds(dst_row, 1), :], sem.at[slot])

    def issue(rt_ref, slot):
        def pair(t2, carry):
            for p in range(2):
                t = 2 * t2 + p
                row_copy(slot, rt_ref[0, 0, t], t).start(priority=p)
            return carry

        lax.fori_loop(0, MOE_BLOCK // 2, pair, 0, unroll=4)

    @pl.when(jnp.logical_and(b == 0, n_used > 0))
    def _():
        issue(rt_cur_ref, 0)

    @pl.when(b + 1 < n_used)
    def _():
        issue(rt_next_ref, (b + 1) % 2)

    @pl.when(b < n_used)
    def _():
        slot = b % 2

        def drain(t, carry):
            row_copy(slot, 0, 0).wait()
            return carry

        lax.fori_loop(0, MOE_BLOCK, drain, 0, unroll=8)
        lo, hi = _unpack_rows(buf_ref[slot])
        xs_ref[:, 0:half] = lo
        xs_ref[:, half:] = hi

    @pl.when(b >= n_used)
    def _():
        xs_ref[...] = jnp.zeros_like(xs_ref)


def _gather_rows(h2p, row_tok, n_used):
    r, half = h2p.shape
    n_blocks = row_tok.shape[0] // MOE_BLOCK
    rt3 = row_tok.reshape(n_blocks, 1, MOE_BLOCK)
    return pl.pallas_call(
        _gather_kernel,
        out_shape=jax.ShapeDtypeStruct((n_blocks * MOE_BLOCK, 2 * half), BF16),
        grid_spec=pltpu.PrefetchScalarGridSpec(
            num_scalar_prefetch=1,
            grid=(n_blocks,),
            in_specs=[pl.BlockSpec((1, 1, MOE_BLOCK), lambda b, *_: (b, 0, 0), memory_space=pltpu.SMEM),
                      pl.BlockSpec((1, 1, MOE_BLOCK), lambda b, *_: (jnp.minimum(b + 1, n_blocks - 1), 0, 0),
                                   memory_space=pltpu.SMEM),
                      pl.BlockSpec(memory_space=pl.ANY)],
            out_specs=pl.BlockSpec((MOE_BLOCK, 2 * half), lambda b, *_: (b, 0)),
            scratch_shapes=[pltpu.VMEM((2, MOE_BLOCK, half), jnp.uint32),
                            pltpu.SemaphoreType.DMA((2,))]),
        compiler_params=_cparams(1),
        name="moe_gather",
    )(n_used.reshape(1), rt3, rt3, h2p)


MOE_CACHE_BLOCKS = 4


class _ExpertRows:
    def __init__(self, src_hbm, slots_ref, sems, b0, nb, first_tile):
        self.src, self.slots, self.sems = src_hbm, slots_ref, sems
        self.b0, self.nb, self.first_tile = b0, nb, first_tile

    def slot_of(self, i):
        return jnp.where(i < MOE_CACHE_BLOCKS, i, MOE_CACHE_BLOCKS + i % 2)

    def _needs_load(self, i):
        return jnp.logical_or(self.first_tile, i >= MOE_CACHE_BLOCKS)

    def _copy(self, i):
        row0 = pl.multiple_of((self.b0 + i) * MOE_BLOCK, MOE_BLOCK)
        return pltpu.make_async_copy(self.src.at[pl.ds(row0, MOE_BLOCK), :],
                                     self.slots.at[self.slot_of(i)], self.sems.at[i % 2])

    def request(self, i):
        @pl.when(jnp.logical_and(i < self.nb, self._needs_load(i)))
        def _():
            self._copy(i).start()

    def arrive(self, i):
        @pl.when(self._needs_load(i))
        def _():
            self._copy(i).wait()

    def request_first(self, e):
        @pl.when(e == 0)
        def _():
            self.request(0)

    def request_next_expert(self, e, ne, last_tile, bstart_ref, nblk_ref):
        @pl.when(jnp.logical_and(last_tile, e + 1 < ne))
        def _():
            @pl.when(nblk_ref[e + 1] > 0)
            def _():
                row0 = pl.multiple_of(bstart_ref[e + 1] * MOE_BLOCK, MOE_BLOCK)
                pltpu.make_async_copy(self.src.at[pl.ds(row0, MOE_BLOCK), :],
                                      self.slots.at[0], self.sems.at[0]).start()


def _block_window(dst_hbm, blk, col0, width):
    row0 = pl.multiple_of(blk * MOE_BLOCK, MOE_BLOCK)
    return dst_hbm.at[pl.ds(row0, MOE_BLOCK), pl.ds(pl.multiple_of(col0, width), width)]


def _zero_tail_blocks(stage_ref, dst_hbm, sem, first_blk, col0, width):
    n_blocks = dst_hbm.shape[0] // MOE_BLOCK
    stage_ref[0] = jnp.zeros(stage_ref.shape[1:], stage_ref.dtype)

    def cp(blk):
        return pltpu.make_async_copy(stage_ref.at[0], _block_window(dst_hbm, blk, col0, width), sem.at[0])

    def start(blk, carry):
        cp(blk).start()
        return carry

    def wait(blk, carry):
        cp(blk).wait()
        return carry

    lax.fori_loop(first_blk, n_blocks, start, 0)
    lax.fori_loop(first_blk, n_blocks, wait, 0)


class _OutRing:
    def __init__(self, stage_ref, sems, cnt_ref, dst_hbm, width):
        self.stage, self.sems, self.cnt, self.dst, self.width = stage_ref, sems, cnt_ref, dst_hbm, width

    def init(self):
        self.cnt[0] = 0
        self.cnt[1] = 0

    def _await_next(self, limit):
        w = self.cnt[1]

        @pl.when(w < limit)
        def _():
            pltpu.make_async_copy(self.stage.at[w % 2], _block_window(self.dst, 0, 0, self.width),
                                  self.sems.at[w % 2]).wait()
            self.cnt[1] = w + 1

    def reserve(self):
        self._await_next(self.cnt[0] - 1)

    def push(self, value, blk, col0):
        c = self.cnt[0]
        self.stage[c % 2] = value
        pltpu.make_async_copy(self.stage.at[c % 2], _block_window(self.dst, blk, col0, self.width),
                              self.sems.at[c % 2]).start()
        self.cnt[0] = c + 1

    def drain(self):
        for _ in range(2):
            self._await_next(self.cnt[0])


MOE_CAST_CHUNKS = 4


def _cast_chunk(w_ref, wbf_ref, c):
    kc = wbf_ref.shape[0] // MOE_CAST_CHUNKS
    wbf_ref[c * kc:(c + 1) * kc, :] = w_ref[0, c * kc:(c + 1) * kc, :].astype(BF16)


def _first_block_dot(x_blk_ref, w_ref, wbf_ref):
    kc = wbf_ref.shape[0] // MOE_CAST_CHUNKS
    acc = None
    for c in range(MOE_CAST_CHUNKS):
        if c + 1 < MOE_CAST_CHUNKS:
            _cast_chunk(w_ref, wbf_ref, c + 1)
        part = jnp.dot(x_blk_ref[:, c * kc:(c + 1) * kc], wbf_ref[c * kc:(c + 1) * kc, :],
                       preferred_element_type=F32)
        acc = part if acc is None else acc + part
    return acc


def _moe_up_kernel(tf, nj, ne, bstart_ref, nblk_ref, xs_hbm, w_ref, b_ref, act_hbm,
                   wbf_ref, xb_ref, gt_ref, ostage_ref, cnt_ref, xsem, osem):
    g = pl.program_id(0)
    e, j = g // nj, g % nj
    nb, b0 = nblk_ref[e], bstart_ref[e]
    nslab = MOE_BLOCK // LANES
    xrows = _ExpertRows(xs_hbm, xb_ref, xsem, b0, nb, j == 0)
    ring = _OutRing(ostage_ref, osem, cnt_ref, act_hbm, tf)

    @pl.when(g == 0)
    def _():
        ring.init()

    def up_dot(i):
        return jnp.dot(xb_ref[xrows.slot_of(i)], wbf_ref[...], preferred_element_type=F32) + b_ref[0]

    def swiglu(gu):
        gut = gu.T
        for sl in range(nslab):
            gt_ref[sl] = gut[:, sl * LANES:(sl + 1) * LANES]
        parts = []
        for sl in range(nslab):
            gate = jnp.minimum(gt_ref[sl, pl.ds(0, tf, stride=2), :], SWIGLU_LIMIT)
            up = jnp.clip(gt_ref[sl, pl.ds(1, tf, stride=2), :], -SWIGLU_LIMIT, SWIGLU_LIMIT)
            parts.append((up + 1.0) * gate * _sigmoid(SWIGLU_ALPHA * gate))
        return jnp.concatenate(parts, axis=1).T.astype(BF16)

    @pl.when(nb > 0)
    def _():
        xrows.request_first(e)
        _cast_chunk(w_ref, wbf_ref, 0)
        xrows.arrive(0)
        xrows.request(1)
        gu0 = _first_block_dot(xb_ref.at[0], w_ref, wbf_ref) + b_ref[0]

        def body(i, gu_prev):
            xrows.arrive(i)
            xrows.request(i + 1)
            ring.reserve()
            gu = up_dot(i)
            ring.push(swiglu(gu_prev), b0 + i - 1, j * tf)
            return gu

        gu_last = lax.fori_loop(1, nb, body, gu0)
        ring.reserve()
        ring.push(swiglu(gu_last), b0 + nb - 1, j * tf)

    xrows.request_next_expert(e, ne, j == nj - 1, bstart_ref, nblk_ref)

    @pl.when(e == ne - 1)
    def _():
        ring.drain()
        _zero_tail_blocks(ostage_ref, act_hbm, osem, b0 + nb, j * tf, tf)


def _moe_up(xs, w_gu, b_gu, bstart, nblk, tf=512):
    n_rows, d = xs.shape
    ne, _, f2 = w_gu.shape
    f = f2 // 2
    tf = min(tf, f)
    assert f % tf == 0 and w_gu.shape[1] == d
    nj = f // tf
    return pl.pallas_call(
        functools.partial(_moe_up_kernel, tf, nj, ne),
        out_shape=jax.ShapeDtypeStruct((n_rows, f), BF16),
        grid_spec=pltpu.PrefetchScalarGridSpec(
            num_scalar_prefetch=2,
            grid=(ne * nj,),
            in_specs=[pl.BlockSpec(memory_space=pl.ANY),
                      pl.BlockSpec((1, d, 2 * tf), lambda g, *_: (g // nj, 0, g % nj)),
                      pl.BlockSpec((1, 1, 2 * tf), lambda g, *_: (g // nj, 0, g % nj))],
            out_specs=pl.BlockSpec(memory_space=pl.ANY),
            scratch_shapes=[pltpu.VMEM((d, 2 * tf), BF16),
                            pltpu.VMEM((MOE_CACHE_BLOCKS + 2, MOE_BLOCK, d), BF16),
                            pltpu.VMEM((MOE_BLOCK // LANES, 2 * tf, LANES), F32),
                            pltpu.VMEM((2, MOE_BLOCK, tf), BF16),
                            pltpu.SMEM((2,), jnp.int32),
                            pltpu.SemaphoreType.DMA((2,)),
                            pltpu.SemaphoreType.DMA((2,))]),
        compiler_params=_cparams(1),
        name="moe_up",
    )(bstart, nblk, xs, w_gu, b_gu.reshape(ne, 1, f2))


def _moe_down_kernel(tn, nj, ne, bstart_ref, nblk_ref, act_hbm, w_ref, b_ref, yb_hbm,
                     wbf_ref, ab_ref, ostage_ref, cnt_ref, asem, osem):
    g = pl.program_id(0)
    e, j = g // nj, g % nj
    nb, b0 = nblk_ref[e], bstart_ref[e]
    arows = _ExpertRows(act_hbm, ab_ref, asem, b0, nb, j == 0)
    ring = _OutRing(ostage_ref, osem, cnt_ref, yb_hbm, tn)

    @pl.when(g == 0)
    def _():
        ring.init()

    @pl.when(nb > 0)
    def _():
        arows.request_first(e)
        _cast_chunk(w_ref, wbf_ref, 0)
        arows.arrive(0)
        arows.request(1)
        ring.reserve()
        ring.push(_first_block_dot(ab_ref.at[0], w_ref, wbf_ref) + b_ref[0], b0, j * tn)

        def body(i, carry):
            arows.arrive(i)
            arows.request(i + 1)
            ring.reserve()
            y = jnp.dot(ab_ref[arows.slot_of(i)], wbf_ref[...], preferred_element_type=F32) + b_ref[0]
            ring.push(y, b0 + i, j * tn)
            return carry

        lax.fori_loop(1, nb, body, 0)

    arows.request_next_expert(e, ne, j == nj - 1, bstart_ref, nblk_ref)

    @pl.when(e == ne - 1)
    def _():
        ring.drain()
        _zero_tail_blocks(ostage_ref, yb_hbm, osem, b0 + nb, j * tn, tn)


def _moe_down(act, w_d, b_d, bstart, nblk, tn=1024):
    n_rows, f = act.shape
    ne, _, d = w_d.shape
    tn = min(tn, d)
    assert d % tn == 0
    nj = d // tn
    return pl.pallas_call(
        functools.partial(_moe_down_kernel, tn, nj, ne),
        out_shape=jax.ShapeDtypeStruct((n_rows, d), F32),
        grid_spec=pltpu.PrefetchScalarGridSpec(
            num_scalar_prefetch=2,
            grid=(ne * nj,),
            in_specs=[pl.BlockSpec(memory_space=pl.ANY),
                      pl.BlockSpec((1, f, tn), lambda g, *_: (g // nj, 0, g % nj)),
                      pl.BlockSpec((1, 1, tn), lambda g, *_: (g // nj, 0, g % nj))],
            out_specs=pl.BlockSpec(memory_space=pl.ANY),
            scratch_shapes=[pltpu.VMEM((f, tn), BF16),
                            pltpu.VMEM((MOE_CACHE_BLOCKS + 2, MOE_BLOCK, f), BF16),
                            pltpu.VMEM((2, MOE_BLOCK, tn), F32),
                            pltpu.SMEM((2,), jnp.int32),
                            pltpu.SemaphoreType.DMA((2,)),
                            pltpu.SemaphoreType.DMA((2,))]),
        compiler_params=_cparams(1),
        name="moe_down",
    )(bstart, nblk, act, w_d, b_d.reshape(ne, 1, d))


def _combine_kernel(rows, tm, d, n_tiles, pstart_ref, e_ref, rank_ref, en_ref, rankn_ref, gt_ref,
                    x_ref, mod_ref, gf_ref, yb_ref, yp_ref, ys_ref, buf_ref, sem):
    i = pl.program_id(0)
    is_prompt = i < rows.groups_p * CHUNK // tm
    slot = i % 2

    def row_copy(s, dst_k, dst_t, src_row):
        return pltpu.make_async_copy(yb_ref.at[pl.ds(src_row, 1), :],
                                     buf_ref.at[s, dst_k, pl.ds(dst_t, 1), :], sem.at[s])

    def issue(eref, rref, s):
        def one(t, carry):
            for kk in range(TOP_K):
                src = pstart_ref[eref[kk, t]] + rref[kk, t]
                row_copy(s, kk, t, src).start(priority=kk % 2)
            return carry

        lax.fori_loop(0, tm, one, 0, unroll=2)

    @pl.when(i == 0)
    def _():
        issue(e_ref, rank_ref, 0)

    @pl.when(i + 1 < n_tiles)
    def _():
        issue(en_ref, rankn_ref, (i + 1) % 2)

    def drain(t, carry):
        for kk in range(TOP_K):
            row_copy(slot, 0, 0, 0).wait()
        return carry

    lax.fori_loop(0, tm, drain, 0, unroll=2)

    gf = gf_ref[...]
    for gi in range(tm // CHUNK):
        rs = slice(gi * CHUNK, (gi + 1) * CHUNK)
        seq = rows.seq_of_group(i * (tm // CHUNK) + gi)
        gate2 = mod_ref[pl.ds(seq, 1), :]
        ff = jnp.zeros((CHUNK, d), F32)
        for kk in range(TOP_K):
            ff = ff + buf_ref[slot, kk, rs, :] * gt_ref[rs, kk:kk + 1]
        x2 = x_ref[rs, :] + gate2 * ff
        ms = jnp.mean(x2 * x2, axis=-1, keepdims=True)
        y = x2 * lax.rsqrt(ms + RMS_EPS) * gf

        @pl.when(is_prompt)
        def _():
            yp_ref[rs, :] = y

        @pl.when(jnp.logical_not(is_prompt))
        def _():
            ys_ref[rs, :] = y


def _combine(rows, x1, mod, g_final, yb, e_idx, rank, gates_t, pstart, tm=128):
    r, d = x1.shape
    tm = min(tm, r)
    assert r % tm == 0
    gate2_blk = 5
    yp_spec, ys_spec = _split_row_specs(rows, tm, d)
    n_tiles = r // tm

    def nxt(i, *_):
        return (0, jnp.minimum(i + 1, n_tiles - 1))

    return pl.pallas_call(
        functools.partial(_combine_kernel, rows, tm, d, n_tiles),
        out_shape=(jax.ShapeDtypeStruct((rows.groups_p * CHUNK, d), F32),
                   jax.ShapeDtypeStruct((r - rows.groups_p * CHUNK, d), F32)),
        grid_spec=pltpu.PrefetchScalarGridSpec(
            num_scalar_prefetch=1,
            grid=(n_tiles,),
            in_specs=[pl.BlockSpec((TOP_K, tm), lambda i, *_: (0, i), memory_space=pltpu.SMEM),
                      pl.BlockSpec((TOP_K, tm), lambda i, *_: (0, i), memory_space=pltpu.SMEM),
                      pl.BlockSpec((TOP_K, tm), nxt, memory_space=pltpu.SMEM),
                      pl.BlockSpec((TOP_K, tm), nxt, memory_space=pltpu.SMEM),
                      pl.BlockSpec((tm, TOP_K), lambda i, *_: (i, 0)),
                      pl.BlockSpec((tm, d), lambda i, *_: (i, 0)),
                      pl.BlockSpec((mod.shape[0], d), lambda i, *_: (0, gate2_blk)),
                      pl.BlockSpec((1, d), lambda i, *_: (0, 0)),
                      pl.BlockSpec(memory_space=pl.ANY)],
            out_specs=(yp_spec, ys_spec),
            scratch_shapes=[pltpu.VMEM((2, TOP_K, tm, d), F32),
                            pltpu.SemaphoreType.DMA((2,))]),
        compiler_params=_cparams(1),
        name="moe_combine",
    )(pstart, e_idx, rank, e_idx, rank, gates_t, x1, mod, g_final.reshape(1, d), yb)


def kernel(x_prompt, x_sample, c_prompt, c_sample, state_hgrn_S, state_conv, state_mlstm_C,
           state_mlstm_n, state_mlstm_m, w_ada, b_ada, g_mix, g_ffn, w_in, lb_logits, conv_w,
           conv_b, b_igate, b_fgate, g_hgrn, g_mlstm, w_out, w_router, b_router, w_gate_up,
           b_gate_up, w_down, b_down, g_final):
    depth = w_ada.shape[0]
    assert depth == 1, "single-layer trunk"
    bp, tp, d = x_prompt.shape
    bs, ts, _ = x_sample.shape
    rows = _Rows(bp, tp, bs, ts)
    _, _, nh_h, dk, dv = state_hgrn_S.shape
    _, _, nh_m, dqk, dvm = state_mlstm_C.shape
    hw = nh_h * dk
    mw = nh_m * dvm
    n_main = 4 * hw + 3 * mw
    assert w_in.shape[2] == n_main + 2 * nh_m and 2 * nh_m <= GATE_LANES
    assert mw == hw, "column sections of the input projection are addressed in hw-wide blocks"
    ne = w_router.shape[2]

    lb = jax.nn.softmax(lb_logits.astype(F32), axis=0)[0]

    x_p = x_prompt.reshape(bp * tp, d)
    x_s = x_sample.reshape(bs * ts, d)
    n_c = bp + bs
    n_c_pad = -(-n_c // 8) * 8
    c_pad = jnp.zeros((n_c_pad, d), F32).at[:n_c].set(jnp.concatenate([c_prompt, c_sample], axis=0))
    mod = _ada(c_pad, w_ada[0], b_ada[0])

    w_in_t = jnp.swapaxes(w_in, 1, 2)
    h_all, gates = _prep(rows, x_p, x_s, g_mix[0], mod, w_in_t, n_main)
    z = _inproj(h_all, w_in_t, n_main)

    og, st = _hgrn(rows, z, state_hgrn_S[0], lb, g_hgrn[0])
    hm, conv_new, c_new, n_new, m_new = _mlstm(
        rows, z, 4, gates, state_conv[0], state_mlstm_C[0], state_mlstm_n[0], state_mlstm_m[0],
        conv_w[0], conv_b[0], b_igate[0], b_fgate[0], g_mlstm[0])

    x1 = _outproj(rows, og, hm, w_out[0], x_p, x_s, mod)

    h2p, e_idx, gate_k, rank, cnt = _router(rows, x1, g_ffn[0], mod, w_router[0], b_router[0])

    counts = cnt[:, 0].astype(jnp.int32)
    nblk_e = (counts + MOE_BLOCK - 1) // MOE_BLOCK
    padded = nblk_e * MOE_BLOCK
    pad_end = jnp.cumsum(padded)
    pstart = jnp.concatenate([jnp.zeros((1,), jnp.int32), pad_end]).astype(jnp.int32)
    n_blocks_max = -(-(rows.rows * TOP_K) // MOE_BLOCK) + ne
    n_rows = n_blocks_max * MOE_BLOCK
    blk_start_e = (pstart[:ne] // MOE_BLOCK).astype(jnp.int32)
    n_used = (pstart[ne] // MOE_BLOCK).astype(jnp.int32)

    row_tok = _invert(e_idx, rank, pstart, n_rows)
    xs = _gather_rows(h2p, row_tok, n_used)
    act = _moe_up(xs, w_gate_up[0], b_gate_up[0], blk_start_e, nblk_e.astype(jnp.int32))
    yb = _moe_down(act, w_down[0], b_down[0], blk_start_e, nblk_e.astype(jnp.int32))

    y_p, y_s = _combine(rows, x1, mod, g_final, yb, e_idx, rank, gate_k.T, pstart)

    y_prompt = y_p.reshape(bp, tp, d)
    y_sample = y_s.reshape(bs, ts, d)
    s_all = jnp.swapaxes(st, 2, 3)
    m_all = m_new.reshape(rows.nseq, nh_m)

    def split(a):
        return a[:bp][None], a[bp:][None]

    p_s, s_s = split(s_all)
    p_conv, s_conv = split(conv_new)
    p_c, s_c = split(c_new)
    p_n, s_n = split(n_new)
    p_m, s_m = split(m_all)
    return (y_prompt, y_sample, p_s, p_conv, p_c, p_n, p_m, s_s, s_conv, s_c, s_n, s_m)
```

```python
import functools

import numpy as np
import jax
import jax.numpy as jnp
from jax import lax
from jax.experimental import pallas as pl
from jax.experimental.pallas import tpu as pltpu

F32 = jnp.float32
BF16 = jnp.bfloat16

CHUNK = 64
TOP_K = 4
CONV_K = 4
RMS_EPS = 1e-6
SWIGLU_LIMIT = 7.0
SWIGLU_ALPHA = 1.702
MOE_BLOCK = 256
LANES = 128
GATE_LANES = 128
VMEM_LIMIT = 60 * 1024 * 1024
HGRN_LEVELS = (32, 16, 8, 4, 2, 1)


def _cparams(n_axes, vmem=VMEM_LIMIT):
    return pltpu.CompilerParams(dimension_semantics=("arbitrary",) * n_axes,
                                vmem_limit_bytes=vmem)


def _sigmoid(x):
    return 1.0 / (1.0 + jnp.exp(-x))


def _silu(x):
    return x * _sigmoid(x)


class _Rows:
    def __init__(self, bp, tp, bs, ts):
        assert tp % CHUNK == 0 and ts % CHUNK == 0
        self.bp, self.tp, self.bs, self.ts = bp, tp, bs, ts
        self.nblk_p, self.nblk_s = tp // CHUNK, ts // CHUNK
        self.groups_p = bp * self.nblk_p
        self.groups = self.groups_p + bs * self.nblk_s
        self.rows = self.groups * CHUNK
        self.nseq = bp + bs

    def seq_of_group(self, g):
        return jnp.where(g < self.groups_p, g // self.nblk_p,
                         self.bp + (g - self.groups_p) // self.nblk_s)

    def blk_in_seq(self, g):
        return jnp.where(g < self.groups_p, g % self.nblk_p, (g - self.groups_p) % self.nblk_s)

    def nblk_of_group(self, g):
        return jnp.where(g < self.groups_p, self.nblk_p, self.nblk_s)


def _ada_kernel(c_ref, w_ref, b_ref, o_ref):
    c = c_ref[...]
    s = _silu(c).astype(BF16)
    o_ref[...] = jnp.dot(s, w_ref[...].astype(BF16), preferred_element_type=F32) + b_ref[...]


def _ada(c_pad, w_ada, b_ada, tn=512):
    m, d = c_pad.shape
    n = w_ada.shape[1]
    tn = min(tn, n)
    assert n % tn == 0
    return pl.pallas_call(
        _ada_kernel,
        out_shape=jax.ShapeDtypeStruct((m, n), F32),
        grid=(n // tn,),
        in_specs=[pl.BlockSpec((m, d), lambda j: (0, 0)),
                  pl.BlockSpec((d, tn), lambda j: (0, j)),
                  pl.BlockSpec((1, tn), lambda j: (0, j))],
        out_specs=pl.BlockSpec((m, tn), lambda j: (0, j)),
        compiler_params=_cparams(1),
        name="ada_mod",
    )(c_pad, w_ada, b_ada.reshape(1, n))


def _prep_kernel(rows, tm, d, n_gate, xp_ref, xs_ref, g_ref, mod_ref, wtail_ref,
                 h_ref, gates_ref, wg_ref, hf_ref):
    i = pl.program_id(0)
    is_prompt = i < rows.groups_p * CHUNK // tm

    @pl.when(i == 0)
    def _():
        lanes = wg_ref.shape[1]
        padded = jnp.concatenate([wtail_ref[0], jnp.zeros((lanes - n_gate, d), F32)], axis=0)
        wg_ref[...] = padded.T

    g = g_ref[...]
    for gi in range(tm // CHUNK):
        seq = rows.seq_of_group(i * (tm // CHUNK) + gi)
        m = mod_ref[pl.ds(seq, 1), :]
        shift, scale = m[:, 0:d], m[:, d:2 * d]
        rs = slice(gi * CHUNK, (gi + 1) * CHUNK)
        x = jnp.where(is_prompt, xp_ref[rs, :], xs_ref[rs, :])
        ms = jnp.mean(x * x, axis=-1, keepdims=True)
        y = x * lax.rsqrt(ms + RMS_EPS) * g
        h = y * (1.0 + scale) + shift
        h_ref[rs, :] = h.astype(BF16)
        hf_ref[rs, :] = h
    gates_ref[...] = jnp.dot(hf_ref[...], wg_ref[...], precision=lax.Precision.HIGHEST,
                             preferred_element_type=F32)


def _split_row_specs(rows, tm, tn, col_major=False):
    n_pt = rows.groups_p * CHUNK // tm
    n_st = rows.rows // tm - n_pt
    assert n_pt * tm == rows.groups_p * CHUNK and n_st >= 1

    def ij(args):
        return (args[1], args[0]) if col_major else (args[0], 0)

    def p_map(*args):
        i, j = ij(args)
        return (jnp.minimum(i, n_pt - 1), j)

    def s_map(*args):
        i, j = ij(args)
        return (jnp.maximum(i - n_pt, 0), j)

    return pl.BlockSpec((tm, tn), p_map), pl.BlockSpec((tm, tn), s_map)


def _prep(rows, x_p, x_s, g_mix, mod, w_in_t, n_main, tm=256):
    d = x_p.shape[1]
    r = rows.rows
    tm = min(tm, x_s.shape[0])
    n_gate = w_in_t.shape[1] - n_main
    assert r % tm == 0 and n_gate % 8 == 0 and n_main % n_gate == 0 and n_gate <= GATE_LANES
    xp_spec, xs_spec = _split_row_specs(rows, tm, d)
    return pl.pallas_call(
        functools.partial(_prep_kernel, rows, tm, d, n_gate),
        out_shape=(jax.ShapeDtypeStruct((r, d), BF16),
                   jax.ShapeDtypeStruct((r, GATE_LANES), F32)),
        grid=(r // tm,),
        in_specs=[xp_spec, xs_spec,
                  pl.BlockSpec((1, d), lambda i: (0, 0)),
                  pl.BlockSpec(mod.shape, lambda i: (0, 0)),
                  pl.BlockSpec((1, n_gate, d), lambda i: (0, n_main // n_gate, 0))],
        out_specs=(pl.BlockSpec((tm, d), lambda i: (i, 0)),
                   pl.BlockSpec((tm, GATE_LANES), lambda i: (i, 0))),
        scratch_shapes=[pltpu.VMEM((d, GATE_LANES), F32),
                        pltpu.VMEM((tm, d), F32)],
        compiler_params=_cparams(1),
        name="prep_norm_mod",
    )(x_p, x_s, g_mix.reshape(1, d), mod, w_in_t)


def _inproj_kernel(h_ref, w_ref, z_ref, wb_ref):
    @pl.when(pl.program_id(1) == 0)
    def _():
        wb_ref[...] = w_ref[0].astype(BF16)

    z_ref[...] = lax.dot_general(h_ref[...], wb_ref[...], (((1,), (1,)), ((), ())),
                                 preferred_element_type=F32)


def _inproj(h_all, w_in_t, n_main, tm=1024, tn=512):
    r, d = h_all.shape
    tm, tn = min(tm, r), min(tn, n_main)
    assert r % tm == 0 and n_main % tn == 0
    return pl.pallas_call(
        _inproj_kernel,
        out_shape=jax.ShapeDtypeStruct((r, n_main), F32),
        grid=(n_main // tn, r // tm),
        in_specs=[pl.BlockSpec((tm, d), lambda j, i: (i, 0)),
                  pl.BlockSpec((1, tn, d), lambda j, i: (0, j, 0))],
        out_specs=pl.BlockSpec((tm, tn), lambda j, i: (i, j)),
        scratch_shapes=[pltpu.VMEM((tn, d), BF16)],
        compiler_params=_cparams(2),
        name="in_proj",
    )(h_all, w_in_t)


def _hgrn_consts():
    L = CHUNK
    t = np.arange(L)[:, None]
    s = np.arange(L)[None, :]
    mats = [(s <= t)]
    masks = []
    for m in HGRN_LEVELS:
        start = (t // (2 * m)) * (2 * m)
        mats.append(s <= start + m - 1)
        masks.append((t // (2 * m) == s // (2 * m)) & (t % (2 * m) >= m) & (s % (2 * m) < m))
    mstack = np.concatenate(mats, axis=0).astype(np.float32)
    masks = np.stack(masks, axis=0).astype(np.float32)
    return jnp.asarray(mstack, BF16), jnp.asarray(masks, F32)


def _split3(x):
    x1 = x.astype(BF16)
    r1 = x - x1.astype(F32)
    x2 = r1.astype(BF16)
    x3 = (r1 - x2.astype(F32)).astype(BF16)
    return x1, x2, x3


def _hgrn_kernel(rows, nh, dk, dv, hq_ref, hf_ref, hi_ref, hg_ref, s0_ref, lb_ref, gn_ref,
                 mstack_ref, masks_ref, og_ref, st_ref, r_ref, o_ref):
    L = CHUNK
    g = pl.program_id(0)
    blk = rows.blk_in_seq(g)
    is_prompt = g < rows.groups_p

    @pl.when(jnp.logical_and(blk == 0, is_prompt))
    def _():
        st_ref[...] = jnp.zeros_like(st_ref)

    @pl.when(jnp.logical_and(blk == 0, jnp.logical_not(is_prompt)))
    def _():
        for h in range(nh):
            st_ref[0, h] = s0_ref[0, h].T

    def sig(x):
        return 0.5 * jnp.tanh(0.5 * x) + 0.5

    lb = lb_ref[...]
    f = lb + (1.0 - lb) * sig(hf_ref[...])
    lf = jnp.log(f)
    mstack = mstack_ref[...]
    p1, p2, p3 = _split3(lf)
    r_ref[...] = (jnp.dot(mstack, p1, preferred_element_type=F32)
                  + jnp.dot(mstack, p2, preferred_element_type=F32)
                  + jnp.dot(mstack, p3, preferred_element_type=F32))

    for h in range(nh):
        cs = slice(h * dk, (h + 1) * dk)
        vs = slice(h * dv, (h + 1) * dv)
        hq = hq_ref[:, cs]
        q = hq * sig(hq) * (dk ** -0.5)
        k = 1.0 - f[:, cs]
        v = hi_ref[:, vs]
        vb = v.astype(BF16)
        b = r_ref[0:L, cs]
        st = st_ref[0, h]
        qe = (q * jnp.exp(b)).astype(BF16)
        o = lax.dot_general(qe, st.astype(BF16), (((1,), (1,)), ((), ())),
                            preferred_element_type=F32)
        a = jnp.zeros((L, L), F32)
        for li in range(len(HGRN_LEVELS)):
            rl = r_ref[(li + 1) * L:(li + 2) * L, cs]
            e = jnp.exp(-jnp.abs(b - rl))
            al = lax.dot_general((q * e).astype(BF16), (k * e).astype(BF16),
                                 (((1,), (1,)), ((), ())), preferred_element_type=F32)
            a = a + al * masks_ref[li]
        diag = jnp.sum(q * k, axis=-1, keepdims=True)
        o = o + jnp.dot(a.astype(BF16), vb, preferred_element_type=F32) + diag * v
        o_ref[:, vs] = o
        b_last = b[L - 1:L, :]
        kd = (k * jnp.exp(b_last - b)).astype(BF16)
        st_ref[0, h] = st * jnp.exp(b_last) + lax.dot_general(
            vb, kd, (((0,), (0,)), ((), ())), preferred_element_type=F32)

    o = o_ref[...]
    ms = jnp.mean(o * o, axis=-1, keepdims=True)
    hg = hg_ref[...]
    og = o * lax.rsqrt(ms + RMS_EPS) * gn_ref[...] * (hg * sig(hg))
    og_ref[...] = og.astype(BF16)


def _hgrn(rows, z, s0, lb, g_hgrn):
    bs, nh, dk, dv = s0.shape
    hw = nh * dk
    assert nh * dv == hw and hw % LANES == 0
    mstack, masks = _hgrn_consts()
    n_lv = len(HGRN_LEVELS)
    L = CHUNK

    def zspec(sec):
        return pl.BlockSpec((L, hw), lambda g, sec=sec: (g, sec))

    og, st = pl.pallas_call(
        functools.partial(_hgrn_kernel, rows, nh, dk, dv),
        out_shape=(jax.ShapeDtypeStruct((rows.rows, hw), BF16),
                   jax.ShapeDtypeStruct((rows.nseq, nh, dv, dk), F32)),
        grid=(rows.groups,),
        in_specs=[zspec(0), zspec(1), zspec(2), zspec(3),
                  pl.BlockSpec((1, nh, dk, dv),
                               lambda g: (jnp.maximum(rows.seq_of_group(g) - rows.bp, 0), 0, 0, 0)),
                  pl.BlockSpec((1, hw), lambda g: (0, 0)),
                  pl.BlockSpec((1, hw), lambda g: (0, 0)),
                  pl.BlockSpec(mstack.shape, lambda g: (0, 0)),
                  pl.BlockSpec(masks.shape, lambda g: (0, 0, 0))],
        out_specs=(pl.BlockSpec((L, hw), lambda g: (g, 0)),
                   pl.BlockSpec((1, nh, dv, dk), lambda g: (rows.seq_of_group(g), 0, 0, 0))),
        scratch_shapes=[pltpu.VMEM(((n_lv + 1) * L, hw), F32),
                        pltpu.VMEM((L, hw), F32)],
        compiler_params=_cparams(1),
        name="hgrn2",
    )(z, z, z, z, s0, lb.reshape(1, hw), g_hgrn.reshape(1, hw), mstack, masks)
    return og, st


def _log_sigmoid(x):
    return jnp.minimum(x, 0.0) - jnp.log(1.0 + jnp.exp(-jnp.abs(x)))


def _mlstm_kernel(rows, nh, dqk, dv, mqk_ref, mv_ref, mo_ref, gates_ref, conv0_ref, c0_ref,
                  n0_ref, m0_ref, cw_ref, cb_ref, gbias_ref, gn_ref,
                  hm_ref, conv_ref, c_ref, n_ref, m_ref, ubuf_ref):
    L = CHUNK
    kq = nh * dqk
    pad = 8
    g = pl.program_id(0)
    blk = rows.blk_in_seq(g)
    is_prompt = g < rows.groups_p

    @pl.when(jnp.logical_and(blk == 0, is_prompt))
    def _():
        ubuf_ref[0:pad, :] = jnp.zeros((pad, 2 * kq), F32)
        c_ref[...] = jnp.zeros_like(c_ref)
        n_ref[...] = jnp.zeros_like(n_ref)
        m_ref[...] = jnp.zeros_like(m_ref)

    @pl.when(jnp.logical_and(blk == 0, jnp.logical_not(is_prompt)))
    def _():
        ubuf_ref[0:pad, :] = jnp.zeros((pad, 2 * kq), F32)
        ubuf_ref[pad - (CONV_K - 1):pad, :] = conv0_ref[0]
        c_ref[...] = c0_ref[...]
        n_ref[...] = n0_ref[...]
        m_ref[...] = m0_ref[...]

    ubuf_ref[pad:pad + L, :] = mqk_ref[...]
    acc = cb_ref[...] + jnp.zeros((L, 2 * kq), F32)
    for j in range(CONV_K):
        off = pad - (CONV_K - 1) + j
        acc = acc + ubuf_ref[off:off + L, :] * cw_ref[j:j + 1, :]
    new_tail = ubuf_ref[pad + L - (CONV_K - 1):pad + L, :]
    conv_ref[0] = new_tail
    ubuf_ref[pad - (CONV_K - 1):pad, :] = new_tail
    qk = _silu(acc)

    gt = gates_ref[...] + gbias_ref[...]
    lane = lax.broadcasted_iota(jnp.int32, gt.shape, 1)
    pg = jnp.where(lane < nh, gt, _log_sigmoid(gt))
    pgt = pg.T
    ti = lax.broadcasted_iota(jnp.int32, (L, L), 0)
    si = lax.broadcasted_iota(jnp.int32, (L, L), 1)
    tri = si <= ti

    for h in range(nh):
        q = qk[:, h * dqk:(h + 1) * dqk]
        k = qk[:, kq + h * dqk:kq + (h + 1) * dqk] * (dqk ** -0.5)
        v = mv_ref[:, h * dv:(h + 1) * dv]
        qb, kb, vb = q.astype(BF16), k.astype(BF16), v.astype(BF16)
        ig_c, lf_c = pg[:, h:h + 1], pg[:, nh + h:nh + h + 1]
        ig_r, lf_r = pgt[h:h + 1, :], pgt[nh + h:nh + h + 1, :]
        b_c = jnp.sum(jnp.where(tri, lf_r, 0.0), axis=1, keepdims=True)
        b_r = jnp.sum(jnp.where(ti <= si, lf_c, 0.0), axis=0, keepdims=True)
        m_prev = m_ref[0, :, h:h + 1]
        log_d = jnp.where(tri, b_c - b_r + ig_r, -jnp.inf)
        log_inter = b_c + m_prev
        m_t = jnp.maximum(jnp.max(log_d, axis=1, keepdims=True), log_inter)
        dm = jnp.exp(log_d - m_t)
        s_mat = lax.dot_general(qb, kb, (((1,), (1,)), ((), ())), preferred_element_type=F32) * dm
        w_inter = jnp.exp(log_inter - m_t)
        c_h = c_ref[0, h]
        n_h = n_ref[0, h:h + 1, :]
        num = (jnp.dot(s_mat.astype(BF16), vb, preferred_element_type=F32)
               + w_inter * jnp.dot(qb, c_h.astype(BF16), preferred_element_type=F32))
        den = (jnp.sum(s_mat, axis=1, keepdims=True)
               + w_inter * jnp.sum(q * n_h, axis=1, keepdims=True))
        hh = num / jnp.maximum(jnp.abs(den), jnp.exp(-m_t))
        b_last = b_c[L - 1:L, :]
        lw_c = b_last - b_c + ig_c
        lw_r = b_last - b_r + ig_r
        m_new = jnp.maximum(b_last + m_prev, jnp.max(lw_r, axis=1, keepdims=True))
        decay = jnp.exp(b_last + m_prev - m_new)
        kw = k * jnp.exp(lw_c - m_new)
        c_ref[0, h] = decay * c_h + lax.dot_general(
            kw.astype(BF16), vb, (((0,), (0,)), ((), ())), preferred_element_type=F32)
        n_ref[0, h:h + 1, :] = decay * n_h + jnp.sum(kw, axis=0, keepdims=True)
        m_ref[0, :, h:h + 1] = m_new
        ms = jnp.mean(hh * hh, axis=-1, keepdims=True)
        hn = hh * lax.rsqrt(ms + RMS_EPS) * gn_ref[:, h * dv:(h + 1) * dv]
        hm_ref[:, h * dv:(h + 1) * dv] = (hn * _sigmoid(mo_ref[:, h * dv:(h + 1) * dv])).astype(BF16)


def _mlstm(rows, z, sec_qk, gates, conv0, c0, n0, m0, conv_w, conv_b, b_igate, b_fgate, g_mlstm):
    bs, nh, dqk, dv = c0.shape
    kq = nh * dqk
    mw = nh * dv
    L = CHUNK
    assert 2 * kq == mw, "q/k conv width must equal the value width for the column sections"
    gbias = jnp.zeros((1, GATE_LANES), F32)
    gbias = gbias.at[0, 0:nh].set(b_igate.astype(F32)).at[0, nh:2 * nh].set(b_fgate.astype(F32))

    def sseq(g):
        return jnp.maximum(rows.seq_of_group(g) - rows.bp, 0)

    outs = pl.pallas_call(
        functools.partial(_mlstm_kernel, rows, nh, dqk, dv),
        out_shape=(jax.ShapeDtypeStruct((rows.rows, mw), BF16),
                   jax.ShapeDtypeStruct((rows.nseq, CONV_K - 1, 2 * kq), F32),
                   jax.ShapeDtypeStruct((rows.nseq, nh, dqk, dv), F32),
                   jax.ShapeDtypeStruct((rows.nseq, nh, dqk), F32),
                   jax.ShapeDtypeStruct((rows.nseq, 1, nh), F32)),
        grid=(rows.groups,),
        in_specs=[pl.BlockSpec((L, mw), lambda g: (g, sec_qk)),
                  pl.BlockSpec((L, mw), lambda g: (g, sec_qk + 1)),
                  pl.BlockSpec((L, mw), lambda g: (g, sec_qk + 2)),
                  pl.BlockSpec((L, GATE_LANES), lambda g: (g, 0)),
                  pl.BlockSpec((1, CONV_K - 1, 2 * kq), lambda g: (sseq(g), 0, 0)),
                  pl.BlockSpec((1, nh, dqk, dv), lambda g: (sseq(g), 0, 0, 0)),
                  pl.BlockSpec((1, nh, dqk), lambda g: (sseq(g), 0, 0)),
                  pl.BlockSpec((1, 1, nh), lambda g: (sseq(g), 0, 0)),
                  pl.BlockSpec((CONV_K, 2 * kq), lambda g: (0, 0)),
                  pl.BlockSpec((1, 2 * kq), lambda g: (0, 0)),
                  pl.BlockSpec((1, GATE_LANES), lambda g: (0, 0)),
                  pl.BlockSpec((1, mw), lambda g: (0, 0))],
        out_specs=(pl.BlockSpec((L, mw), lambda g: (g, 0)),
                   pl.BlockSpec((1, CONV_K - 1, 2 * kq), lambda g: (rows.seq_of_group(g), 0, 0)),
                   pl.BlockSpec((1, nh, dqk, dv), lambda g: (rows.seq_of_group(g), 0, 0, 0)),
                   pl.BlockSpec((1, nh, dqk), lambda g: (rows.seq_of_group(g), 0, 0)),
                   pl.BlockSpec((1, 1, nh), lambda g: (rows.seq_of_group(g), 0, 0))),
        scratch_shapes=[pltpu.VMEM((8 + L, 2 * kq), F32)],
        compiler_params=_cparams(1),
        name="mlstm",
    )(z, z, z, gates, conv0, c0, n0, m0.reshape(bs, 1, nh), conv_w, conv_b.reshape(1, 2 * kq),
      gbias, g_mlstm.reshape(1, mw))
    return outs


def _outproj_kernel(rows, tm, d, hw, og_ref, hm_ref, w_ref, xp_ref, xs_ref, mod_ref, o_ref, wb_ref):
    j, i = pl.program_id(0), pl.program_id(1)
    is_prompt = i < rows.groups_p * CHUNK // tm

    @pl.when(i == 0)
    def _():
        wb_ref[...] = w_ref[...].astype(BF16)

    mix = (jnp.dot(og_ref[...], wb_ref[0:hw, :], preferred_element_type=F32)
           + jnp.dot(hm_ref[...], wb_ref[hw:, :], preferred_element_type=F32))
    for gi in range(tm // CHUNK):
        seq = rows.seq_of_group(i * (tm // CHUNK) + gi)
        gate = mod_ref[pl.ds(seq, 1), :]
        rs = slice(gi * CHUNK, (gi + 1) * CHUNK)
        x = jnp.where(is_prompt, xp_ref[rs, :], xs_ref[rs, :])
        o_ref[rs, :] = x + gate * mix[rs, :]


def _outproj(rows, og, hm, w_out, x_p, x_s, mod, tm=1024, tn=512):
    d = x_p.shape[1]
    r = rows.rows
    hw = og.shape[1]
    tm, tn = min(tm, x_s.shape[0]), min(tn, d)
    assert r % tm == 0 and d % tn == 0
    xp_spec, xs_spec = _split_row_specs(rows, tm, tn, col_major=True)
    gate_blk0 = 2 * d // tn
    return pl.pallas_call(
        functools.partial(_outproj_kernel, rows, tm, d, hw),
        out_shape=jax.ShapeDtypeStruct((r, d), F32),
        grid=(d // tn, r // tm),
        in_specs=[pl.BlockSpec((tm, hw), lambda j, i: (i, 0)),
                  pl.BlockSpec((tm, hm.shape[1]), lambda j, i: (i, 0)),
                  pl.BlockSpec((w_out.shape[0], tn), lambda j, i: (0, j)),
                  xp_spec, xs_spec,
                  pl.BlockSpec((mod.shape[0], tn), lambda j, i: (0, gate_blk0 + j))],
        out_specs=pl.BlockSpec((tm, tn), lambda j, i: (i, j)),
        scratch_shapes=[pltpu.VMEM((w_out.shape[0], tn), BF16)],
        compiler_params=_cparams(2),
        name="out_proj",
    )(og, hm, w_out, x_p, x_s, mod)


def _router_kernel(rows, tm, d, ne, x_ref, g_ref, mod_ref, wr_ref, br_ref, ut_ref,
                   h2_ref, e_ref, gate_ref, rank_ref, cnt_ref, h2s_ref):
    i = pl.program_id(0)
    half = d // 2

    @pl.when(i == 0)
    def _():
        cnt_ref[...] = jnp.zeros_like(cnt_ref)

    g = g_ref[...]
    for gi in range(tm // CHUNK):
        seq = rows.seq_of_group(i * (tm // CHUNK) + gi)
        m = mod_ref[pl.ds(seq, 1), :]
        shift, scale = m[:, 3 * d:4 * d], m[:, 4 * d:5 * d]
        x = x_ref[gi * CHUNK:(gi + 1) * CHUNK, :]
        ms = jnp.mean(x * x, axis=-1, keepdims=True)
        h2s_ref[gi * CHUNK:(gi + 1) * CHUNK, :] = (x * lax.rsqrt(ms + RMS_EPS) * g) * (1.0 + scale) + shift
    h2 = h2s_ref[...]
    lo = pltpu.bitcast(h2[:, :half].astype(BF16).astype(F32), jnp.uint32)
    hi = pltpu.bitcast(h2[:, half:].astype(BF16).astype(F32), jnp.uint32)
    h2_ref[...] = (hi & jnp.uint32(0xFFFF0000)) | (lo >> 16)

    logits = jnp.dot(h2, wr_ref[...], precision=lax.Precision.HIGHEST,
                     preferred_element_type=F32)
    lt = logits.T[0:ne, :] + br_ref[...]
    eidx = lax.broadcasted_iota(jnp.int32, (ne, tm), 0)
    cur = lt
    tops, sels, hots = [], [], []
    for _ in range(TOP_K):
        mx = jnp.max(cur, axis=0, keepdims=True)
        sel = jnp.min(jnp.where(cur == mx, eidx, ne), axis=0, keepdims=True)
        hot = eidx == sel
        tops.append(mx)
        sels.append(sel)
        hots.append(hot)
        cur = jnp.where(hot, -jnp.inf, cur)
    ex = [jnp.exp(t - tops[0]) for t in tops]
    tot = ex[0] + ex[1] + ex[2] + ex[3]
    oh = jnp.zeros((ne, tm), F32)
    for hot in hots:
        oh = oh + hot.astype(F32)
    prefix = jnp.dot(oh.astype(BF16), ut_ref[...], preferred_element_type=F32)
    base = cnt_ref[:, 0:1]
    pos = base + prefix
    for kk in range(TOP_K):
        e_ref[kk:kk + 1, :] = sels[kk]
        gate_ref[kk:kk + 1, :] = ex[kk] / tot
        rank_ref[kk:kk + 1, :] = jnp.sum(jnp.where(hots[kk], pos, 0.0), axis=0,
                                         keepdims=True).astype(jnp.int32)
    cnt_ref[...] = cnt_ref[...] + jnp.sum(oh, axis=1, keepdims=True)


def _router(rows, x1, g_ffn, mod, w_router, b_router, tm=256):
    r, d = x1.shape
    ne = w_router.shape[1]
    tm = min(tm, r)
    assert r % tm == 0 and ne % 8 == 0 and ne <= LANES
    wr_pad = jnp.zeros((d, LANES), F32).at[:, :ne].set(w_router.astype(F32))
    tt = np.arange(tm)
    ut = jnp.asarray((tt[:, None] < tt[None, :]).astype(np.float32), BF16)
    return pl.pallas_call(
        functools.partial(_router_kernel, rows, tm, d, ne),
        out_shape=(jax.ShapeDtypeStruct((r, d // 2), jnp.uint32),
                   jax.ShapeDtypeStruct((TOP_K, r), jnp.int32),
                   jax.ShapeDtypeStruct((TOP_K, r), F32),
                   jax.ShapeDtypeStruct((TOP_K, r), jnp.int32),
                   jax.ShapeDtypeStruct((ne, LANES), F32)),
        grid=(r // tm,),
        in_specs=[pl.BlockSpec((tm, d), lambda i: (i, 0)),
                  pl.BlockSpec((1, d), lambda i: (0, 0)),
                  pl.BlockSpec(mod.shape, lambda i: (0, 0)),
                  pl.BlockSpec((d, LANES), lambda i: (0, 0)),
                  pl.BlockSpec((ne, 1), lambda i: (0, 0)),
                  pl.BlockSpec((tm, tm), lambda i: (0, 0))],
        out_specs=(pl.BlockSpec((tm, d // 2), lambda i: (i, 0)),
                   pl.BlockSpec((TOP_K, tm), lambda i: (0, i)),
                   pl.BlockSpec((TOP_K, tm), lambda i: (0, i)),
                   pl.BlockSpec((TOP_K, tm), lambda i: (0, i)),
                   pl.BlockSpec((ne, LANES), lambda i: (0, 0))),
        scratch_shapes=[pltpu.VMEM((tm, d), F32)],
        compiler_params=_cparams(1),
        name="router",
    )(x1, g_ffn.reshape(1, d), mod, wr_pad, b_router.reshape(ne, 1).astype(F32), ut)


def _invert_kernel(tm, pstart_ref, e_ref, rank_ref, rt_ref):
    i = pl.program_id(0)

    @pl.when(i == 0)
    def _():
        def clear(r, carry):
            rt_ref[r] = 0
            return carry

        lax.fori_loop(0, rt_ref.shape[0], clear, 0, unroll=8)

    def body(t, carry):
        for kk in range(TOP_K):
            rt_ref[pstart_ref[e_ref[kk, t]] + rank_ref[kk, t]] = i * tm + t
        return carry

    lax.fori_loop(0, tm, body, 0, unroll=4)


def _invert(e_idx, rank, pstart, n_rows, tm=256):
    r = e_idx.shape[1]
    tm = min(tm, r)
    assert r % tm == 0
    return pl.pallas_call(
        functools.partial(_invert_kernel, tm),
        out_shape=jax.ShapeDtypeStruct((n_rows,), jnp.int32),
        grid_spec=pltpu.PrefetchScalarGridSpec(
            num_scalar_prefetch=1,
            grid=(r // tm,),
            in_specs=[pl.BlockSpec((TOP_K, tm), lambda i, *_: (0, i), memory_space=pltpu.SMEM),
                      pl.BlockSpec((TOP_K, tm), lambda i, *_: (0, i), memory_space=pltpu.SMEM)],
            out_specs=pl.BlockSpec(memory_space=pltpu.SMEM)),
        compiler_params=_cparams(1),
        name="moe_invert",
    )(pstart, e_idx, rank)


def _unpack_rows(xw):
    lo = pltpu.bitcast(xw << 16, F32).astype(BF16)
    hi = pltpu.bitcast(xw & jnp.uint32(0xFFFF0000), F32).astype(BF16)
    return lo, hi


def _gather_kernel(nused_ref, rt_cur_ref, rt_next_ref, h2_ref, xs_ref, buf_ref, sem):
    b = pl.program_id(0)
    n_used = nused_ref[0]
    half = buf_ref.shape[2]

    def row_copy(slot, src_row, dst_row):
        return pltpu.make_async_copy(h2_ref.at[pl.ds(src_row, 1), :],
                                     buf_ref.at[slot, pl.ds(dst_row, 1), :], sem.at[slot])

    def issue(rt_ref, slot):
        def pair(t2, carry):
            for p in range(2):
                t = 2 * t2 + p
                row_copy(slot, rt_ref[0, 0, t], t).start(priority=p)
            return carry

        lax.fori_loop(0, MOE_BLOCK // 2, pair, 0, unroll=4)

    @pl.when(jnp.logical_and(b == 0, n_used > 0))
    def _():
        issue(rt_cur_ref, 0)

    @pl.when(b + 1 < n_used)
    def _():
        issue(rt_next_ref, (b + 1) % 2)

    @pl.when(b < n_used)
    def _():
        slot = b % 2

        def drain(t, carry):
            row_copy(slot, 0, 0).wait()
            return carry

        lax.fori_loop(0, MOE_BLOCK, drain, 0, unroll=8)
        lo, hi = _unpack_rows(buf_ref[slot])
        xs_ref[:, 0:half] = lo
        xs_ref[:, half:] = hi

    @pl.when(b >= n_used)
    def _():
        xs_ref[...] = jnp.zeros_like(xs_ref)


def _gather_rows(h2p, row_tok, n_used):
    r, half = h2p.shape
    n_blocks = row_tok.shape[0] // MOE_BLOCK
    rt3 = row_tok.reshape(n_blocks, 1, MOE_BLOCK)
    return pl.pallas_call(
        _gather_kernel,
        out_shape=jax.ShapeDtypeStruct((n_blocks * MOE_BLOCK, 2 * half), BF16),
        grid_spec=pltpu.PrefetchScalarGridSpec(
            num_scalar_prefetch=1,
            grid=(n_blocks,),
            in_specs=[pl.BlockSpec((1, 1, MOE_BLOCK), lambda b, *_: (b, 0, 0), memory_space=pltpu.SMEM),
                      pl.BlockSpec((1, 1, MOE_BLOCK), lambda b, *_: (jnp.minimum(b + 1, n_blocks - 1), 0, 0),
                                   memory_space=pltpu.SMEM),
                      pl.BlockSpec(memory_space=pl.ANY)],
            out_specs=pl.BlockSpec((MOE_BLOCK, 2 * half), lambda b, *_: (b, 0)),
            scratch_shapes=[pltpu.VMEM((2, MOE_BLOCK, half), jnp.uint32),
                            pltpu.SemaphoreType.DMA((2,))]),
        compiler_params=_cparams(1),
        name="moe_gather",
    )(n_used.reshape(1), rt3, rt3, h2p)


MOE_CACHE_BLOCKS = 4


class _ExpertRows:
    def __init__(self, src_hbm, slots_ref, sems, b0, nb, first_tile):
        self.src, self.slots, self.sems = src_hbm, slots_ref, sems
        self.b0, self.nb, self.first_tile = b0, nb, first_tile

    def slot_of(self, i):
        return jnp.where(i < MOE_CACHE_BLOCKS, i, MOE_CACHE_BLOCKS + i % 2)

    def _needs_load(self, i):
        return jnp.logical_or(self.first_tile, i >= MOE_CACHE_BLOCKS)

    def _copy(self, i):
        row0 = pl.multiple_of((self.b0 + i) * MOE_BLOCK, MOE_BLOCK)
        return pltpu.make_async_copy(self.src.at[pl.ds(row0, MOE_BLOCK), :],
                                     self.slots.at[self.slot_of(i)], self.sems.at[i % 2])

    def request(self, i):
        @pl.when(jnp.logical_and(i < self.nb, self._needs_load(i)))
        def _():
            self._copy(i).start()

    def arrive(self, i):
        @pl.when(self._needs_load(i))
        def _():
            self._copy(i).wait()

    def request_first(self, e):
        @pl.when(e == 0)
        def _():
            self.request(0)

    def request_next_expert(self, e, ne, last_tile, bstart_ref, nblk_ref):
        @pl.when(jnp.logical_and(last_tile, e + 1 < ne))
        def _():
            @pl.when(nblk_ref[e + 1] > 0)
            def _():
                row0 = pl.multiple_of(bstart_ref[e + 1] * MOE_BLOCK, MOE_BLOCK)
                pltpu.make_async_copy(self.src.at[pl.ds(row0, MOE_BLOCK), :],
                                      self.slots.at[0], self.sems.at[0]).start()


def _block_window(dst_hbm, blk, col0, width):
    row0 = pl.multiple_of(blk * MOE_BLOCK, MOE_BLOCK)
    return dst_hbm.at[pl.ds(row0, MOE_BLOCK), pl.ds(pl.multiple_of(col0, width), width)]


def _zero_tail_blocks(stage_ref, dst_hbm, sem, first_blk, col0, width):
    n_blocks = dst_hbm.shape[0] // MOE_BLOCK
    stage_ref[0] = jnp.zeros(stage_ref.shape[1:], stage_ref.dtype)

    def cp(blk):
        return pltpu.make_async_copy(stage_ref.at[0], _block_window(dst_hbm, blk, col0, width), sem.at[0])

    def start(blk, carry):
        cp(blk).start()
        return carry

    def wait(blk, carry):
        cp(blk).wait()
        return carry

    lax.fori_loop(first_blk, n_blocks, start, 0)
    lax.fori_loop(first_blk, n_blocks, wait, 0)


class _OutRing:
    def __init__(self, stage_ref, sems, cnt_ref, dst_hbm, width):
        self.stage, self.sems, self.cnt, self.dst, self.width = stage_ref, sems, cnt_ref, dst_hbm, width

    def init(self):
        self.cnt[0] = 0
        self.cnt[1] = 0

    def _await_next(self, limit):
        w = self.cnt[1]

        @pl.when(w < limit)
        def _():
            pltpu.make_async_copy(self.stage.at[w % 2], _block_window(self.dst, 0, 0, self.width),
                                  self.sems.at[w % 2]).wait()
            self.cnt[1] = w + 1

    def reserve(self):
        self._await_next(self.cnt[0] - 1)

    def push(self, value, blk, col0):
        c = self.cnt[0]
        self.stage[c % 2] = value
        pltpu.make_async_copy(self.stage.at[c % 2], _block_window(self.dst, blk, col0, self.width),
                              self.sems.at[c % 2]).start()
        self.cnt[0] = c + 1

    def drain(self):
        for _ in range(2):
            self._await_next(self.cnt[0])


MOE_CAST_CHUNKS = 4


def _cast_chunk(w_ref, wbf_ref, c):
    kc = wbf_ref.shape[0] // MOE_CAST_CHUNKS
    wbf_ref[c * kc:(c + 1) * kc, :] = w_ref[0, c * kc:(c + 1) * kc, :].astype(BF16)


def _first_block_dot(x_blk_ref, w_ref, wbf_ref):
    kc = wbf_ref.shape[0] // MOE_CAST_CHUNKS
    acc = None
    for c in range(MOE_CAST_CHUNKS):
        if c + 1 < MOE_CAST_CHUNKS:
            _cast_chunk(w_ref, wbf_ref, c + 1)
        part = jnp.dot(x_blk_ref[:, c * kc:(c + 1) * kc], wbf_ref[c * kc:(c + 1) * kc, :],
                       preferred_element_type=F32)
        acc = part if acc is None else acc + part
    return acc


def _moe_up_kernel(tf, nj, ne, bstart_ref, nblk_ref, xs_hbm, w_ref, b_ref, act_hbm,
                   wbf_ref, xb_ref, gt_ref, ostage_ref, gu_carry_ref, cnt_ref, xsem, osem):
    g = pl.program_id(0)
    e, j = g // nj, g % nj
    nb, b0 = nblk_ref[e], bstart_ref[e]
    nslab = MOE_BLOCK // LANES
    xrows = _ExpertRows(xs_hbm, xb_ref, xsem, b0, nb, j == 0)
    ring = _OutRing(ostage_ref, osem, cnt_ref, act_hbm, tf)
    PEND, PEND_BLK, PEND_COL = 2, 3, 4

    @pl.when(g == 0)
    def _():
        ring.init()
        cnt_ref[PEND] = 0
        gu_carry_ref[...] = jnp.zeros_like(gu_carry_ref)

    def up_dot(i):
        return jnp.dot(xb_ref[xrows.slot_of(i)], wbf_ref[...], preferred_element_type=F32) + b_ref[0]

    def swiglu(gu):
        gut = gu.T
        for sl in range(nslab):
            gt_ref[sl] = gut[:, sl * LANES:(sl + 1) * LANES]
        parts = []
        for sl in range(nslab):
            gate = jnp.minimum(gt_ref[sl, pl.ds(0, tf, stride=2), :], SWIGLU_LIMIT)
            up = jnp.clip(gt_ref[sl, pl.ds(1, tf, stride=2), :], -SWIGLU_LIMIT, SWIGLU_LIMIT)
            parts.append((up + 1.0) * gate * _sigmoid(SWIGLU_ALPHA * gate))
        return jnp.concatenate(parts, axis=1).T.astype(BF16)

    @pl.when(nb > 0)
    def _():
        xrows.request_first(e)
        _cast_chunk(w_ref, wbf_ref, 0)
        xrows.arrive(0)
        xrows.request(1)
        ring.reserve()
        owed = cnt_ref[PEND] == 1
        gu0 = _first_block_dot(xb_ref.at[0], w_ref, wbf_ref) + b_ref[0]
        act_owed = swiglu(gu_carry_ref[...])

        @pl.when(owed)
        def _():
            ring.push(act_owed, cnt_ref[PEND_BLK], cnt_ref[PEND_COL])
            cnt_ref[PEND] = 0

        def body(i, gu_prev):
            xrows.arrive(i)
            xrows.request(i + 1)
            ring.reserve()
            gu = up_dot(i)
            ring.push(swiglu(gu_prev), b0 + i - 1, j * tf)
            return gu

        gu_last = lax.fori_loop(1, nb, body, gu0)

        @pl.when(e < ne - 1)
        def _():
            gu_carry_ref[...] = gu_last
            cnt_ref[PEND] = 1
            cnt_ref[PEND_BLK] = b0 + nb - 1
            cnt_ref[PEND_COL] = j * tf

        @pl.when(e == ne - 1)
        def _():
            ring.reserve()
            ring.push(swiglu(gu_last), b0 + nb - 1, j * tf)

    xrows.request_next_expert(e, ne, j == nj - 1, bstart_ref, nblk_ref)

    @pl.when(e == ne - 1)
    def _():
        @pl.when(cnt_ref[PEND] == 1)
        def _():
            ring.reserve()
            ring.push(swiglu(gu_carry_ref[...]), cnt_ref[PEND_BLK], cnt_ref[PEND_COL])
            cnt_ref[PEND] = 0

        ring.drain()
        _zero_tail_blocks(ostage_ref, act_hbm, osem, b0 + nb, j * tf, tf)


def _moe_up(xs, w_gu, b_gu, bstart, nblk, tf=512):
    n_rows, d = xs.shape
    ne, _, f2 = w_gu.shape
    f = f2 // 2
    tf = min(tf, f)
    assert f % tf == 0 and w_gu.shape[1] == d
    nj = f // tf
    return pl.pallas_call(
        functools.partial(_moe_up_kernel, tf, nj, ne),
        out_shape=jax.ShapeDtypeStruct((n_rows, f), BF16),
        grid_spec=pltpu.PrefetchScalarGridSpec(
            num_scalar_prefetch=2,
            grid=(ne * nj,),
            in_specs=[pl.BlockSpec(memory_space=pl.ANY),
                      pl.BlockSpec((1, d, 2 * tf), lambda g, *_: (g // nj, 0, g % nj)),
                      pl.BlockSpec((1, 1, 2 * tf), lambda g, *_: (g // nj, 0, g % nj))],
            out_specs=pl.BlockSpec(memory_space=pl.ANY),
            scratch_shapes=[pltpu.VMEM((d, 2 * tf), BF16),
                            pltpu.VMEM((MOE_CACHE_BLOCKS + 2, MOE_BLOCK, d), BF16),
                            pltpu.VMEM((MOE_BLOCK // LANES, 2 * tf, LANES), F32),
                            pltpu.VMEM((2, MOE_BLOCK, tf), BF16),
                            pltpu.VMEM((MOE_BLOCK, 2 * tf), F32),
                            pltpu.SMEM((8,), jnp.int32),
                            pltpu.SemaphoreType.DMA((2,)),
                            pltpu.SemaphoreType.DMA((2,))]),
        compiler_params=_cparams(1),
        name="moe_up",
    )(bstart, nblk, xs, w_gu, b_gu.reshape(ne, 1, f2))


def _moe_down_kernel(tn, nj, ne, bstart_ref, nblk_ref, act_hbm, w_ref, b_ref, yb_hbm,
                     wbf_ref, ab_ref, ostage_ref, cnt_ref, asem, osem):
    g = pl.program_id(0)
    e, j = g // nj, g % nj
    nb, b0 = nblk_ref[e], bstart_ref[e]
    arows = _ExpertRows(act_hbm, ab_ref, asem, b0, nb, j == 0)
    ring = _OutRing(ostage_ref, osem, cnt_ref, yb_hbm, tn)

    @pl.when(g == 0)
    def _():
        ring.init()

    @pl.when(nb > 0)
    def _():
        arows.request_first(e)
        _cast_chunk(w_ref, wbf_ref, 0)
        arows.arrive(0)
        arows.request(1)
        ring.reserve()
        ring.push(_first_block_dot(ab_ref.at[0], w_ref, wbf_ref) + b_ref[0], b0, j * tn)

        def body(i, carry):
            arows.arrive(i)
            arows.request(i + 1)
            ring.reserve()
            y = jnp.dot(ab_ref[arows.slot_of(i)], wbf_ref[...], preferred_element_type=F32) + b_ref[0]
            ring.push(y, b0 + i, j * tn)
            return carry

        lax.fori_loop(1, nb, body, 0)

    arows.request_next_expert(e, ne, j == nj - 1, bstart_ref, nblk_ref)

    @pl.when(e == ne - 1)
    def _():
        ring.drain()
        _zero_tail_blocks(ostage_ref, yb_hbm, osem, b0 + nb, j * tn, tn)


def _moe_down(act, w_d, b_d, bstart, nblk, tn=1024):
    n_rows, f = act.shape
    ne, _, d = w_d.shape
    tn = min(tn, d)
    assert d % tn == 0
    nj = d // tn
    return pl.pallas_call(
        functools.partial(_moe_down_kernel, tn, nj, ne),
        out_shape=jax.ShapeDtypeStruct((n_rows, d), F32),
        grid_spec=pltpu.PrefetchScalarGridSpec(
            num_scalar_prefetch=2,
            grid=(ne * nj,),
            in_specs=[pl.BlockSpec(memory_space=pl.ANY),
                      pl.BlockSpec((1, f, tn), lambda g, *_: (g // nj, 0, g % nj)),
                      pl.BlockSpec((1, 1, tn), lambda g, *_: (g // nj, 0, g % nj))],
            out_specs=pl.BlockSpec(memory_space=pl.ANY),
            scratch_shapes=[pltpu.VMEM((f, tn), BF16),
                            pltpu.VMEM((MOE_CACHE_BLOCKS + 2, MOE_BLOCK, f), BF16),
                            pltpu.VMEM((2, MOE_BLOCK, tn), F32),
                            pltpu.SMEM((2,), jnp.int32),
                            pltpu.SemaphoreType.DMA((2,)),
                            pltpu.SemaphoreType.DMA((2,))]),
        compiler_params=_cparams(1),
        name="moe_down",
    )(bstart, nblk, act, w_d, b_d.reshape(ne, 1, d))


def _combine_kernel(rows, tm, d, n_tiles, pstart_ref, e_ref, rank_ref, en_ref, rankn_ref, gt_ref,
                    x_ref, mod_ref, gf_ref, yb_ref, yp_ref, ys_ref, buf_ref, sem):
    i = pl.program_id(0)
    is_prompt = i < rows.groups_p * CHUNK // tm
    slot = i % 2

    def row_copy(s, dst_k, dst_t, src_row):
        return pltpu.make_async_copy(yb_ref.at[pl.ds(src_row, 1), :],
                                     buf_ref.at[s, dst_k, pl.ds(dst_t, 1), :], sem.at[s])

    def issue(eref, rref, s):
        def one(t, carry):
            for kk in range(TOP_K):
                src = pstart_ref[eref[kk, t]] + rref[kk, t]
                row_copy(s, kk, t, src).start(priority=kk % 2)
            return carry

        lax.fori_loop(0, tm, one, 0, unroll=2)

    @pl.when(i == 0)
    def _():
        issue(e_ref, rank_ref, 0)

    @pl.when(i + 1 < n_tiles)
    def _():
        issue(en_ref, rankn_ref, (i + 1) % 2)

    def drain(t, carry):
        for kk in range(TOP_K):
            row_copy(slot, 0, 0, 0).wait()
        return carry

    lax.fori_loop(0, tm, drain, 0, unroll=2)

    gf = gf_ref[...]
    for gi in range(tm // CHUNK):
        rs = slice(gi * CHUNK, (gi + 1) * CHUNK)
        seq = rows.seq_of_group(i * (tm // CHUNK) + gi)
        gate2 = mod_ref[pl.ds(seq, 1), :]
        ff = jnp.zeros((CHUNK, d), F32)
        for kk in range(TOP_K):
            ff = ff + buf_ref[slot, kk, rs, :] * gt_ref[rs, kk:kk + 1]
        x2 = x_ref[rs, :] + gate2 * ff
        ms = jnp.mean(x2 * x2, axis=-1, keepdims=True)
        y = x2 * lax.rsqrt(ms + RMS_EPS) * gf

        @pl.when(is_prompt)
        def _():
            yp_ref[rs, :] = y

        @pl.when(jnp.logical_not(is_prompt))
        def _():
            ys_ref[rs, :] = y


def _combine(rows, x1, mod, g_final, yb, e_idx, rank, gates_t, pstart, tm=128):
    r, d = x1.shape
    tm = min(tm, r)
    assert r % tm == 0
    gate2_blk = 5
    yp_spec, ys_spec = _split_row_specs(rows, tm, d)
    n_tiles = r // tm

    def nxt(i, *_):
        return (0, jnp.minimum(i + 1, n_tiles - 1))

    return pl.pallas_call(
        functools.partial(_combine_kernel, rows, tm, d, n_tiles),
        out_shape=(jax.ShapeDtypeStruct((rows.groups_p * CHUNK, d), F32),
                   jax.ShapeDtypeStruct((r - rows.groups_p * CHUNK, d), F32)),
        grid_spec=pltpu.PrefetchScalarGridSpec(
            num_scalar_prefetch=1,
            grid=(n_tiles,),
            in_specs=[pl.BlockSpec((TOP_K, tm), lambda i, *_: (0, i), memory_space=pltpu.SMEM),
                      pl.BlockSpec((TOP_K, tm), lambda i, *_: (0, i), memory_space=pltpu.SMEM),
                      pl.BlockSpec((TOP_K, tm), nxt, memory_space=pltpu.SMEM),
                      pl.BlockSpec((TOP_K, tm), nxt, memory_space=pltpu.SMEM),
                      pl.BlockSpec((tm, TOP_K), lambda i, *_: (i, 0)),
                      pl.BlockSpec((tm, d), lambda i, *_: (i, 0)),
                      pl.BlockSpec((mod.shape[0], d), lambda i, *_: (0, gate2_blk)),
                      pl.BlockSpec((1, d), lambda i, *_: (0, 0)),
                      pl.BlockSpec(memory_space=pl.ANY)],
            out_specs=(yp_spec, ys_spec),
            scratch_shapes=[pltpu.VMEM((2, TOP_K, tm, d), F32),
                            pltpu.SemaphoreType.DMA((2,))]),
        compiler_params=_cparams(1),
        name="moe_combine",
    )(pstart, e_idx, rank, e_idx, rank, gates_t, x1, mod, g_final.reshape(1, d), yb)


def kernel(x_prompt, x_sample, c_prompt, c_sample, state_hgrn_S, state_conv, state_mlstm_C,
           state_mlstm_n, state_mlstm_m, w_ada, b_ada, g_mix, g_ffn, w_in, lb_logits, conv_w,
           conv_b, b_igate, b_fgate, g_hgrn, g_mlstm, w_out, w_router, b_router, w_gate_up,
           b_gate_up, w_down, b_down, g_final):
    depth = w_ada.shape[0]
    assert depth == 1, "single-layer trunk"
    bp, tp, d = x_prompt.shape
    bs, ts, _ = x_sample.shape
    rows = _Rows(bp, tp, bs, ts)
    _, _, nh_h, dk, dv = state_hgrn_S.shape
    _, _, nh_m, dqk, dvm = state_mlstm_C.shape
    hw = nh_h * dk
    mw = nh_m * dvm
    n_main = 4 * hw + 3 * mw
    assert w_in.shape[2] == n_main + 2 * nh_m and 2 * nh_m <= GATE_LANES
    assert mw == hw, "column sections of the input projection are addressed in hw-wide blocks"
    ne = w_router.shape[2]

    lb = jax.nn.softmax(lb_logits.astype(F32), axis=0)[0]

    x_p = x_prompt.reshape(bp * tp, d)
    x_s = x_sample.reshape(bs * ts, d)
    n_c = bp + bs
    n_c_pad = -(-n_c // 8) * 8
    c_pad = jnp.zeros((n_c_pad, d), F32).at[:n_c].set(jnp.concatenate([c_prompt, c_sample], axis=0))
    mod = _ada(c_pad, w_ada[0], b_ada[0])

    w_in_t = jnp.swapaxes(w_in, 1, 2)
    h_all, gates = _prep(rows, x_p, x_s, g_mix[0], mod, w_in_t, n_main)
    z = _inproj(h_all, w_in_t, n_main)

    og, st = _hgrn(rows, z, state_hgrn_S[0], lb, g_hgrn[0])
    hm, conv_new, c_new, n_new, m_new = _mlstm(
        rows, z, 4, gates, state_conv[0], state_mlstm_C[0], state_mlstm_n[0], state_mlstm_m[0],
        conv_w[0], conv_b[0], b_igate[0], b_fgate[0], g_mlstm[0])

    x1 = _outproj(rows, og, hm, w_out[0], x_p, x_s, mod)

    h2p, e_idx, gate_k, rank, cnt = _router(rows, x1, g_ffn[0], mod, w_router[0], b_router[0])

    counts = cnt[:, 0].astype(jnp.int32)
    nblk_e = (counts + MOE_BLOCK - 1) // MOE_BLOCK
    padded = nblk_e * MOE_BLOCK
    pad_end = jnp.cumsum(padded)
    pstart = jnp.concatenate([jnp.zeros((1,), jnp.int32), pad_end]).astype(jnp.int32)
    n_blocks_max = -(-(rows.rows * TOP_K) // MOE_BLOCK) + ne
    n_rows = n_blocks_max * MOE_BLOCK
    blk_start_e = (pstart[:ne] // MOE_BLOCK).astype(jnp.int32)
    n_used = (pstart[ne] // MOE_BLOCK).astype(jnp.int32)

    row_tok = _invert(e_idx, rank, pstart, n_rows)
    xs = _gather_rows(h2p, row_tok, n_used)
    act = _moe_up(xs, w_gate_up[0], b_gate_up[0], blk_start_e, nblk_e.astype(jnp.int32))
    yb = _moe_down(act, w_down[0], b_down[0], blk_start_e, nblk_e.astype(jnp.int32))

    y_p, y_s = _combine(rows, x1, mod, g_final, yb, e_idx, rank, gate_k.T, pstart)

    y_prompt = y_p.reshape(bp, tp, d)
    y_sample = y_s.reshape(bs, ts, d)
    s_all = jnp.swapaxes(st, 2, 3)
    m_all = m_new.reshape(rows.nseq, nh_m)

    def split(a):
        return a[:bp][None], a[bp:][None]

    p_s, s_s = split(s_all)
    p_conv, s_conv = split(conv_new)
    p_c, s_c = split(c_new)
    p_n, s_n = split(n_new)
    p_m, s_m = split(m_all)
    return (y_prompt, y_sample, p_s, p_conv, p_c, p_n, p_m, s_s, s_conv, s_c, s_n, s_m)
```

```python
import functools

import numpy as np
import jax
import jax.numpy as jnp
from jax import lax
from jax.experimental import pallas as pl
from jax.experimental.pallas import tpu as pltpu

F32 = jnp.float32
BF16 = jnp.bfloat16

CHUNK = 64
TOP_K = 4
CONV_K = 4
RMS_EPS = 1e-6
SWIGLU_LIMIT = 7.0
SWIGLU_ALPHA = 1.702
MOE_BLOCK = 256
LANES = 128
GATE_LANES = 128
VMEM_LIMIT = 60 * 1024 * 1024
HGRN_LEVELS = (32, 16, 8, 4, 2, 1)
HGRN_MATMUL_LEVELS = (2, 1)
HGRN_HEAD_UNROLL = 8


def _cparams(n_axes, vmem=VMEM_LIMIT):
    return pltpu.CompilerParams(dimension_semantics=("arbitrary",) * n_axes,
                                vmem_limit_bytes=vmem)


def _sigmoid(x):
    return 1.0 / (1.0 + jnp.exp(-x))


def _silu(x):
    return x * _sigmoid(x)


class _Rows:
    def __init__(self, bp, tp, bs, ts):
        assert tp % CHUNK == 0 and ts % CHUNK == 0
        self.bp, self.tp, self.bs, self.ts = bp, tp, bs, ts
        self.nblk_p, self.nblk_s = tp // CHUNK, ts // CHUNK
        self.groups_p = bp * self.nblk_p
        self.groups = self.groups_p + bs * self.nblk_s
        self.rows = self.groups * CHUNK
        self.nseq = bp + bs

    def seq_of_group(self, g):
        return jnp.where(g < self.groups_p, g // self.nblk_p,
                         self.bp + (g - self.groups_p) // self.nblk_s)

    def blk_in_seq(self, g):
        return jnp.where(g < self.groups_p, g % self.nblk_p, (g - self.groups_p) % self.nblk_s)

    def nblk_of_group(self, g):
        return jnp.where(g < self.groups_p, self.nblk_p, self.nblk_s)


def _ada_kernel(c_ref, w_ref, b_ref, o_ref):
    c = c_ref[...]
    s = _silu(c).astype(BF16)
    o_ref[...] = jnp.dot(s, w_ref[...].astype(BF16), preferred_element_type=F32) + b_ref[...]


def _ada(c_pad, w_ada, b_ada, tn=512):
    m, d = c_pad.shape
    n = w_ada.shape[1]
    tn = min(tn, n)
    assert n % tn == 0
    return pl.pallas_call(
        _ada_kernel,
        out_shape=jax.ShapeDtypeStruct((m, n), F32),
        grid=(n // tn,),
        in_specs=[pl.BlockSpec((m, d), lambda j: (0, 0)),
                  pl.BlockSpec((d, tn), lambda j: (0, j)),
                  pl.BlockSpec((1, tn), lambda j: (0, j))],
        out_specs=pl.BlockSpec((m, tn), lambda j: (0, j)),
        compiler_params=_cparams(1),
        name="ada_mod",
    )(c_pad, w_ada, b_ada.reshape(1, n))


def _prep_kernel(rows, tm, d, n_gate, xp_ref, xs_ref, g_ref, mod_ref, wtail_ref,
                 h_ref, gates_ref, wg_ref, hf_ref):
    i = pl.program_id(0)
    is_prompt = i < rows.groups_p * CHUNK // tm

    @pl.when(i == 0)
    def _():
        lanes = wg_ref.shape[1]
        padded = jnp.concatenate([wtail_ref[0], jnp.zeros((lanes - n_gate, d), F32)], axis=0)
        wg_ref[...] = padded.T

    g = g_ref[...]
    for gi in range(tm // CHUNK):
        seq = rows.seq_of_group(i * (tm // CHUNK) + gi)
        m = mod_ref[pl.ds(seq, 1), :]
        shift, scale = m[:, 0:d], m[:, d:2 * d]
        rs = slice(gi * CHUNK, (gi + 1) * CHUNK)
        x = jnp.where(is_prompt, xp_ref[rs, :], xs_ref[rs, :])
        ms = jnp.mean(x * x, axis=-1, keepdims=True)
        y = x * lax.rsqrt(ms + RMS_EPS) * g
        h = y * (1.0 + scale) + shift
        h_ref[rs, :] = h.astype(BF16)
        hf_ref[rs, :] = h
    gates_ref[...] = jnp.dot(hf_ref[...], wg_ref[...], precision=lax.Precision.HIGHEST,
                             preferred_element_type=F32)


def _split_row_specs(rows, tm, tn, col_major=False):
    n_pt = rows.groups_p * CHUNK // tm
    n_st = rows.rows // tm - n_pt
    assert n_pt * tm == rows.groups_p * CHUNK and n_st >= 1

    def ij(args):
        return (args[1], args[0]) if col_major else (args[0], 0)

    def p_map(*args):
        i, j = ij(args)
        return (jnp.minimum(i, n_pt - 1), j)

    def s_map(*args):
        i, j = ij(args)
        return (jnp.maximum(i - n_pt, 0), j)

    return pl.BlockSpec((tm, tn), p_map), pl.BlockSpec((tm, tn), s_map)


def _prep(rows, x_p, x_s, g_mix, mod, w_in_t, n_main, tm=256):
    d = x_p.shape[1]
    r = rows.rows
    tm = min(tm, x_s.shape[0])
    n_gate = w_in_t.shape[1] - n_main
    assert r % tm == 0 and n_gate % 8 == 0 and n_main % n_gate == 0 and n_gate <= GATE_LANES
    xp_spec, xs_spec = _split_row_specs(rows, tm, d)
    return pl.pallas_call(
        functools.partial(_prep_kernel, rows, tm, d, n_gate),
        out_shape=(jax.ShapeDtypeStruct((r, d), BF16),
                   jax.ShapeDtypeStruct((r, GATE_LANES), F32)),
        grid=(r // tm,),
        in_specs=[xp_spec, xs_spec,
                  pl.BlockSpec((1, d), lambda i: (0, 0)),
                  pl.BlockSpec(mod.shape, lambda i: (0, 0)),
                  pl.BlockSpec((1, n_gate, d), lambda i: (0, n_main // n_gate, 0))],
        out_specs=(pl.BlockSpec((tm, d), lambda i: (i, 0)),
                   pl.BlockSpec((tm, GATE_LANES), lambda i: (i, 0))),
        scratch_shapes=[pltpu.VMEM((d, GATE_LANES), F32),
                        pltpu.VMEM((tm, d), F32)],
        compiler_params=_cparams(1),
        name="prep_norm_mod",
    )(x_p, x_s, g_mix.reshape(1, d), mod, w_in_t)


def _inproj_kernel(h_ref, w_ref, z_ref, wb_ref):
    @pl.when(pl.program_id(1) == 0)
    def _():
        wb_ref[...] = w_ref[0].astype(BF16)

    z_ref[...] = lax.dot_general(h_ref[...], wb_ref[...], (((1,), (1,)), ((), ())),
                                 preferred_element_type=F32)


def _inproj(h_all, w_in_t, n_main, tm=1024, tn=512):
    r, d = h_all.shape
    tm, tn = min(tm, r), min(tn, n_main)
    assert r % tm == 0 and n_main % tn == 0
    return pl.pallas_call(
        _inproj_kernel,
        out_shape=jax.ShapeDtypeStruct((r, n_main), F32),
        grid=(n_main // tn, r // tm),
        in_specs=[pl.BlockSpec((tm, d), lambda j, i: (i, 0)),
                  pl.BlockSpec((1, tn, d), lambda j, i: (0, j, 0))],
        out_specs=pl.BlockSpec((tm, tn), lambda j, i: (i, j)),
        scratch_shapes=[pltpu.VMEM((tn, d), BF16)],
        compiler_params=_cparams(2),
        name="in_proj",
    )(h_all, w_in_t)


def _hgrn_consts():
    L = CHUNK
    t = np.arange(L)[:, None]
    s = np.arange(L)[None, :]
    mats = [(s <= t)]
    masks = []
    for m in HGRN_LEVELS:
        start = (t // (2 * m)) * (2 * m)
        if m in HGRN_MATMUL_LEVELS:
            mats.append(s <= start + m - 1)
        masks.append((t // (2 * m) == s // (2 * m)) & (t % (2 * m) >= m) & (s % (2 * m) < m))
    mstack = np.concatenate(mats, axis=0).astype(np.float32)
    masks = np.stack(masks, axis=0).astype(np.float32)
    return jnp.asarray(mstack, BF16), jnp.asarray(masks, F32)


def _split3(x):
    x1 = x.astype(BF16)
    r1 = x - x1.astype(F32)
    x2 = r1.astype(BF16)
    x3 = (r1 - x2.astype(F32)).astype(BF16)
    return x1, x2, x3


def _hgrn_kernel(rows, nh, dk, dv, hq_ref, hf_ref, hi_ref, hg_ref, s0_ref, lb_ref, gn_ref,
                 mstack_ref, masks_ref, og_ref, st_ref, r_ref, o_ref, k_ref):
    L = CHUNK
    g = pl.program_id(0)
    blk = rows.blk_in_seq(g)
    is_prompt = g < rows.groups_p

    @pl.when(jnp.logical_and(blk == 0, is_prompt))
    def _():
        st_ref[...] = jnp.zeros_like(st_ref)

    @pl.when(jnp.logical_and(blk == 0, jnp.logical_not(is_prompt)))
    def _():
        for h in range(nh):
            st_ref[0, h] = s0_ref[0, h].T

    def sig(x):
        return 0.5 * jnp.tanh(0.5 * x) + 0.5

    lb = lb_ref[...]
    f = lb + (1.0 - lb) * sig(hf_ref[...])
    lf = jnp.log(f)
    mstack = mstack_ref[...]
    p1, p2, p3 = _split3(lf)
    r_ref[...] = (jnp.dot(mstack, p1, preferred_element_type=F32)
                  + jnp.dot(mstack, p2, preferred_element_type=F32)
                  + jnp.dot(mstack, p3, preferred_element_type=F32))

    k_ref[...] = 1.0 - f

    def head(h, carry):
        cs = pl.ds(pl.multiple_of(h * dk, dk), dk)
        vs = pl.ds(pl.multiple_of(h * dv, dv), dv)
        hq = hq_ref[:, cs]
        q = hq * sig(hq) * (dk ** -0.5)
        k = k_ref[:, cs]
        v = hi_ref[:, vs]
        vb = v.astype(BF16)
        b = r_ref[0:L, cs]
        st = st_ref[0, h]
        qe = (q * jnp.exp(b)).astype(BF16)
        o = lax.dot_general(qe, st.astype(BF16), (((1,), (1,)), ((), ())),
                            preferred_element_type=F32)
        a = jnp.zeros((L, L), F32)
        for li, m in enumerate(HGRN_LEVELS):
            if m in HGRN_MATMUL_LEVELS:
                mi = 1 + HGRN_MATMUL_LEVELS.index(m)
                rl = r_ref[mi * L:(mi + 1) * L, cs]
            else:
                rl = jnp.concatenate(
                    [jnp.broadcast_to(b[s0 + m - 1:s0 + m, :], (2 * m, dk)) for s0 in range(0, L, 2 * m)],
                    axis=0)
            e = jnp.exp(-jnp.abs(b - rl))
            al = lax.dot_general((q * e).astype(BF16), (k * e).astype(BF16),
                                 (((1,), (1,)), ((), ())), preferred_element_type=F32)
            a = a + al * masks_ref[li]
        diag = jnp.sum(q * k, axis=-1, keepdims=True)
        o = o + jnp.dot(a.astype(BF16), vb, preferred_element_type=F32) + diag * v
        o_ref[:, vs] = o
        b_last = b[L - 1:L, :]
        kd = (k * jnp.exp(b_last - b)).astype(BF16)
        st_ref[0, h] = st * jnp.exp(b_last) + lax.dot_general(
            vb, kd, (((0,), (0,)), ((), ())), preferred_element_type=F32)
        return carry

    lax.fori_loop(0, nh, head, 0, unroll=HGRN_HEAD_UNROLL)

    o = o_ref[...]
    ms = jnp.mean(o * o, axis=-1, keepdims=True)
    hg = hg_ref[...]
    og = o * lax.rsqrt(ms + RMS_EPS) * gn_ref[...] * (hg * sig(hg))
    og_ref[...] = og.astype(BF16)


def _hgrn(rows, z, s0, lb, g_hgrn):
    bs, nh, dk, dv = s0.shape
    hw = nh * dk
    assert nh * dv == hw and hw % LANES == 0
    mstack, masks = _hgrn_consts()
    n_lv = len(HGRN_LEVELS)
    L = CHUNK

    def zspec(sec):
        return pl.BlockSpec((L, hw), lambda g, sec=sec: (g, sec))

    og, st = pl.pallas_call(
        functools.partial(_hgrn_kernel, rows, nh, dk, dv),
        out_shape=(jax.ShapeDtypeStruct((rows.rows, hw), BF16),
                   jax.ShapeDtypeStruct((rows.nseq, nh, dv, dk), F32)),
        grid=(rows.groups,),
        in_specs=[zspec(0), zspec(1), zspec(2), zspec(3),
                  pl.BlockSpec((1, nh, dk, dv),
                               lambda g: (jnp.maximum(rows.seq_of_group(g) - rows.bp, 0), 0, 0, 0)),
                  pl.BlockSpec((1, hw), lambda g: (0, 0)),
                  pl.BlockSpec((1, hw), lambda g: (0, 0)),
                  pl.BlockSpec(mstack.shape, lambda g: (0, 0)),
                  pl.BlockSpec(masks.shape, lambda g: (0, 0, 0))],
        out_specs=(pl.BlockSpec((L, hw), lambda g: (g, 0)),
                   pl.BlockSpec((1, nh, dv, dk), lambda g: (rows.seq_of_group(g), 0, 0, 0))),
        scratch_shapes=[pltpu.VMEM(((len(HGRN_MATMUL_LEVELS) + 1) * L, hw), F32),
                        pltpu.VMEM((L, hw), F32),
                        pltpu.VMEM((L, hw), F32)],
        compiler_params=_cparams(1),
        name="hgrn2",
    )(z, z, z, z, s0, lb.reshape(1, hw), g_hgrn.reshape(1, hw), mstack, masks)
    return og, st


def _log_sigmoid(x):
    return jnp.minimum(x, 0.0) - jnp.log(1.0 + jnp.exp(-jnp.abs(x)))


def _mlstm_kernel(rows, nh, dqk, dv, mqk_ref, mv_ref, mo_ref, gates_ref, conv0_ref, c0_ref,
                  n0_ref, m0_ref, cw_ref, cb_ref, gbias_ref, gn_ref,
                  hm_ref, conv_ref, c_ref, n_ref, m_ref, ubuf_ref):
    L = CHUNK
    kq = nh * dqk
    pad = 8
    g = pl.program_id(0)
    blk = rows.blk_in_seq(g)
    is_prompt = g < rows.groups_p

    @pl.when(jnp.logical_and(blk == 0, is_prompt))
    def _():
        ubuf_ref[0:pad, :] = jnp.zeros((pad, 2 * kq), F32)
        c_ref[...] = jnp.zeros_like(c_ref)
        n_ref[...] = jnp.zeros_like(n_ref)
        m_ref[...] = jnp.zeros_like(m_ref)

    @pl.when(jnp.logical_and(blk == 0, jnp.logical_not(is_prompt)))
    def _():
        ubuf_ref[0:pad, :] = jnp.zeros((pad, 2 * kq), F32)
        ubuf_ref[pad - (CONV_K - 1):pad, :] = conv0_ref[0]
        c_ref[...] = c0_ref[...]
        n_ref[...] = n0_ref[...]
        m_ref[...] = m0_ref[...]

    ubuf_ref[pad:pad + L, :] = mqk_ref[...]
    acc = cb_ref[...] + jnp.zeros((L, 2 * kq), F32)
    for j in range(CONV_K):
        off = pad - (CONV_K - 1) + j
        acc = acc + ubuf_ref[off:off + L, :] * cw_ref[j:j + 1, :]
    new_tail = ubuf_ref[pad + L - (CONV_K - 1):pad + L, :]
    conv_ref[0] = new_tail
    ubuf_ref[pad - (CONV_K - 1):pad, :] = new_tail
    qk = _silu(acc)

    gt = gates_ref[...] + gbias_ref[...]
    lane = lax.broadcasted_iota(jnp.int32, gt.shape, 1)
    pg = jnp.where(lane < nh, gt, _log_sigmoid(gt))
    pgt = pg.T
    ti = lax.broadcasted_iota(jnp.int32, (L, L), 0)
    si = lax.broadcasted_iota(jnp.int32, (L, L), 1)
    tri = si <= ti

    for h in range(nh):
        q = qk[:, h * dqk:(h + 1) * dqk]
        k = qk[:, kq + h * dqk:kq + (h + 1) * dqk] * (dqk ** -0.5)
        v = mv_ref[:, h * dv:(h + 1) * dv]
        qb, kb, vb = q.astype(BF16), k.astype(BF16), v.astype(BF16)
        ig_c, lf_c = pg[:, h:h + 1], pg[:, nh + h:nh + h + 1]
        ig_r, lf_r = pgt[h:h + 1, :], pgt[nh + h:nh + h + 1, :]
        b_c = jnp.sum(jnp.where(tri, lf_r, 0.0), axis=1, keepdims=True)
        b_r = jnp.sum(jnp.where(ti <= si, lf_c, 0.0), axis=0, keepdims=True)
        m_prev = m_ref[0, :, h:h + 1]
        log_d = jnp.where(tri, b_c - b_r + ig_r, -jnp.inf)
        log_inter = b_c + m_prev
        m_t = jnp.maximum(jnp.max(log_d, axis=1, keepdims=True), log_inter)
        dm = jnp.exp(log_d - m_t)
        s_mat = lax.dot_general(qb, kb, (((1,), (1,)), ((), ())), preferred_element_type=F32) * dm
        w_inter = jnp.exp(log_inter - m_t)
        c_h = c_ref[0, h]
        n_h = n_ref[0, h:h + 1, :]
        num = (jnp.dot(s_mat.astype(BF16), vb, preferred_element_type=F32)
               + w_inter * jnp.dot(qb, c_h.astype(BF16), preferred_element_type=F32))
        den = (jnp.sum(s_mat, axis=1, keepdims=True)
               + w_inter * jnp.sum(q * n_h, axis=1, keepdims=True))
        hh = num / jnp.maximum(jnp.abs(den), jnp.exp(-m_t))
        b_last = b_c[L - 1:L, :]
        lw_c = b_last - b_c + ig_c
        lw_r = b_last - b_r + ig_r
        m_new = jnp.maximum(b_last + m_prev, jnp.max(lw_r, axis=1, keepdims=True))
        decay = jnp.exp(b_last + m_prev - m_new)
        kw = k * jnp.exp(lw_c - m_new)
        c_ref[0, h] = decay * c_h + lax.dot_general(
            kw.astype(BF16), vb, (((0,), (0,)), ((), ())), preferred_element_type=F32)
        n_ref[0, h:h + 1, :] = decay * n_h + jnp.sum(kw, axis=0, keepdims=True)
        m_ref[0, :, h:h + 1] = m_new
        ms = jnp.mean(hh * hh, axis=-1, keepdims=True)
        hn = hh * lax.rsqrt(ms + RMS_EPS) * gn_ref[:, h * dv:(h + 1) * dv]
        hm_ref[:, h * dv:(h + 1) * dv] = (hn * _sigmoid(mo_ref[:, h * dv:(h + 1) * dv])).astype(BF16)


def _mlstm(rows, z, sec_qk, gates, conv0, c0, n0, m0, conv_w, conv_b, b_igate, b_fgate, g_mlstm):
    bs, nh, dqk, dv = c0.shape
    kq = nh * dqk
    mw = nh * dv
    L = CHUNK
    assert 2 * kq == mw, "q/k conv width must equal the value width for the column sections"
    gbias = jnp.zeros((1, GATE_LANES), F32)
    gbias = gbias.at[0, 0:nh].set(b_igate.astype(F32)).at[0, nh:2 * nh].set(b_fgate.astype(F32))

    def sseq(g):
        return jnp.maximum(rows.seq_of_group(g) - rows.bp, 0)

    outs = pl.pallas_call(
        functools.partial(_mlstm_kernel, rows, nh, dqk, dv),
        out_shape=(jax.ShapeDtypeStruct((rows.rows, mw), BF16),
                   jax.ShapeDtypeStruct((rows.nseq, CONV_K - 1, 2 * kq), F32),
                   jax.ShapeDtypeStruct((rows.nseq, nh, dqk, dv), F32),
                   jax.ShapeDtypeStruct((rows.nseq, nh, dqk), F32),
                   jax.ShapeDtypeStruct((rows.nseq, 1, nh), F32)),
        grid=(rows.groups,),
        in_specs=[pl.BlockSpec((L, mw), lambda g: (g, sec_qk)),
                  pl.BlockSpec((L, mw), lambda g: (g, sec_qk + 1)),
                  pl.BlockSpec((L, mw), lambda g: (g, sec_qk + 2)),
                  pl.BlockSpec((L, GATE_LANES), lambda g: (g, 0)),
                  pl.BlockSpec((1, CONV_K - 1, 2 * kq), lambda g: (sseq(g), 0, 0)),
                  pl.BlockSpec((1, nh, dqk, dv), lambda g: (sseq(g), 0, 0, 0)),
                  pl.BlockSpec((1, nh, dqk), lambda g: (sseq(g), 0, 0)),
                  pl.BlockSpec((1, 1, nh), lambda g: (sseq(g), 0, 0)),
                  pl.BlockSpec((CONV_K, 2 * kq), lambda g: (0, 0)),
                  pl.BlockSpec((1, 2 * kq), lambda g: (0, 0)),
                  pl.BlockSpec((1, GATE_LANES), lambda g: (0, 0)),
                  pl.BlockSpec((1, mw), lambda g: (0, 0))],
        out_specs=(pl.BlockSpec((L, mw), lambda g: (g, 0)),
                   pl.BlockSpec((1, CONV_K - 1, 2 * kq), lambda g: (rows.seq_of_group(g), 0, 0)),
                   pl.BlockSpec((1, nh, dqk, dv), lambda g: (rows.seq_of_group(g), 0, 0, 0)),
                   pl.BlockSpec((1, nh, dqk), lambda g: (rows.seq_of_group(g), 0, 0)),
                   pl.BlockSpec((1, 1, nh), lambda g: (rows.seq_of_group(g), 0, 0))),
        scratch_shapes=[pltpu.VMEM((8 + L, 2 * kq), F32)],
        compiler_params=_cparams(1),
        name="mlstm",
    )(z, z, z, gates, conv0, c0, n0, m0.reshape(bs, 1, nh), conv_w, conv_b.reshape(1, 2 * kq),
      gbias, g_mlstm.reshape(1, mw))
    return outs


def _outproj_kernel(rows, tm, d, hw, og_ref, hm_ref, w_ref, xp_ref, xs_ref, mod_ref, o_ref, wb_ref):
    j, i = pl.program_id(0), pl.program_id(1)
    is_prompt = i < rows.groups_p * CHUNK // tm

    @pl.when(i == 0)
    def _():
        wb_ref[...] = w_ref[...].astype(BF16)

    mix = (jnp.dot(og_ref[...], wb_ref[0:hw, :], preferred_element_type=F32)
           + jnp.dot(hm_ref[...], wb_ref[hw:, :], preferred_element_type=F32))
    for gi in range(tm // CHUNK):
        seq = rows.seq_of_group(i * (tm // CHUNK) + gi)
        gate = mod_ref[pl.ds(seq, 1), :]
        rs = slice(gi * CHUNK, (gi + 1) * CHUNK)
        x = jnp.where(is_prompt, xp_ref[rs, :], xs_ref[rs, :])
        o_ref[rs, :] = x + gate * mix[rs, :]


def _outproj(rows, og, hm, w_out, x_p, x_s, mod, tm=1024, tn=512):
    d = x_p.shape[1]
    r = rows.rows
    hw = og.shape[1]
    tm, tn = min(tm, x_s.shape[0]), min(tn, d)
    assert r % tm == 0 and d % tn == 0
    xp_spec, xs_spec = _split_row_specs(rows, tm, tn, col_major=True)
    gate_blk0 = 2 * d // tn
    return pl.pallas_call(
        functools.partial(_outproj_kernel, rows, tm, d, hw),
        out_shape=jax.ShapeDtypeStruct((r, d), F32),
        grid=(d // tn, r // tm),
        in_specs=[pl.BlockSpec((tm, hw), lambda j, i: (i, 0)),
                  pl.BlockSpec((tm, hm.shape[1]), lambda j, i: (i, 0)),
                  pl.BlockSpec((w_out.shape[0], tn), lambda j, i: (0, j)),
                  xp_spec, xs_spec,
                  pl.BlockSpec((mod.shape[0], tn), lambda j, i: (0, gate_blk0 + j))],
        out_specs=pl.BlockSpec((tm, tn), lambda j, i: (i, j)),
        scratch_shapes=[pltpu.VMEM((w_out.shape[0], tn), BF16)],
        compiler_params=_cparams(2),
        name="out_proj",
    )(og, hm, w_out, x_p, x_s, mod)


def _router_kernel(rows, tm, d, ne, x_ref, g_ref, mod_ref, wr_ref, br_ref, ut_ref,
                   h2_ref, e_ref, gate_ref, rank_ref, cnt_ref, h2s_ref):
    i = pl.program_id(0)
    half = d // 2

    @pl.when(i == 0)
    def _():
        cnt_ref[...] = jnp.zeros_like(cnt_ref)

    g = g_ref[...]
    for gi in range(tm // CHUNK):
        seq = rows.seq_of_group(i * (tm // CHUNK) + gi)
        m = mod_ref[pl.ds(seq, 1), :]
        shift, scale = m[:, 3 * d:4 * d], m[:, 4 * d:5 * d]
        x = x_ref[gi * CHUNK:(gi + 1) * CHUNK, :]
        ms = jnp.mean(x * x, axis=-1, keepdims=True)
        h2s_ref[gi * CHUNK:(gi + 1) * CHUNK, :] = (x * lax.rsqrt(ms + RMS_EPS) * g) * (1.0 + scale) + shift
    h2 = h2s_ref[...]
    lo = pltpu.bitcast(h2[:, :half].astype(BF16).astype(F32), jnp.uint32)
    hi = pltpu.bitcast(h2[:, half:].astype(BF16).astype(F32), jnp.uint32)
    h2_ref[...] = (hi & jnp.uint32(0xFFFF0000)) | (lo >> 16)

    logits = jnp.dot(h2, wr_ref[...], precision=lax.Precision.HIGHEST,
                     preferred_element_type=F32)
    lt = logits.T[0:ne, :] + br_ref[...]
    eidx = lax.broadcasted_iota(jnp.int32, (ne, tm), 0)
    cur = lt
    tops, sels, hots = [], [], []
    for _ in range(TOP_K):
        mx = jnp.max(cur, axis=0, keepdims=True)
        sel = jnp.min(jnp.where(cur == mx, eidx, ne), axis=0, keepdims=True)
        hot = eidx == sel
        tops.append(mx)
        sels.append(sel)
        hots.append(hot)
        cur = jnp.where(hot, -jnp.inf, cur)
    ex = [jnp.exp(t - tops[0]) for t in tops]
    tot = ex[0] + ex[1] + ex[2] + ex[3]
    oh = jnp.zeros((ne, tm), F32)
    for hot in hots:
        oh = oh + hot.astype(F32)
    prefix = jnp.dot(oh.astype(BF16), ut_ref[...], preferred_element_type=F32)
    base = cnt_ref[:, 0:1]
    pos = base + prefix
    for kk in range(TOP_K):
        e_ref[kk:kk + 1, :] = sels[kk]
        gate_ref[kk:kk + 1, :] = ex[kk] / tot
        rank_ref[kk:kk + 1, :] = jnp.sum(jnp.where(hots[kk], pos, 0.0), axis=0,
                                         keepdims=True).astype(jnp.int32)
    cnt_ref[...] = cnt_ref[...] + jnp.sum(oh, axis=1, keepdims=True)


def _router(rows, x1, g_ffn, mod, w_router, b_router, tm=256):
    r, d = x1.shape
    ne = w_router.shape[1]
    tm = min(tm, r)
    assert r % tm == 0 and ne % 8 == 0 and ne <= LANES
    wr_pad = jnp.zeros((d, LANES), F32).at[:, :ne].set(w_router.astype(F32))
    tt = np.arange(tm)
    ut = jnp.asarray((tt[:, None] < tt[None, :]).astype(np.float32), BF16)
    return pl.pallas_call(
        functools.partial(_router_kernel, rows, tm, d, ne),
        out_shape=(jax.ShapeDtypeStruct((r, d // 2), jnp.uint32),
                   jax.ShapeDtypeStruct((TOP_K, r), jnp.int32),
                   jax.ShapeDtypeStruct((TOP_K, r), F32),
                   jax.ShapeDtypeStruct((TOP_K, r), jnp.int32),
                   jax.ShapeDtypeStruct((ne, LANES), F32)),
        grid=(r // tm,),
        in_specs=[pl.BlockSpec((tm, d), lambda i: (i, 0)),
                  pl.BlockSpec((1, d), lambda i: (0, 0)),
                  pl.BlockSpec(mod.shape, lambda i: (0, 0)),
                  pl.BlockSpec((d, LANES), lambda i: (0, 0)),
                  pl.BlockSpec((ne, 1), lambda i: (0, 0)),
                  pl.BlockSpec((tm, tm), lambda i: (0, 0))],
        out_specs=(pl.BlockSpec((tm, d // 2), lambda i: (i, 0)),
                   pl.BlockSpec((TOP_K, tm), lambda i: (0, i)),
                   pl.BlockSpec((TOP_K, tm), lambda i: (0, i)),
                   pl.BlockSpec((TOP_K, tm), lambda i: (0, i)),
                   pl.BlockSpec((ne, LANES), lambda i: (0, 0))),
        scratch_shapes=[pltpu.VMEM((tm, d), F32)],
        compiler_params=_cparams(1),
        name="router",
    )(x1, g_ffn.reshape(1, d), mod, wr_pad, b_router.reshape(ne, 1).astype(F32), ut)


def _invert_kernel(tm, pstart_ref, e_ref, rank_ref, rt_ref):
    i = pl.program_id(0)

    @pl.when(i == 0)
    def _():
        def clear(r, carry):
            rt_ref[r] = 0
            return carry

        lax.fori_loop(0, rt_ref.shape[0], clear, 0, unroll=8)

    def body(t, carry):
        for kk in range(TOP_K):
            rt_ref[pstart_ref[e_ref[kk, t]] + rank_ref[kk, t]] = i * tm + t
        return carry

    lax.fori_loop(0, tm, body, 0, unroll=4)


def _invert(e_idx, rank, pstart, n_rows, tm=256):
    r = e_idx.shape[1]
    tm = min(tm, r)
    assert r % tm == 0
    return pl.pallas_call(
        functools.partial(_invert_kernel, tm),
        out_shape=jax.ShapeDtypeStruct((n_rows,), jnp.int32),
        grid_spec=pltpu.PrefetchScalarGridSpec(
            num_scalar_prefetch=1,
            grid=(r // tm,),
            in_specs=[pl.BlockSpec((TOP_K, tm), lambda i, *_: (0, i), memory_space=pltpu.SMEM),
                      pl.BlockSpec((TOP_K, tm), lambda i, *_: (0, i), memory_space=pltpu.SMEM)],
            out_specs=pl.BlockSpec(memory_space=pltpu.SMEM)),
        compiler_params=_cparams(1),
        name="moe_invert",
    )(pstart, e_idx, rank)


def _unpack_rows(xw):
    lo = pltpu.bitcast(xw << 16, F32).astype(BF16)
    hi = pltpu.bitcast(xw & jnp.uint32(0xFFFF0000), F32).astype(BF16)
    return lo, hi


def _gather_kernel(nused_ref, rt_cur_ref, rt_next_ref, h2_ref, xs_ref, buf_ref, sem):
    b = pl.program_id(0)
    n_used = nused_ref[0]
    half = buf_ref.shape[2]

    def row_copy(slot, src_row, dst_row):
        return pltpu.make_async_copy(h2_ref.at[pl.ds(src_row, 1), :],
                                     buf_ref.at[slot, pl.ds(dst_row, 1), :], sem.at[slot])

    def issue(rt_ref, slot):
        def pair(t2, carry):
            for p in range(2):
                t = 2 * t2 + p
                row_copy(slot, rt_ref[0, 0, t], t).start(priority=p)
            return carry

        lax.fori_loop(0, MOE_BLOCK // 2, pair, 0, unroll=4)

    @pl.when(jnp.logical_and(b == 0, n_used > 0))
    def _():
        issue(rt_cur_ref, 0)

    @pl.when(b + 1 < n_used)
    def _():
        issue(rt_next_ref, (b + 1) % 2)

    @pl.when(b < n_used)
    def _():
        slot = b % 2

        def drain(t, carry):
            row_copy(slot, 0, 0).wait()
            return carry

        lax.fori_loop(0, MOE_BLOCK, drain, 0, unroll=8)
        lo, hi = _unpack_rows(buf_ref[slot])
        xs_ref[:, 0:half] = lo
        xs_ref[:, half:] = hi

    @pl.when(b >= n_used)
    def _():
        xs_ref[...] = jnp.zeros_like(xs_ref)


def _gather_rows(h2p, row_tok, n_used):
    r, half = h2p.shape
    n_blocks = row_tok.shape[0] // MOE_BLOCK
    rt3 = row_tok.reshape(n_blocks, 1, MOE_BLOCK)
    return pl.pallas_call(
        _gather_kernel,
        out_shape=jax.ShapeDtypeStruct((n_blocks * MOE_BLOCK, 2 * half), BF16),
        grid_spec=pltpu.PrefetchScalarGridSpec(
            num_scalar_prefetch=1,
            grid=(n_blocks,),
            in_specs=[pl.BlockSpec((1, 1, MOE_BLOCK), lambda b, *_: (b, 0, 0), memory_space=pltpu.SMEM),
                      pl.BlockSpec((1, 1, MOE_BLOCK), lambda b, *_: (jnp.minimum(b + 1, n_blocks - 1), 0, 0),
                                   memory_space=pltpu.SMEM),
                      pl.BlockSpec(memory_space=pl.ANY)],
            out_specs=pl.BlockSpec((MOE_BLOCK, 2 * half), lambda b, *_: (b, 0)),
            scratch_shapes=[pltpu.VMEM((2, MOE_BLOCK, half), jnp.uint32),
                            pltpu.SemaphoreType.DMA((2,))]),
        compiler_params=_cparams(1),
        name="moe_gather",
    )(n_used.reshape(1), rt3, rt3, h2p)


MOE_CACHE_BLOCKS = 4


class _ExpertRows:
    def __init__(self, src_hbm, slots_ref, sems, b0, nb, first_tile):
        self.src, self.slots, self.sems = src_hbm, slots_ref, sems
        self.b0, self.nb, self.first_tile = b0, nb, first_tile

    def slot_of(self, i):
        return jnp.where(i < MOE_CACHE_BLOCKS, i, MOE_CACHE_BLOCKS + i % 2)

    def _needs_load(self, i):
        return jnp.logical_or(self.first_tile, i >= MOE_CACHE_BLOCKS)

    def _copy(self, i):
        row0 = pl.multiple_of((self.b0 + i) * MOE_BLOCK, MOE_BLOCK)
        return pltpu.make_async_copy(self.src.at[pl.ds(row0, MOE_BLOCK), :],
                                     self.slots.at[self.slot_of(i)], self.sems.at[i % 2])

    def request(self, i):
        @pl.when(jnp.logical_and(i < self.nb, self._needs_load(i)))
        def _():
            self._copy(i).start()

    def arrive(self, i):
        @pl.when(self._needs_load(i))
        def _():
            self._copy(i).wait()

    def request_first(self, e):
        @pl.when(e == 0)
        def _():
            self.request(0)

    def request_next_expert(self, e, ne, last_tile, bstart_ref, nblk_ref):
        @pl.when(jnp.logical_and(last_tile, e + 1 < ne))
        def _():
            @pl.when(nblk_ref[e + 1] > 0)
            def _():
                row0 = pl.multiple_of(bstart_ref[e + 1] * MOE_BLOCK, MOE_BLOCK)
                pltpu.make_async_copy(self.src.at[pl.ds(row0, MOE_BLOCK), :],
                                      self.slots.at[0], self.sems.at[0]).start()


def _block_window(dst_hbm, blk, col0, width):
    row0 = pl.multiple_of(blk * MOE_BLOCK, MOE_BLOCK)
    return dst_hbm.at[pl.ds(row0, MOE_BLOCK), pl.ds(pl.multiple_of(col0, width), width)]


def _zero_tail_blocks(stage_ref, dst_hbm, sem, first_blk, col0, width):
    n_blocks = dst_hbm.shape[0] // MOE_BLOCK
    stage_ref[0] = jnp.zeros(stage_ref.shape[1:], stage_ref.dtype)

    def cp(blk):
        return pltpu.make_async_copy(stage_ref.at[0], _block_window(dst_hbm, blk, col0, width), sem.at[0])

    def start(blk, carry):
        cp(blk).start()
        return carry

    def wait(blk, carry):
        cp(blk).wait()
        return carry

    lax.fori_loop(first_blk, n_blocks, start, 0)
    lax.fori_loop(first_blk, n_blocks, wait, 0)


class _OutRing:
    def __init__(self, stage_ref, sems, cnt_ref, dst_hbm, width):
        self.stage, self.sems, self.cnt, self.dst, self.width = stage_ref, sems, cnt_ref, dst_hbm, width

    def init(self):
        self.cnt[0] = 0
        self.cnt[1] = 0

    def _await_next(self, limit):
        w = self.cnt[1]

        @pl.when(w < limit)
        def _():
            pltpu.make_async_copy(self.stage.at[w % 2], _block_window(self.dst, 0, 0, self.width),
                                  self.sems.at[w % 2]).wait()
            self.cnt[1] = w + 1

    def reserve(self):
        self._await_next(self.cnt[0] - 1)

    def push(self, value, blk, col0):
        c = self.cnt[0]
        self.stage[c % 2] = value
        pltpu.make_async_copy(self.stage.at[c % 2], _block_window(self.dst, blk, col0, self.width),
                              self.sems.at[c % 2]).start()
        self.cnt[0] = c + 1

    def drain(self):
        for _ in range(2):
            self._await_next(self.cnt[0])


MOE_CAST_CHUNKS = 4


def _cast_chunk(w_ref, wbf_ref, c):
    kc = wbf_ref.shape[0] // MOE_CAST_CHUNKS
    wbf_ref[c * kc:(c + 1) * kc, :] = w_ref[0, c * kc:(c + 1) * kc, :].astype(BF16)


def _first_block_dot(x_blk_ref, w_ref, wbf_ref):
    kc = wbf_ref.shape[0] // MOE_CAST_CHUNKS
    acc = None
    for c in range(MOE_CAST_CHUNKS):
        if c + 1 < MOE_CAST_CHUNKS:
            _cast_chunk(w_ref, wbf_ref, c + 1)
        part = jnp.dot(x_blk_ref[:, c * kc:(c + 1) * kc], wbf_ref[c * kc:(c + 1) * kc, :],
                       preferred_element_type=F32)
        acc = part if acc is None else acc + part
    return acc


def _moe_up_kernel(tf, nj, ne, bstart_ref, nblk_ref, xs_hbm, w_ref, b_ref, act_hbm,
                   wbf_ref, xb_ref, gt_ref, ostage_ref, gu_carry_ref, cnt_ref, xsem, osem):
    g = pl.program_id(0)
    e, j = g // nj, g % nj
    nb, b0 = nblk_ref[e], bstart_ref[e]
    nslab = MOE_BLOCK // LANES
    xrows = _ExpertRows(xs_hbm, xb_ref, xsem, b0, nb, j == 0)
    ring = _OutRing(ostage_ref, osem, cnt_ref, act_hbm, tf)
    PEND, PEND_BLK, PEND_COL = 2, 3, 4

    @pl.when(g == 0)
    def _():
        ring.init()
        cnt_ref[PEND] = 0
        gu_carry_ref[...] = jnp.zeros_like(gu_carry_ref)

    def up_dot(i):
        return jnp.dot(xb_ref[xrows.slot_of(i)], wbf_ref[...], preferred_element_type=F32) + b_ref[0]

    def swiglu(gu):
        gut = gu.T
        for sl in range(nslab):
            gt_ref[sl] = gut[:, sl * LANES:(sl + 1) * LANES]
        parts = []
        for sl in range(nslab):
            gate = jnp.minimum(gt_ref[sl, pl.ds(0, tf, stride=2), :], SWIGLU_LIMIT)
            up = jnp.clip(gt_ref[sl, pl.ds(1, tf, stride=2), :], -SWIGLU_LIMIT, SWIGLU_LIMIT)
            parts.append((up + 1.0) * gate * _sigmoid(SWIGLU_ALPHA * gate))
        return jnp.concatenate(parts, axis=1).T.astype(BF16)

    @pl.when(nb > 0)
    def _():
        xrows.request_first(e)
        _cast_chunk(w_ref, wbf_ref, 0)
        xrows.arrive(0)
        xrows.request(1)
        ring.reserve()
        owed = cnt_ref[PEND] == 1
        gu0 = _first_block_dot(xb_ref.at[0], w_ref, wbf_ref) + b_ref[0]
        act_owed = swiglu(gu_carry_ref[...])

        @pl.when(owed)
        def _():
            ring.push(act_owed, cnt_ref[PEND_BLK], cnt_ref[PEND_COL])
            cnt_ref[PEND] = 0

        def body(i, gu_prev):
            xrows.arrive(i)
            xrows.request(i + 1)
            ring.reserve()
            gu = up_dot(i)
            ring.push(swiglu(gu_prev), b0 + i - 1, j * tf)
            return gu

        gu_last = lax.fori_loop(1, nb, body, gu0)

        @pl.when(e < ne - 1)
        def _():
            gu_carry_ref[...] = gu_last
            cnt_ref[PEND] = 1
            cnt_ref[PEND_BLK] = b0 + nb - 1
            cnt_ref[PEND_COL] = j * tf

        @pl.when(e == ne - 1)
        def _():
            ring.reserve()
            ring.push(swiglu(gu_last), b0 + nb - 1, j * tf)

    xrows.request_next_expert(e, ne, j == nj - 1, bstart_ref, nblk_ref)

    @pl.when(e == ne - 1)
    def _():
        @pl.when(cnt_ref[PEND] == 1)
        def _():
            ring.reserve()
            ring.push(swiglu(gu_carry_ref[...]), cnt_ref[PEND_BLK], cnt_ref[PEND_COL])
            cnt_ref[PEND] = 0

        ring.drain()
        _zero_tail_blocks(ostage_ref, act_hbm, osem, b0 + nb, j * tf, tf)


def _moe_up(xs, w_gu, b_gu, bstart, nblk, tf=512):
    n_rows, d = xs.shape
    ne, _, f2 = w_gu.shape
    f = f2 // 2
    tf = min(tf, f)
    assert f % tf == 0 and w_gu.shape[1] == d
    nj = f // tf
    return pl.pallas_call(
        functools.partial(_moe_up_kernel, tf, nj, ne),
        out_shape=jax.ShapeDtypeStruct((n_rows, f), BF16),
        grid_spec=pltpu.PrefetchScalarGridSpec(
            num_scalar_prefetch=2,
            grid=(ne * nj,),
            in_specs=[pl.BlockSpec(memory_space=pl.ANY),
                      pl.BlockSpec((1, d, 2 * tf), lambda g, *_: (g // nj, 0, g % nj)),
                      pl.BlockSpec((1, 1, 2 * tf), lambda g, *_: (g // nj, 0, g % nj))],
            out_specs=pl.BlockSpec(memory_space=pl.ANY),
            scratch_shapes=[pltpu.VMEM((d, 2 * tf), BF16),
                            pltpu.VMEM((MOE_CACHE_BLOCKS + 2, MOE_BLOCK, d), BF16),
                            pltpu.VMEM((MOE_BLOCK // LANES, 2 * tf, LANES), F32),
                            pltpu.VMEM((2, MOE_BLOCK, tf), BF16),
                            pltpu.VMEM((MOE_BLOCK, 2 * tf), F32),
                            pltpu.SMEM((8,), jnp.int32),
                            pltpu.SemaphoreType.DMA((2,)),
                            pltpu.SemaphoreType.DMA((2,))]),
        compiler_params=_cparams(1),
        name="moe_up",
    )(bstart, nblk, xs, w_gu, b_gu.reshape(ne, 1, f2))


def _moe_down_kernel(tn, nj, ne, bstart_ref, nblk_ref, act_hbm, w_ref, b_ref, yb_hbm,
                     wbf_ref, ab_ref, ostage_ref, cnt_ref, asem, osem):
    g = pl.program_id(0)
    e, j = g // nj, g % nj
    nb, b0 = nblk_ref[e], bstart_ref[e]
    arows = _ExpertRows(act_hbm, ab_ref, asem, b0, nb, j == 0)
    ring = _OutRing(ostage_ref, osem, cnt_ref, yb_hbm, tn)

    @pl.when(g == 0)
    def _():
        ring.init()

    @pl.when(nb > 0)
    def _():
        arows.request_first(e)
        _cast_chunk(w_ref, wbf_ref, 0)
        arows.arrive(0)
        arows.request(1)
        ring.reserve()
        ring.push(_first_block_dot(ab_ref.at[0], w_ref, wbf_ref) + b_ref[0], b0, j * tn)

        def body(i, carry):
            arows.arrive(i)
            arows.request(i + 1)
            ring.reserve()
            y = jnp.dot(ab_ref[arows.slot_of(i)], wbf_ref[...], preferred_element_type=F32) + b_ref[0]
            ring.push(y, b0 + i, j * tn)
            return carry

        lax.fori_loop(1, nb, body, 0)

    arows.request_next_expert(e, ne, j == nj - 1, bstart_ref, nblk_ref)

    @pl.when(e == ne - 1)
    def _():
        ring.drain()
        _zero_tail_blocks(ostage_ref, yb_hbm, osem, b0 + nb, j * tn, tn)


def _moe_down(act, w_d, b_d, bstart, nblk, tn=1024):
    n_rows, f = act.shape
    ne, _, d = w_d.shape
    tn = min(tn, d)
    assert d % tn == 0
    nj = d // tn
    return pl.pallas_call(
        functools.partial(_moe_down_kernel, tn, nj, ne),
        out_shape=jax.ShapeDtypeStruct((n_rows, d), F32),
        grid_spec=pltpu.PrefetchScalarGridSpec(
            num_scalar_prefetch=2,
            grid=(ne * nj,),
            in_specs=[pl.BlockSpec(memory_space=pl.ANY),
                      pl.BlockSpec((1, f, tn), lambda g, *_: (g // nj, 0, g % nj)),
                      pl.BlockSpec((1, 1, tn), lambda g, *_: (g // nj, 0, g % nj))],
            out_specs=pl.BlockSpec(memory_space=pl.ANY),
            scratch_shapes=[pltpu.VMEM((f, tn), BF16),
                            pltpu.VMEM((MOE_CACHE_BLOCKS + 2, MOE_BLOCK, f), BF16),
                            pltpu.VMEM((2, MOE_BLOCK, tn), F32),
                            pltpu.SMEM((2,), jnp.int32),
                            pltpu.SemaphoreType.DMA((2,)),
                            pltpu.SemaphoreType.DMA((2,))]),
        compiler_params=_cparams(1),
        name="moe_down",
    )(bstart, nblk, act, w_d, b_d.reshape(ne, 1, d))


def _combine_kernel(rows, tm, d, n_tiles, pstart_ref, e_ref, rank_ref, en_ref, rankn_ref, gt_ref,
                    x_ref, mod_ref, gf_ref, yb_ref, yp_ref, ys_ref, buf_ref, sem):
    i = pl.program_id(0)
    is_prompt = i < rows.groups_p * CHUNK // tm
    slot = i % 2

    def row_copy(s, dst_k, dst_t, src_row):
        return pltpu.make_async_copy(yb_ref.at[pl.ds(src_row, 1), :],
                                     buf_ref.at[s, dst_k, pl.ds(dst_t, 1), :], sem.at[s])

    def issue(eref, rref, s):
        def one(t, carry):
            for kk in range(TOP_K):
                src = pstart_ref[eref[kk, t]] + rref[kk, t]
                row_copy(s, kk, t, src).start(priority=kk % 2)
            return carry

        lax.fori_loop(0, tm, one, 0, unroll=2)

    @pl.when(i == 0)
    def _():
        issue(e_ref, rank_ref, 0)

    @pl.when(i + 1 < n_tiles)
    def _():
        issue(en_ref, rankn_ref, (i + 1) % 2)

    def drain(t, carry):
        for kk in range(TOP_K):
            row_copy(slot, 0, 0, 0).wait()
        return carry

    lax.fori_loop(0, tm, drain, 0, unroll=2)

    gf = gf_ref[...]
    for gi in range(tm // CHUNK):
        rs = slice(gi * CHUNK, (gi + 1) * CHUNK)
        seq = rows.seq_of_group(i * (tm // CHUNK) + gi)
        gate2 = mod_ref[pl.ds(seq, 1), :]
        ff = jnp.zeros((CHUNK, d), F32)
        for kk in range(TOP_K):
            ff = ff + buf_ref[slot, kk, rs, :] * gt_ref[rs, kk:kk + 1]
        x2 = x_ref[rs, :] + gate2 * ff
        ms = jnp.mean(x2 * x2, axis=-1, keepdims=True)
        y = x2 * lax.rsqrt(ms + RMS_EPS) * gf

        @pl.when(is_prompt)
        def _():
            yp_ref[rs, :] = y

        @pl.when(jnp.logical_not(is_prompt))
        def _():
            ys_ref[rs, :] = y


def _combine(rows, x1, mod, g_final, yb, e_idx, rank, gates_t, pstart, tm=128):
    r, d = x1.shape
    tm = min(tm, r)
    assert r % tm == 0
    gate2_blk = 5
    yp_spec, ys_spec = _split_row_specs(rows, tm, d)
    n_tiles = r // tm

    def nxt(i, *_):
        return (0, jnp.minimum(i + 1, n_tiles - 1))

    return pl.pallas_call(
        functools.partial(_combine_kernel, rows, tm, d, n_tiles),
        out_shape=(jax.ShapeDtypeStruct((rows.groups_p * CHUNK, d), F32),
                   jax.ShapeDtypeStruct((r - rows.groups_p * CHUNK, d), F32)),
        grid_spec=pltpu.PrefetchScalarGridSpec(
            num_scalar_prefetch=1,
            grid=(n_tiles,),
            in_specs=[pl.BlockSpec((TOP_K, tm), lambda i, *_: (0, i), memory_space=pltpu.SMEM),
                      pl.BlockSpec((TOP_K, tm), lambda i, *_: (0, i), memory_space=pltpu.SMEM),
                      pl.BlockSpec((TOP_K, tm), nxt, memory_space=pltpu.SMEM),
                      pl.BlockSpec((TOP_K, tm), nxt, memory_space=pltpu.SMEM),
                      pl.BlockSpec((tm, TOP_K), lambda i, *_: (i, 0)),
                      pl.BlockSpec((tm, d), lambda i, *_: (i, 0)),
                      pl.BlockSpec((mod.shape[0], d), lambda i, *_: (0, gate2_blk)),
                      pl.BlockSpec((1, d), lambda i, *_: (0, 0)),
                      pl.BlockSpec(memory_space=pl.ANY)],
            out_specs=(yp_spec, ys_spec),
            scratch_shapes=[pltpu.VMEM((2, TOP_K, tm, d), F32),
                            pltpu.SemaphoreType.DMA((2,))]),
        compiler_params=_cparams(1),
        name="moe_combine",
    )(pstart, e_idx, rank, e_idx, rank, gates_t, x1, mod, g_final.reshape(1, d), yb)


def kernel(x_prompt, x_sample, c_prompt, c_sample, state_hgrn_S, state_conv, state_mlstm_C,
           state_mlstm_n, state_mlstm_m, w_ada, b_ada, g_mix, g_ffn, w_in, lb_logits, conv_w,
           conv_b, b_igate, b_fgate, g_hgrn, g_mlstm, w_out, w_router, b_router, w_gate_up,
           b_gate_up, w_down, b_down, g_final):
    depth = w_ada.shape[0]
    assert depth == 1, "single-layer trunk"
    bp, tp, d = x_prompt.shape
    bs, ts, _ = x_sample.shape
    rows = _Rows(bp, tp, bs, ts)
    _, _, nh_h, dk, dv = state_hgrn_S.shape
    _, _, nh_m, dqk, dvm = state_mlstm_C.shape
    hw = nh_h * dk
    mw = nh_m * dvm
    n_main = 4 * hw + 3 * mw
    assert w_in.shape[2] == n_main + 2 * nh_m and 2 * nh_m <= GATE_LANES
    assert mw == hw, "column sections of the input projection are addressed in hw-wide blocks"
    ne = w_router.shape[2]

    lb = jax.nn.softmax(lb_logits.astype(F32), axis=0)[0]

    x_p = x_prompt.reshape(bp * tp, d)
    x_s = x_sample.reshape(bs * ts, d)
    n_c = bp + bs
    n_c_pad = -(-n_c // 8) * 8
    c_pad = jnp.zeros((n_c_pad, d), F32).at[:n_c].set(jnp.concatenate([c_prompt, c_sample], axis=0))
    mod = _ada(c_pad, w_ada[0], b_ada[0])

    w_in_t = jnp.swapaxes(w_in, 1, 2)
    h_all, gates = _prep(rows, x_p, x_s, g_mix[0], mod, w_in_t, n_main)
    z = _inproj(h_all, w_in_t, n_main)

    og, st = _hgrn(rows, z, state_hgrn_S[0], lb, g_hgrn[0])
    hm, conv_new, c_new, n_new, m_new = _mlstm(
        rows, z, 4, gates, state_conv[0], state_mlstm_C[0], state_mlstm_n[0], state_mlstm_m[0],
        conv_w[0], conv_b[0], b_igate[0], b_fgate[0], g_mlstm[0])

    x1 = _outproj(rows, og, hm, w_out[0], x_p, x_s, mod)

    h2p, e_idx, gate_k, rank, cnt = _router(rows, x1, g_ffn[0], mod, w_router[0], b_router[0])

    counts = cnt[:, 0].astype(jnp.int32)
    nblk_e = (counts + MOE_BLOCK - 1) // MOE_BLOCK
    padded = nblk_e * MOE_BLOCK
    pad_end = jnp.cumsum(padded)
    pstart = jnp.concatenate([jnp.zeros((1,), jnp.int32), pad_end]).astype(jnp.int32)
    n_blocks_max = -(-(rows.rows * TOP_K) // MOE_BLOCK) + ne
    n_rows = n_blocks_max * MOE_BLOCK
    blk_start_e = (pstart[:ne] // MOE_BLOCK).astype(jnp.int32)
    n_used = (pstart[ne] // MOE_BLOCK).astype(jnp.int32)

    row_tok = _invert(e_idx, rank, pstart, n_rows)
    xs = _gather_rows(h2p, row_tok, n_used)
    act = _moe_up(xs, w_gate_up[0], b_gate_up[0], blk_start_e, nblk_e.astype(jnp.int32))
    yb = _moe_down(act, w_down[0], b_down[0], blk_start_e, nblk_e.astype(jnp.int32))

    y_p, y_s = _combine(rows, x1, mod, g_final, yb, e_idx, rank, gate_k.T, pstart)

    y_prompt = y_p.reshape(bp, tp, d)
    y_sample = y_s.reshape(bs, ts, d)
    s_all = jnp.swapaxes(st, 2, 3)
    m_all = m_new.reshape(rows.nseq, nh_m)

    def split(a):
        return a[:bp][None], a[bp:][None]

    p_s, s_s = split(s_all)
    p_conv, s_conv = split(conv_new)
    p_c, s_c = split(c_new)
    p_n, s_n = split(n_new)
    p_m, s_m = split(m_all)
    return (y_prompt, y_sample, p_s, p_conv, p_c, p_n, p_m, s_s, s_conv, s_c, s_n, s_m)
```

```python
import functools

import numpy as np
import jax
import jax.numpy as jnp
from jax import lax
from jax.experimental import pallas as pl
from jax.experimental.pallas import tpu as pltpu

F32 = jnp.float32
BF16 = jnp.bfloat16

CHUNK = 64
TOP_K = 4
CONV_K = 4
RMS_EPS = 1e-6
SWIGLU_LIMIT = 7.0
SWIGLU_ALPHA = 1.702
MOE_BLOCK = 256
LANES = 128
GATE_LANES = 128
VMEM_LIMIT = 60 * 1024 * 1024
HGRN_LEVELS = (32, 16, 8, 4, 2, 1)
HGRN_MATMUL_LEVELS = (2, 1)
HGRN_HEAD_UNROLL = 8


def _cparams(n_axes, vmem=VMEM_LIMIT):
    return pltpu.CompilerParams(dimension_semantics=("arbitrary",) * n_axes,
                                vmem_limit_bytes=vmem)


def _sigmoid(x):
    return 1.0 / (1.0 + jnp.exp(-x))


def _silu(x):
    return x * _sigmoid(x)


def _dot_f32x3(a, b):
    a_hi = a.astype(BF16)
    a_lo = (a - a_hi.astype(F32)).astype(BF16)
    b_hi = b.astype(BF16)
    b_lo = (b - b_hi.astype(F32)).astype(BF16)
    return (jnp.dot(a_hi, b_hi, preferred_element_type=F32)
            + jnp.dot(a_hi, b_lo, preferred_element_type=F32)
            + jnp.dot(a_lo, b_hi, preferred_element_type=F32))


class _Rows:
    def __init__(self, bp, tp, bs, ts):
        assert tp % CHUNK == 0 and ts % CHUNK == 0
        self.bp, self.tp, self.bs, self.ts = bp, tp, bs, ts
        self.nblk_p, self.nblk_s = tp // CHUNK, ts // CHUNK
        self.groups_p = bp * self.nblk_p
        self.groups = self.groups_p + bs * self.nblk_s
        self.rows = self.groups * CHUNK
        self.nseq = bp + bs

    def seq_of_group(self, g):
        return jnp.where(g < self.groups_p, g // self.nblk_p,
                         self.bp + (g - self.groups_p) // self.nblk_s)

    def blk_in_seq(self, g):
        return jnp.where(g < self.groups_p, g % self.nblk_p, (g - self.groups_p) % self.nblk_s)

    def nblk_of_group(self, g):
        return jnp.where(g < self.groups_p, self.nblk_p, self.nblk_s)


def _ada_kernel(c_ref, w_ref, b_ref, o_ref):
    c = c_ref[...]
    s = _silu(c).astype(BF16)
    o_ref[...] = jnp.dot(s, w_ref[...].astype(BF16), preferred_element_type=F32) + b_ref[...]


def _ada(c_pad, w_ada, b_ada, tn=512):
    m, d = c_pad.shape
    n = w_ada.shape[1]
    tn = min(tn, n)
    assert n % tn == 0
    return pl.pallas_call(
        _ada_kernel,
        out_shape=jax.ShapeDtypeStruct((m, n), F32),
        grid=(n // tn,),
        in_specs=[pl.BlockSpec((m, d), lambda j: (0, 0)),
                  pl.BlockSpec((d, tn), lambda j: (0, j)),
                  pl.BlockSpec((1, tn), lambda j: (0, j))],
        out_specs=pl.BlockSpec((m, tn), lambda j: (0, j)),
        compiler_params=_cparams(1),
        name="ada_mod",
    )(c_pad, w_ada, b_ada.reshape(1, n))


def _prep_kernel(rows, tm, d, n_gate, xp_ref, xs_ref, g_ref, mod_ref, wtail_ref,
                 h_ref, gates_ref, wg_ref, hf_ref):
    i = pl.program_id(0)
    is_prompt = i < rows.groups_p * CHUNK // tm

    @pl.when(i == 0)
    def _():
        lanes = wg_ref.shape[1]
        padded = jnp.concatenate([wtail_ref[0], jnp.zeros((lanes - n_gate, d), F32)], axis=0)
        wg_ref[...] = padded.T

    g = g_ref[...]
    for gi in range(tm // CHUNK):
        seq = rows.seq_of_group(i * (tm // CHUNK) + gi)
        m = mod_ref[pl.ds(seq, 1), :]
        shift, scale = m[:, 0:d], m[:, d:2 * d]
        rs = slice(gi * CHUNK, (gi + 1) * CHUNK)
        x = jnp.where(is_prompt, xp_ref[rs, :], xs_ref[rs, :])
        ms = jnp.mean(x * x, axis=-1, keepdims=True)
        y = x * lax.rsqrt(ms + RMS_EPS) * g
        h = y * (1.0 + scale) + shift
        h_ref[rs, :] = h.astype(BF16)
        hf_ref[rs, :] = h
    gates_ref[...] = _dot_f32x3(hf_ref[...], wg_ref[...])


def _split_row_specs(rows, tm, tn, col_major=False):
    n_pt = rows.groups_p * CHUNK // tm
    n_st = rows.rows // tm - n_pt
    assert n_pt * tm == rows.groups_p * CHUNK and n_st >= 1

    def ij(args):
        return (args[1], args[0]) if col_major else (args[0], 0)

    def p_map(*args):
        i, j = ij(args)
        return (jnp.minimum(i, n_pt - 1), j)

    def s_map(*args):
        i, j = ij(args)
        return (jnp.maximum(i - n_pt, 0), j)

    return pl.BlockSpec((tm, tn), p_map), pl.BlockSpec((tm, tn), s_map)


def _prep(rows, x_p, x_s, g_mix, mod, w_in_t, n_main, tm=256):
    d = x_p.shape[1]
    r = rows.rows
    tm = min(tm, x_s.shape[0])
    n_gate = w_in_t.shape[1] - n_main
    assert r % tm == 0 and n_gate % 8 == 0 and n_main % n_gate == 0 and n_gate <= GATE_LANES
    xp_spec, xs_spec = _split_row_specs(rows, tm, d)
    return pl.pallas_call(
        functools.partial(_prep_kernel, rows, tm, d, n_gate),
        out_shape=(jax.ShapeDtypeStruct((r, d), BF16),
                   jax.ShapeDtypeStruct((r, GATE_LANES), F32)),
        grid=(r // tm,),
        in_specs=[xp_spec, xs_spec,
                  pl.BlockSpec((1, d), lambda i: (0, 0)),
                  pl.BlockSpec(mod.shape, lambda i: (0, 0)),
                  pl.BlockSpec((1, n_gate, d), lambda i: (0, n_main // n_gate, 0))],
        out_specs=(pl.BlockSpec((tm, d), lambda i: (i, 0)),
                   pl.BlockSpec((tm, GATE_LANES), lambda i: (i, 0))),
        scratch_shapes=[pltpu.VMEM((d, GATE_LANES), F32),
                        pltpu.VMEM((tm, d), F32)],
        compiler_params=_cparams(1),
        name="prep_norm_mod",
    )(x_p, x_s, g_mix.reshape(1, d), mod, w_in_t)


def _inproj_kernel(h_ref, w_ref, z_ref, wb_ref):
    @pl.when(pl.program_id(1) == 0)
    def _():
        wb_ref[...] = w_ref[0].astype(BF16)

    z_ref[...] = lax.dot_general(h_ref[...], wb_ref[...], (((1,), (1,)), ((), ())),
                                 preferred_element_type=F32)


def _inproj(h_all, w_in_t, n_main, tm=1024, tn=512):
    r, d = h_all.shape
    tm, tn = min(tm, r), min(tn, n_main)
    assert r % tm == 0 and n_main % tn == 0
    return pl.pallas_call(
        _inproj_kernel,
        out_shape=jax.ShapeDtypeStruct((r, n_main), F32),
        grid=(n_main // tn, r // tm),
        in_specs=[pl.BlockSpec((tm, d), lambda j, i: (i, 0)),
                  pl.BlockSpec((1, tn, d), lambda j, i: (0, j, 0))],
        out_specs=pl.BlockSpec((tm, tn), lambda j, i: (i, j)),
        scratch_shapes=[pltpu.VMEM((tn, d), BF16)],
        compiler_params=_cparams(2),
        name="in_proj",
    )(h_all, w_in_t)


def _hgrn_consts():
    L = CHUNK
    t = np.arange(L)[:, None]
    s = np.arange(L)[None, :]
    mats = [(s <= t)]
    masks = []
    for m in HGRN_LEVELS:
        start = (t // (2 * m)) * (2 * m)
        if m in HGRN_MATMUL_LEVELS:
            mats.append(s <= start + m - 1)
        masks.append((t // (2 * m) == s // (2 * m)) & (t % (2 * m) >= m) & (s % (2 * m) < m))
    mstack = np.concatenate(mats, axis=0).astype(np.float32)
    masks = np.stack(masks, axis=0).astype(np.float32)
    return jnp.asarray(mstack, BF16), jnp.asarray(masks, F32)


def _split3(x):
    x1 = x.astype(BF16)
    r1 = x - x1.astype(F32)
    x2 = r1.astype(BF16)
    x3 = (r1 - x2.astype(F32)).astype(BF16)
    return x1, x2, x3


def _hgrn_kernel(rows, nh, dk, dv, hq_ref, hf_ref, hi_ref, hg_ref, s0_ref, lb_ref, gn_ref,
                 mstack_ref, masks_ref, og_ref, st_ref, r_ref, o_ref, k_ref):
    L = CHUNK
    g = pl.program_id(0)
    blk = rows.blk_in_seq(g)
    is_prompt = g < rows.groups_p

    @pl.when(jnp.logical_and(blk == 0, is_prompt))
    def _():
        st_ref[...] = jnp.zeros_like(st_ref)

    @pl.when(jnp.logical_and(blk == 0, jnp.logical_not(is_prompt)))
    def _():
        for h in range(nh):
            st_ref[0, h] = s0_ref[0, h].T

    def sig(x):
        return 0.5 * jnp.tanh(0.5 * x) + 0.5

    lb = lb_ref[...]
    f = lb + (1.0 - lb) * sig(hf_ref[...])
    lf = jnp.log(f)
    mstack = mstack_ref[...]
    p1, p2, p3 = _split3(lf)
    r_ref[...] = (jnp.dot(mstack, p1, preferred_element_type=F32)
                  + jnp.dot(mstack, p2, preferred_element_type=F32)
                  + jnp.dot(mstack, p3, preferred_element_type=F32))

    k_ref[...] = 1.0 - f

    def head(h, carry):
        cs = pl.ds(pl.multiple_of(h * dk, dk), dk)
        vs = pl.ds(pl.multiple_of(h * dv, dv), dv)
        hq = hq_ref[:, cs]
        q = hq * sig(hq) * (dk ** -0.5)
        k = k_ref[:, cs]
        v = hi_ref[:, vs]
        vb = v.astype(BF16)
        b = r_ref[0:L, cs]
        st = st_ref[0, h]
        qe = (q * jnp.exp(b)).astype(BF16)
        o = lax.dot_general(qe, st.astype(BF16), (((1,), (1,)), ((), ())),
                            preferred_element_type=F32)
        a = jnp.zeros((L, L), F32)
        for li, m in enumerate(HGRN_LEVELS):
            if m in HGRN_MATMUL_LEVELS:
                mi = 1 + HGRN_MATMUL_LEVELS.index(m)
                rl = r_ref[mi * L:(mi + 1) * L, cs]
            else:
                rl = jnp.concatenate(
                    [jnp.broadcast_to(b[s0 + m - 1:s0 + m, :], (2 * m, dk)) for s0 in range(0, L, 2 * m)],
                    axis=0)
            e = jnp.exp(-jnp.abs(b - rl))
            al = lax.dot_general((q * e).astype(BF16), (k * e).astype(BF16),
                                 (((1,), (1,)), ((), ())), preferred_element_type=F32)
            a = a + al * masks_ref[li]
        diag = jnp.sum(q * k, axis=-1, keepdims=True)
        o = o + jnp.dot(a.astype(BF16), vb, preferred_element_type=F32) + diag * v
        o_ref[:, vs] = o
        b_last = b[L - 1:L, :]
        kd = (k * jnp.exp(b_last - b)).astype(BF16)
        st_ref[0, h] = st * jnp.exp(b_last) + lax.dot_general(
            vb, kd, (((0,), (0,)), ((), ())), preferred_element_type=F32)
        return carry

    lax.fori_loop(0, nh, head, 0, unroll=HGRN_HEAD_UNROLL)

    o = o_ref[...]
    ms = jnp.mean(o * o, axis=-1, keepdims=True)
    hg = hg_ref[...]
    og = o * lax.rsqrt(ms + RMS_EPS) * gn_ref[...] * (hg * sig(hg))
    og_ref[...] = og.astype(BF16)


def _hgrn(rows, z, s0, lb, g_hgrn):
    bs, nh, dk, dv = s0.shape
    hw = nh * dk
    assert nh * dv == hw and hw % LANES == 0
    mstack, masks = _hgrn_consts()
    n_lv = len(HGRN_LEVELS)
    L = CHUNK

    def zspec(sec):
        return pl.BlockSpec((L, hw), lambda g, sec=sec: (g, sec))

    og, st = pl.pallas_call(
        functools.partial(_hgrn_kernel, rows, nh, dk, dv),
        out_shape=(jax.ShapeDtypeStruct((rows.rows, hw), BF16),
                   jax.ShapeDtypeStruct((rows.nseq, nh, dv, dk), F32)),
        grid=(rows.groups,),
        in_specs=[zspec(0), zspec(1), zspec(2), zspec(3),
                  pl.BlockSpec((1, nh, dk, dv),
                               lambda g: (jnp.maximum(rows.seq_of_group(g) - rows.bp, 0), 0, 0, 0)),
                  pl.BlockSpec((1, hw), lambda g: (0, 0)),
                  pl.BlockSpec((1, hw), lambda g: (0, 0)),
                  pl.BlockSpec(mstack.shape, lambda g: (0, 0)),
                  pl.BlockSpec(masks.shape, lambda g: (0, 0, 0))],
        out_specs=(pl.BlockSpec((L, hw), lambda g: (g, 0)),
                   pl.BlockSpec((1, nh, dv, dk), lambda g: (rows.seq_of_group(g), 0, 0, 0))),
        scratch_shapes=[pltpu.VMEM(((len(HGRN_MATMUL_LEVELS) + 1) * L, hw), F32),
                        pltpu.VMEM((L, hw), F32),
                        pltpu.VMEM((L, hw), F32)],
        compiler_params=_cparams(1),
        name="hgrn2",
    )(z, z, z, z, s0, lb.reshape(1, hw), g_hgrn.reshape(1, hw), mstack, masks)
    return og, st


def _log_sigmoid(x):
    return jnp.minimum(x, 0.0) - jnp.log(1.0 + jnp.exp(-jnp.abs(x)))


def _mlstm_kernel(rows, nh, dqk, dv, mqk_ref, mv_ref, mo_ref, gates_ref, conv0_ref, c0_ref,
                  n0_ref, m0_ref, cw_ref, cb_ref, gbias_ref, gn_ref,
                  hm_ref, conv_ref, c_ref, n_ref, m_ref, ubuf_ref):
    L = CHUNK
    kq = nh * dqk
    pad = 8
    g = pl.program_id(0)
    blk = rows.blk_in_seq(g)
    is_prompt = g < rows.groups_p

    @pl.when(jnp.logical_and(blk == 0, is_prompt))
    def _():
        ubuf_ref[0:pad, :] = jnp.zeros((pad, 2 * kq), F32)
        c_ref[...] = jnp.zeros_like(c_ref)
        n_ref[...] = jnp.zeros_like(n_ref)
        m_ref[...] = jnp.zeros_like(m_ref)

    @pl.when(jnp.logical_and(blk == 0, jnp.logical_not(is_prompt)))
    def _():
        ubuf_ref[0:pad, :] = jnp.zeros((pad, 2 * kq), F32)
        ubuf_ref[pad - (CONV_K - 1):pad, :] = conv0_ref[0]
        c_ref[...] = c0_ref[...]
        n_ref[...] = n0_ref[...]
        m_ref[...] = m0_ref[...]

    ubuf_ref[pad:pad + L, :] = mqk_ref[...]
    acc = cb_ref[...] + jnp.zeros((L, 2 * kq), F32)
    for j in range(CONV_K):
        off = pad - (CONV_K - 1) + j
        acc = acc + ubuf_ref[off:off + L, :] * cw_ref[j:j + 1, :]
    new_tail = ubuf_ref[pad + L - (CONV_K - 1):pad + L, :]
    conv_ref[0] = new_tail
    ubuf_ref[pad - (CONV_K - 1):pad, :] = new_tail
    qk = _silu(acc)

    gt = gates_ref[...] + gbias_ref[...]
    lane = lax.broadcasted_iota(jnp.int32, gt.shape, 1)
    pg = jnp.where(lane < nh, gt, _log_sigmoid(gt))
    pgt = pg.T
    ti = lax.broadcasted_iota(jnp.int32, (L, L), 0)
    si = lax.broadcasted_iota(jnp.int32, (L, L), 1)
    tri = si <= ti

    for h in range(nh):
        q = qk[:, h * dqk:(h + 1) * dqk]
        k = qk[:, kq + h * dqk:kq + (h + 1) * dqk] * (dqk ** -0.5)
        v = mv_ref[:, h * dv:(h + 1) * dv]
        qb, kb, vb = q.astype(BF16), k.astype(BF16), v.astype(BF16)
        ig_c, lf_c = pg[:, h:h + 1], pg[:, nh + h:nh + h + 1]
        ig_r, lf_r = pgt[h:h + 1, :], pgt[nh + h:nh + h + 1, :]
        b_c = jnp.sum(jnp.where(tri, lf_r, 0.0), axis=1, keepdims=True)
        b_r = jnp.sum(jnp.where(ti <= si, lf_c, 0.0), axis=0, keepdims=True)
        m_prev = m_ref[0, :, h:h + 1]
        log_d = jnp.where(tri, b_c - b_r + ig_r, -jnp.inf)
        log_inter = b_c + m_prev
        m_t = jnp.maximum(jnp.max(log_d, axis=1, keepdims=True), log_inter)
        dm = jnp.exp(log_d - m_t)
        s_mat = lax.dot_general(qb, kb, (((1,), (1,)), ((), ())), preferred_element_type=F32) * dm
        w_inter = jnp.exp(log_inter - m_t)
        c_h = c_ref[0, h]
        n_h = n_ref[0, h:h + 1, :]
        num = (jnp.dot(s_mat.astype(BF16), vb, preferred_element_type=F32)
               + w_inter * jnp.dot(qb, c_h.astype(BF16), preferred_element_type=F32))
        den = (jnp.sum(s_mat, axis=1, keepdims=True)
               + w_inter * jnp.sum(q * n_h, axis=1, keepdims=True))
        hh = num / jnp.maximum(jnp.abs(den), jnp.exp(-m_t))
        b_last = b_c[L - 1:L, :]
        lw_c = b_last - b_c + ig_c
        lw_r = b_last - b_r + ig_r
        m_new = jnp.maximum(b_last + m_prev, jnp.max(lw_r, axis=1, keepdims=True))
        decay = jnp.exp(b_last + m_prev - m_new)
        kw = k * jnp.exp(lw_c - m_new)
        c_ref[0, h] = decay * c_h + lax.dot_general(
            kw.astype(BF16), vb, (((0,), (0,)), ((), ())), preferred_element_type=F32)
        n_ref[0, h:h + 1, :] = decay * n_h + jnp.sum(kw, axis=0, keepdims=True)
        m_ref[0, :, h:h + 1] = m_new
        ms = jnp.mean(hh * hh, axis=-1, keepdims=True)
        hn = hh * lax.rsqrt(ms + RMS_EPS) * gn_ref[:, h * dv:(h + 1) * dv]
        hm_ref[:, h * dv:(h + 1) * dv] = (hn * _sigmoid(mo_ref[:, h * dv:(h + 1) * dv])).astype(BF16)


def _mlstm(rows, z, sec_qk, gates, conv0, c0, n0, m0, conv_w, conv_b, b_igate, b_fgate, g_mlstm):
    bs, nh, dqk, dv = c0.shape
    kq = nh * dqk
    mw = nh * dv
    L = CHUNK
    assert 2 * kq == mw, "q/k conv width must equal the value width for the column sections"
    gbias = jnp.zeros((1, GATE_LANES), F32)
    gbias = gbias.at[0, 0:nh].set(b_igate.astype(F32)).at[0, nh:2 * nh].set(b_fgate.astype(F32))

    def sseq(g):
        return jnp.maximum(rows.seq_of_group(g) - rows.bp, 0)

    outs = pl.pallas_call(
        functools.partial(_mlstm_kernel, rows, nh, dqk, dv),
        out_shape=(jax.ShapeDtypeStruct((rows.rows, mw), BF16),
                   jax.ShapeDtypeStruct((rows.nseq, CONV_K - 1, 2 * kq), F32),
                   jax.ShapeDtypeStruct((rows.nseq, nh, dqk, dv), F32),
                   jax.ShapeDtypeStruct((rows.nseq, nh, dqk), F32),
                   jax.ShapeDtypeStruct((rows.nseq, 1, nh), F32)),
        grid=(rows.groups,),
        in_specs=[pl.BlockSpec((L, mw), lambda g: (g, sec_qk)),
                  pl.BlockSpec((L, mw), lambda g: (g, sec_qk + 1)),
                  pl.BlockSpec((L, mw), lambda g: (g, sec_qk + 2)),
                  pl.BlockSpec((L, GATE_LANES), lambda g: (g, 0)),
                  pl.BlockSpec((1, CONV_K - 1, 2 * kq), lambda g: (sseq(g), 0, 0)),
                  pl.BlockSpec((1, nh, dqk, dv), lambda g: (sseq(g), 0, 0, 0)),
                  pl.BlockSpec((1, nh, dqk), lambda g: (sseq(g), 0, 0)),
                  pl.BlockSpec((1, 1, nh), lambda g: (sseq(g), 0, 0)),
                  pl.BlockSpec((CONV_K, 2 * kq), lambda g: (0, 0)),
                  pl.BlockSpec((1, 2 * kq), lambda g: (0, 0)),
                  pl.BlockSpec((1, GATE_LANES), lambda g: (0, 0)),
                  pl.BlockSpec((1, mw), lambda g: (0, 0))],
        out_specs=(pl.BlockSpec((L, mw), lambda g: (g, 0)),
                   pl.BlockSpec((1, CONV_K - 1, 2 * kq), lambda g: (rows.seq_of_group(g), 0, 0)),
                   pl.BlockSpec((1, nh, dqk, dv), lambda g: (rows.seq_of_group(g), 0, 0, 0)),
                   pl.BlockSpec((1, nh, dqk), lambda g: (rows.seq_of_group(g), 0, 0)),
                   pl.BlockSpec((1, 1, nh), lambda g: (rows.seq_of_group(g), 0, 0))),
        scratch_shapes=[pltpu.VMEM((8 + L, 2 * kq), F32)],
        compiler_params=_cparams(1),
        name="mlstm",
    )(z, z, z, gates, conv0, c0, n0, m0.reshape(bs, 1, nh), conv_w, conv_b.reshape(1, 2 * kq),
      gbias, g_mlstm.reshape(1, mw))
    return outs


def _outproj_kernel(rows, tm, d, hw, og_ref, hm_ref, w_ref, xp_ref, xs_ref, mod_ref, o_ref, wb_ref):
    j, i = pl.program_id(0), pl.program_id(1)
    is_prompt = i < rows.groups_p * CHUNK // tm

    @pl.when(i == 0)
    def _():
        wb_ref[...] = w_ref[...].astype(BF16)

    mix = (jnp.dot(og_ref[...], wb_ref[0:hw, :], preferred_element_type=F32)
           + jnp.dot(hm_ref[...], wb_ref[hw:, :], preferred_element_type=F32))
    for gi in range(tm // CHUNK):
        seq = rows.seq_of_group(i * (tm // CHUNK) + gi)
        gate = mod_ref[pl.ds(seq, 1), :]
        rs = slice(gi * CHUNK, (gi + 1) * CHUNK)
        x = jnp.where(is_prompt, xp_ref[rs, :], xs_ref[rs, :])
        o_ref[rs, :] = x + gate * mix[rs, :]


def _outproj(rows, og, hm, w_out, x_p, x_s, mod, tm=1024, tn=512):
    d = x_p.shape[1]
    r = rows.rows
    hw = og.shape[1]
    tm, tn = min(tm, x_s.shape[0]), min(tn, d)
    assert r % tm == 0 and d % tn == 0
    xp_spec, xs_spec = _split_row_specs(rows, tm, tn, col_major=True)
    gate_blk0 = 2 * d // tn
    return pl.pallas_call(
        functools.partial(_outproj_kernel, rows, tm, d, hw),
        out_shape=jax.ShapeDtypeStruct((r, d), F32),
        grid=(d // tn, r // tm),
        in_specs=[pl.BlockSpec((tm, hw), lambda j, i: (i, 0)),
                  pl.BlockSpec((tm, hm.shape[1]), lambda j, i: (i, 0)),
                  pl.BlockSpec((w_out.shape[0], tn), lambda j, i: (0, j)),
                  xp_spec, xs_spec,
                  pl.BlockSpec((mod.shape[0], tn), lambda j, i: (0, gate_blk0 + j))],
        out_specs=pl.BlockSpec((tm, tn), lambda j, i: (i, j)),
        scratch_shapes=[pltpu.VMEM((w_out.shape[0], tn), BF16)],
        compiler_params=_cparams(2),
        name="out_proj",
    )(og, hm, w_out, x_p, x_s, mod)


def _router_kernel(rows, tm, d, ne, x_ref, g_ref, mod_ref, wr_ref, br_ref, ut_ref,
                   h2_ref, e_ref, gate_ref, rank_ref, cnt_ref, h2s_ref):
    i = pl.program_id(0)
    half = d // 2

    @pl.when(i == 0)
    def _():
        cnt_ref[...] = jnp.zeros_like(cnt_ref)

    g = g_ref[...]
    for gi in range(tm // CHUNK):
        seq = rows.seq_of_group(i * (tm // CHUNK) + gi)
        m = mod_ref[pl.ds(seq, 1), :]
        shift, scale = m[:, 3 * d:4 * d], m[:, 4 * d:5 * d]
        x = x_ref[gi * CHUNK:(gi + 1) * CHUNK, :]
        ms = jnp.mean(x * x, axis=-1, keepdims=True)
        h2s_ref[gi * CHUNK:(gi + 1) * CHUNK, :] = (x * lax.rsqrt(ms + RMS_EPS) * g) * (1.0 + scale) + shift
    h2 = h2s_ref[...]
    lo = pltpu.bitcast(h2[:, :half].astype(BF16).astype(F32), jnp.uint32)
    hi = pltpu.bitcast(h2[:, half:].astype(BF16).astype(F32), jnp.uint32)
    h2_ref[...] = (hi & jnp.uint32(0xFFFF0000)) | (lo >> 16)

    logits = _dot_f32x3(h2, wr_ref[...])
    lt = logits.T[0:ne, :] + br_ref[...]
    eidx = lax.broadcasted_iota(jnp.int32, (ne, tm), 0)
    cur = lt
    tops, sels, hots = [], [], []
    for _ in range(TOP_K):
        mx = jnp.max(cur, axis=0, keepdims=True)
        sel = jnp.min(jnp.where(cur == mx, eidx, ne), axis=0, keepdims=True)
        hot = eidx == sel
        tops.append(mx)
        sels.append(sel)
        hots.append(hot)
        cur = jnp.where(hot, -jnp.inf, cur)
    ex = [jnp.exp(t - tops[0]) for t in tops]
    tot = ex[0] + ex[1] + ex[2] + ex[3]
    oh = jnp.zeros((ne, tm), F32)
    for hot in hots:
        oh = oh + hot.astype(F32)
    prefix = jnp.dot(oh.astype(BF16), ut_ref[...], preferred_element_type=F32)
    base = cnt_ref[:, 0:1]
    pos = base + prefix
    for kk in range(TOP_K):
        e_ref[kk:kk + 1, :] = sels[kk]
        gate_ref[kk:kk + 1, :] = ex[kk] / tot
        rank_ref[kk:kk + 1, :] = jnp.sum(jnp.where(hots[kk], pos, 0.0), axis=0,
                                         keepdims=True).astype(jnp.int32)
    cnt_ref[...] = cnt_ref[...] + jnp.sum(oh, axis=1, keepdims=True)


def _router(rows, x1, g_ffn, mod, w_router, b_router, tm=256):
    r, d = x1.shape
    ne = w_router.shape[1]
    tm = min(tm, r)
    assert r % tm == 0 and ne % 8 == 0 and ne <= LANES
    wr_pad = jnp.zeros((d, LANES), F32).at[:, :ne].set(w_router.astype(F32))
    tt = np.arange(tm)
    ut = jnp.asarray((tt[:, None] < tt[None, :]).astype(np.float32), BF16)
    return pl.pallas_call(
        functools.partial(_router_kernel, rows, tm, d, ne),
        out_shape=(jax.ShapeDtypeStruct((r, d // 2), jnp.uint32),
                   jax.ShapeDtypeStruct((TOP_K, r), jnp.int32),
                   jax.ShapeDtypeStruct((TOP_K, r), F32),
                   jax.ShapeDtypeStruct((TOP_K, r), jnp.int32),
                   jax.ShapeDtypeStruct((ne, LANES), F32)),
        grid=(r // tm,),
        in_specs=[pl.BlockSpec((tm, d), lambda i: (i, 0)),
                  pl.BlockSpec((1, d), lambda i: (0, 0)),
                  pl.BlockSpec(mod.shape, lambda i: (0, 0)),
                  pl.BlockSpec((d, LANES), lambda i: (0, 0)),
                  pl.BlockSpec((ne, 1), lambda i: (0, 0)),
                  pl.BlockSpec((tm, tm), lambda i: (0, 0))],
        out_specs=(pl.BlockSpec((tm, d // 2), lambda i: (i, 0)),
                   pl.BlockSpec((TOP_K, tm), lambda i: (0, i)),
                   pl.BlockSpec((TOP_K, tm), lambda i: (0, i)),
                   pl.BlockSpec((TOP_K, tm), lambda i: (0, i)),
                   pl.BlockSpec((ne, LANES), lambda i: (0, 0))),
        scratch_shapes=[pltpu.VMEM((tm, d), F32)],
        compiler_params=_cparams(1),
        name="router",
    )(x1, g_ffn.reshape(1, d), mod, wr_pad, b_router.reshape(ne, 1).astype(F32), ut)


def _invert_kernel(tm, pstart_ref, e_ref, rank_ref, rt_ref):
    i = pl.program_id(0)

    @pl.when(i == 0)
    def _():
        def clear(r, carry):
            rt_ref[r] = 0
            return carry

        lax.fori_loop(0, rt_ref.shape[0], clear, 0, unroll=8)

    def body(t, carry):
        for kk in range(TOP_K):
            rt_ref[pstart_ref[e_ref[kk, t]] + rank_ref[kk, t]] = i * tm + t
        return carry

    lax.fori_loop(0, tm, body, 0, unroll=4)


def _invert(e_idx, rank, pstart, n_rows, tm=256):
    r = e_idx.shape[1]
    tm = min(tm, r)
    assert r % tm == 0
    return pl.pallas_call(
        functools.partial(_invert_kernel, tm),
        out_shape=jax.ShapeDtypeStruct((n_rows,), jnp.int32),
        grid_spec=pltpu.PrefetchScalarGridSpec(
            num_scalar_prefetch=1,
            grid=(r // tm,),
            in_specs=[pl.BlockSpec((TOP_K, tm), lambda i, *_: (0, i), memory_space=pltpu.SMEM),
                      pl.BlockSpec((TOP_K, tm), lambda i, *_: (0, i), memory_space=pltpu.SMEM)],
            out_specs=pl.BlockSpec(memory_space=pltpu.SMEM)),
        compiler_params=_cparams(1),
        name="moe_invert",
    )(pstart, e_idx, rank)


def _unpack_rows(xw):
    lo = pltpu.bitcast(xw << 16, F32).astype(BF16)
    hi = pltpu.bitcast(xw & jnp.uint32(0xFFFF0000), F32).astype(BF16)
    return lo, hi


def _gather_kernel(nused_ref, rt_cur_ref, rt_next_ref, h2_ref, xs_ref, buf_ref, sem):
    b = pl.program_id(0)
    n_used = nused_ref[0]
    half = buf_ref.shape[2]

    def row_copy(slot, src_row, dst_row):
        return pltpu.make_async_copy(h2_ref.at[pl.ds(src_row, 1), :],
                                     buf_ref.at[slot, pl.ds(dst_row, 1), :], sem.at[slot])

    def issue(rt_ref, slot):
        def pair(t2, carry):
            for p in range(2):
                t = 2 * t2 + p
                row_copy(slot, rt_ref[0, 0, t], t).start(priority=p)
            return carry

        lax.fori_loop(0, MOE_BLOCK // 2, pair, 0, unroll=4)

    @pl.when(jnp.logical_and(b == 0, n_used > 0))
    def _():
        issue(rt_cur_ref, 0)

    @pl.when(b + 1 < n_used)
    def _():
        issue(rt_next_ref, (b + 1) % 2)

    @pl.when(b < n_used)
    def _():
        slot = b % 2
        pltpu.make_async_copy(buf_ref.at[slot], buf_ref.at[slot], sem.at[slot]).wait()
        lo, hi = _unpack_rows(buf_ref[slot])
        xs_ref[:, 0:half] = lo
        xs_ref[:, half:] = hi

    @pl.when(b >= n_used)
    def _():
        xs_ref[...] = jnp.zeros_like(xs_ref)


def _gather_rows(h2p, row_tok, n_used):
    r, half = h2p.shape
    n_blocks = row_tok.shape[0] // MOE_BLOCK
    rt3 = row_tok.reshape(n_blocks, 1, MOE_BLOCK)
    return pl.pallas_call(
        _gather_kernel,
        out_shape=jax.ShapeDtypeStruct((n_blocks * MOE_BLOCK, 2 * half), BF16),
        grid_spec=pltpu.PrefetchScalarGridSpec(
            num_scalar_prefetch=1,
            grid=(n_blocks,),
            in_specs=[pl.BlockSpec((1, 1, MOE_BLOCK), lambda b, *_: (b, 0, 0), memory_space=pltpu.SMEM),
                      pl.BlockSpec((1, 1, MOE_BLOCK), lambda b, *_: (jnp.minimum(b + 1, n_blocks - 1), 0, 0),
                                   memory_space=pltpu.SMEM),
                      pl.BlockSpec(memory_space=pl.ANY)],
            out_specs=pl.BlockSpec((MOE_BLOCK, 2 * half), lambda b, *_: (b, 0)),
            scratch_shapes=[pltpu.VMEM((2, MOE_BLOCK, half), jnp.uint32),
                            pltpu.SemaphoreType.DMA((2,))]),
        compiler_params=_cparams(1),
        name="moe_gather",
    )(n_used.reshape(1), rt3, rt3, h2p)


MOE_CACHE_BLOCKS = 4


class _ExpertRows:
    def __init__(self, src_hbm, slots_ref, sems, b0, nb, first_tile):
        self.src, self.slots, self.sems = src_hbm, slots_ref, sems
        self.b0, self.nb, self.first_tile = b0, nb, first_tile

    def slot_of(self, i):
        return jnp.where(i < MOE_CACHE_BLOCKS, i, MOE_CACHE_BLOCKS + i % 2)

    def _needs_load(self, i):
        return jnp.logical_or(self.first_tile, i >= MOE_CACHE_BLOCKS)

    def _copy(self, i):
        row0 = pl.multiple_of((self.b0 + i) * MOE_BLOCK, MOE_BLOCK)
        return pltpu.make_async_copy(self.src.at[pl.ds(row0, MOE_BLOCK), :],
                                     self.slots.at[self.slot_of(i)], self.sems.at[i % 2])

    def request(self, i):
        @pl.when(jnp.logical_and(i < self.nb, self._needs_load(i)))
        def _():
            self._copy(i).start()

    def arrive(self, i):
        @pl.when(self._needs_load(i))
        def _():
            self._copy(i).wait()

    def request_first(self, e):
        @pl.when(e == 0)
        def _():
            self.request(0)

    def request_next_expert(self, e, ne, last_tile, bstart_ref, nblk_ref):
        @pl.when(jnp.logical_and(last_tile, e + 1 < ne))
        def _():
            @pl.when(nblk_ref[e + 1] > 0)
            def _():
                row0 = pl.multiple_of(bstart_ref[e + 1] * MOE_BLOCK, MOE_BLOCK)
                pltpu.make_async_copy(self.src.at[pl.ds(row0, MOE_BLOCK), :],
                                      self.slots.at[0], self.sems.at[0]).start()


def _block_window(dst_hbm, blk, col0, width):
    row0 = pl.multiple_of(blk * MOE_BLOCK, MOE_BLOCK)
    return dst_hbm.at[pl.ds(row0, MOE_BLOCK), pl.ds(pl.multiple_of(col0, width), width)]


def _zero_tail_blocks(stage_ref, dst_hbm, sem, first_blk, col0, width):
    n_blocks = dst_hbm.shape[0] // MOE_BLOCK
    stage_ref[0] = jnp.zeros(stage_ref.shape[1:], stage_ref.dtype)

    def cp(blk):
        return pltpu.make_async_copy(stage_ref.at[0], _block_window(dst_hbm, blk, col0, width), sem.at[0])

    def start(blk, carry):
        cp(blk).start()
        return carry

    def wait(blk, carry):
        cp(blk).wait()
        return carry

    lax.fori_loop(first_blk, n_blocks, start, 0)
    lax.fori_loop(first_blk, n_blocks, wait, 0)


class _OutRing:
    def __init__(self, stage_ref, sems, cnt_ref, dst_hbm, width):
        self.stage, self.sems, self.cnt, self.dst, self.width = stage_ref, sems, cnt_ref, dst_hbm, width

    def init(self):
        self.cnt[0] = 0
        self.cnt[1] = 0

    def _await_next(self, limit):
        w = self.cnt[1]

        @pl.when(w < limit)
        def _():
            pltpu.make_async_copy(self.stage.at[w % 2], _block_window(self.dst, 0, 0, self.width),
                                  self.sems.at[w % 2]).wait()
            self.cnt[1] = w + 1

    def reserve(self):
        self._await_next(self.cnt[0] - 1)

    def push(self, value, blk, col0):
        c = self.cnt[0]
        self.stage[c % 2] = value
        pltpu.make_async_copy(self.stage.at[c % 2], _block_window(self.dst, blk, col0, self.width),
                              self.sems.at[c % 2]).start()
        self.cnt[0] = c + 1

    def drain(self):
        for _ in range(2):
            self._await_next(self.cnt[0])


MOE_CAST_CHUNKS = 4


def _cast_chunk(w_ref, wbf_ref, c):
    kc = wbf_ref.shape[0] // MOE_CAST_CHUNKS
    wbf_ref[c * kc:(c + 1) * kc, :] = w_ref[0, c * kc:(c + 1) * kc, :].astype(BF16)


def _first_block_dot(x_blk_ref, w_ref, wbf_ref):
    kc = wbf_ref.shape[0] // MOE_CAST_CHUNKS
    acc = None
    for c in range(MOE_CAST_CHUNKS):
        if c + 1 < MOE_CAST_CHUNKS:
            _cast_chunk(w_ref, wbf_ref, c + 1)
        part = jnp.dot(x_blk_ref[:, c * kc:(c + 1) * kc], wbf_ref[c * kc:(c + 1) * kc, :],
                       preferred_element_type=F32)
        acc = part if acc is None else acc + part
    return acc


def _moe_up_kernel(tf, nj, ne, bstart_ref, nblk_ref, xs_hbm, w_ref, b_ref, act_hbm,
                   wbf_ref, xb_ref, gt_ref, ostage_ref, gu_carry_ref, cnt_ref, xsem, osem):
    g = pl.program_id(0)
    e, j = g // nj, g % nj
    nb, b0 = nblk_ref[e], bstart_ref[e]
    nslab = MOE_BLOCK // LANES
    xrows = _ExpertRows(xs_hbm, xb_ref, xsem, b0, nb, j == 0)
    ring = _OutRing(ostage_ref, osem, cnt_ref, act_hbm, tf)
    PEND, PEND_BLK, PEND_COL = 2, 3, 4

    @pl.when(g == 0)
    def _():
        ring.init()
        cnt_ref[PEND] = 0
        gu_carry_ref[...] = jnp.zeros_like(gu_carry_ref)

    def up_dot(i):
        return jnp.dot(xb_ref[xrows.slot_of(i)], wbf_ref[...], preferred_element_type=F32) + b_ref[0]

    def swiglu(gu):
        gut = gu.T
        for sl in range(nslab):
            gt_ref[sl] = gut[:, sl * LANES:(sl + 1) * LANES]
        parts = []
        for sl in range(nslab):
            gate = jnp.minimum(gt_ref[sl, pl.ds(0, tf, stride=2), :], SWIGLU_LIMIT)
            up = jnp.clip(gt_ref[sl, pl.ds(1, tf, stride=2), :], -SWIGLU_LIMIT, SWIGLU_LIMIT)
            parts.append((up + 1.0) * gate * _sigmoid(SWIGLU_ALPHA * gate))
        return jnp.concatenate(parts, axis=1).T.astype(BF16)

    @pl.when(nb > 0)
    def _():
        xrows.request_first(e)
        _cast_chunk(w_ref, wbf_ref, 0)
        xrows.arrive(0)
        xrows.request(1)
        ring.reserve()
        owed = cnt_ref[PEND] == 1
        gu0 = _first_block_dot(xb_ref.at[0], w_ref, wbf_ref) + b_ref[0]
        act_owed = swiglu(gu_carry_ref[...])

        @pl.when(owed)
        def _():
            ring.push(act_owed, cnt_ref[PEND_BLK], cnt_ref[PEND_COL])
            cnt_ref[PEND] = 0

        def body(i, gu_prev):
            xrows.arrive(i)
            xrows.request(i + 1)
            ring.reserve()
            gu = up_dot(i)
            ring.push(swiglu(gu_prev), b0 + i - 1, j * tf)
            return gu

        gu_last = lax.fori_loop(1, nb, body, gu0)

        @pl.when(e < ne - 1)
        def _():
            gu_carry_ref[...] = gu_last
            cnt_ref[PEND] = 1
            cnt_ref[PEND_BLK] = b0 + nb - 1
            cnt_ref[PEND_COL] = j * tf

        @pl.when(e == ne - 1)
        def _():
            ring.reserve()
            ring.push(swiglu(gu_last), b0 + nb - 1, j * tf)

    xrows.request_next_expert(e, ne, j == nj - 1, bstart_ref, nblk_ref)

    @pl.when(e == ne - 1)
    def _():
        @pl.when(cnt_ref[PEND] == 1)
        def _():
            ring.reserve()
            ring.push(swiglu(gu_carry_ref[...]), cnt_ref[PEND_BLK], cnt_ref[PEND_COL])
            cnt_ref[PEND] = 0

        ring.drain()
        _zero_tail_blocks(ostage_ref, act_hbm, osem, b0 + nb, j * tf, tf)


def _moe_up(xs, w_gu, b_gu, bstart, nblk, tf=512):
    n_rows, d = xs.shape
    ne, _, f2 = w_gu.shape
    f = f2 // 2
    tf = min(tf, f)
    assert f % tf == 0 and w_gu.shape[1] == d
    nj = f // tf
    return pl.pallas_call(
        functools.partial(_moe_up_kernel, tf, nj, ne),
        out_shape=jax.ShapeDtypeStruct((n_rows, f), BF16),
        grid_spec=pltpu.PrefetchScalarGridSpec(
            num_scalar_prefetch=2,
            grid=(ne * nj,),
            in_specs=[pl.BlockSpec(memory_space=pl.ANY),
                      pl.BlockSpec((1, d, 2 * tf), lambda g, *_: (g // nj, 0, g % nj)),
                      pl.BlockSpec((1, 1, 2 * tf), lambda g, *_: (g // nj, 0, g % nj))],
            out_specs=pl.BlockSpec(memory_space=pl.ANY),
            scratch_shapes=[pltpu.VMEM((d, 2 * tf), BF16),
                            pltpu.VMEM((MOE_CACHE_BLOCKS + 2, MOE_BLOCK, d), BF16),
                            pltpu.VMEM((MOE_BLOCK // LANES, 2 * tf, LANES), F32),
                            pltpu.VMEM((2, MOE_BLOCK, tf), BF16),
                            pltpu.VMEM((MOE_BLOCK, 2 * tf), F32),
                            pltpu.SMEM((8,), jnp.int32),
                            pltpu.SemaphoreType.DMA((2,)),
                            pltpu.SemaphoreType.DMA((2,))]),
        compiler_params=_cparams(1),
        name="moe_up",
    )(bstart, nblk, xs, w_gu, b_gu.reshape(ne, 1, f2))


def _moe_down_kernel(tn, nj, ne, bstart_ref, nblk_ref, act_hbm, w_ref, b_ref, yb_hbm,
                     wbf_ref, ab_ref, ostage_ref, cnt_ref, asem, osem):
    g = pl.program_id(0)
    e, j = g // nj, g % nj
    nb, b0 = nblk_ref[e], bstart_ref[e]
    arows = _ExpertRows(act_hbm, ab_ref, asem, b0, nb, j == 0)
    ring = _OutRing(ostage_ref, osem, cnt_ref, yb_hbm, tn)

    @pl.when(g == 0)
    def _():
        ring.init()

    @pl.when(nb > 0)
    def _():
        arows.request_first(e)
        _cast_chunk(w_ref, wbf_ref, 0)
        arows.arrive(0)
        arows.request(1)
        ring.reserve()
        ring.push(_first_block_dot(ab_ref.at[0], w_ref, wbf_ref) + b_ref[0], b0, j * tn)

        def body(i, carry):
            arows.arrive(i)
            arows.request(i + 1)
            ring.reserve()
            y = jnp.dot(ab_ref[arows.slot_of(i)], wbf_ref[...], preferred_element_type=F32) + b_ref[0]
            ring.push(y, b0 + i, j * tn)
            return carry

        lax.fori_loop(1, nb, body, 0)

    arows.request_next_expert(e, ne, j == nj - 1, bstart_ref, nblk_ref)

    @pl.when(e == ne - 1)
    def _():
        ring.drain()
        _zero_tail_blocks(ostage_ref, yb_hbm, osem, b0 + nb, j * tn, tn)


def _moe_down(act, w_d, b_d, bstart, nblk, tn=1024):
    n_rows, f = act.shape
    ne, _, d = w_d.shape
    tn = min(tn, d)
    assert d % tn == 0
    nj = d // tn
    return pl.pallas_call(
        functools.partial(_moe_down_kernel, tn, nj, ne),
        out_shape=jax.ShapeDtypeStruct((n_rows, d), F32),
        grid_spec=pltpu.PrefetchScalarGridSpec(
            num_scalar_prefetch=2,
            grid=(ne * nj,),
            in_specs=[pl.BlockSpec(memory_space=pl.ANY),
                      pl.BlockSpec((1, f, tn), lambda g, *_: (g // nj, 0, g % nj)),
                      pl.BlockSpec((1, 1, tn), lambda g, *_: (g // nj, 0, g % nj))],
            out_specs=pl.BlockSpec(memory_space=pl.ANY),
            scratch_shapes=[pltpu.VMEM((f, tn), BF16),
                            pltpu.VMEM((MOE_CACHE_BLOCKS + 2, MOE_BLOCK, f), BF16),
                            pltpu.VMEM((2, MOE_BLOCK, tn), F32),
                            pltpu.SMEM((2,), jnp.int32),
                            pltpu.SemaphoreType.DMA((2,)),
                            pltpu.SemaphoreType.DMA((2,))]),
        compiler_params=_cparams(1),
        name="moe_down",
    )(bstart, nblk, act, w_d, b_d.reshape(ne, 1, d))


def _combine_kernel(rows, tm, d, n_tiles, pstart_ref, e_ref, rank_ref, en_ref, rankn_ref, gt_ref,
                    x_ref, mod_ref, gf_ref, yb_ref, yp_ref, ys_ref, buf_ref, sem):
    i = pl.program_id(0)
    is_prompt = i < rows.groups_p * CHUNK // tm
    slot = i % 2

    def row_copy(s, dst_k, dst_t, src_row):
        return pltpu.make_async_copy(yb_ref.at[pl.ds(src_row, 1), :],
                                     buf_ref.at[s, dst_k, pl.ds(dst_t, 1), :], sem.at[s])

    def issue(eref, rref, s):
        def one(t, carry):
            for kk in range(TOP_K):
                src = pstart_ref[eref[kk, t]] + rref[kk, t]
                row_copy(s, kk, t, src).start(priority=kk % 2)
            return carry

        lax.fori_loop(0, tm, one, 0, unroll=2)

    @pl.when(i == 0)
    def _():
        issue(e_ref, rank_ref, 0)

    @pl.when(i + 1 < n_tiles)
    def _():
        issue(en_ref, rankn_ref, (i + 1) % 2)

    pltpu.make_async_copy(buf_ref.at[slot], buf_ref.at[slot], sem.at[slot]).wait()

    gf = gf_ref[...]
    for gi in range(tm // CHUNK):
        rs = slice(gi * CHUNK, (gi + 1) * CHUNK)
        seq = rows.seq_of_group(i * (tm // CHUNK) + gi)
        gate2 = mod_ref[pl.ds(seq, 1), :]
        ff = jnp.zeros((CHUNK, d), F32)
        for kk in range(TOP_K):
            ff = ff + buf_ref[slot, kk, rs, :] * gt_ref[rs, kk:kk + 1]
        x2 = x_ref[rs, :] + gate2 * ff
        ms = jnp.mean(x2 * x2, axis=-1, keepdims=True)
        y = x2 * lax.rsqrt(ms + RMS_EPS) * gf

        @pl.when(is_prompt)
        def _():
            yp_ref[rs, :] = y

        @pl.when(jnp.logical_not(is_prompt))
        def _():
            ys_ref[rs, :] = y


def _combine(rows, x1, mod, g_final, yb, e_idx, rank, gates_t, pstart, tm=128):
    r, d = x1.shape
    tm = min(tm, r)
    assert r % tm == 0
    gate2_blk = 5
    yp_spec, ys_spec = _split_row_specs(rows, tm, d)
    n_tiles = r // tm

    def nxt(i, *_):
        return (0, jnp.minimum(i + 1, n_tiles - 1))

    return pl.pallas_call(
        functools.partial(_combine_kernel, rows, tm, d, n_tiles),
        out_shape=(jax.ShapeDtypeStruct((rows.groups_p * CHUNK, d), F32),
                   jax.ShapeDtypeStruct((r - rows.groups_p * CHUNK, d), F32)),
        grid_spec=pltpu.PrefetchScalarGridSpec(
            num_scalar_prefetch=1,
            grid=(n_tiles,),
            in_specs=[pl.BlockSpec((TOP_K, tm), lambda i, *_: (0, i), memory_space=pltpu.SMEM),
                      pl.BlockSpec((TOP_K, tm), lambda i, *_: (0, i), memory_space=pltpu.SMEM),
                      pl.BlockSpec((TOP_K, tm), nxt, memory_space=pltpu.SMEM),
                      pl.BlockSpec((TOP_K, tm), nxt, memory_space=pltpu.SMEM),
                      pl.BlockSpec((tm, TOP_K), lambda i, *_: (i, 0)),
                      pl.BlockSpec((tm, d), lambda i, *_: (i, 0)),
                      pl.BlockSpec((mod.shape[0], d), lambda i, *_: (0, gate2_blk)),
                      pl.BlockSpec((1, d), lambda i, *_: (0, 0)),
                      pl.BlockSpec(memory_space=pl.ANY)],
            out_specs=(yp_spec, ys_spec),
            scratch_shapes=[pltpu.VMEM((2, TOP_K, tm, d), F32),
                            pltpu.SemaphoreType.DMA((2,))]),
        compiler_params=_cparams(1),
        name="moe_combine",
    )(pstart, e_idx, rank, e_idx, rank, gates_t, x1, mod, g_final.reshape(1, d), yb)


def kernel(x_prompt, x_sample, c_prompt, c_sample, state_hgrn_S, state_conv, state_mlstm_C,
           state_mlstm_n, state_mlstm_m, w_ada, b_ada, g_mix, g_ffn, w_in, lb_logits, conv_w,
           conv_b, b_igate, b_fgate, g_hgrn, g_mlstm, w_out, w_router, b_router, w_gate_up,
           b_gate_up, w_down, b_down, g_final):
    depth = w_ada.shape[0]
    assert depth == 1, "single-layer trunk"
    bp, tp, d = x_prompt.shape
    bs, ts, _ = x_sample.shape
    rows = _Rows(bp, tp, bs, ts)
    _, _, nh_h, dk, dv = state_hgrn_S.shape
    _, _, nh_m, dqk, dvm = state_mlstm_C.shape
    hw = nh_h * dk
    mw = nh_m * dvm
    n_main = 4 * hw + 3 * mw
    assert w_in.shape[2] == n_main + 2 * nh_m and 2 * nh_m <= GATE_LANES
    assert mw == hw, "column sections of the input projection are addressed in hw-wide blocks"
    ne = w_router.shape[2]

    lb = jax.nn.softmax(lb_logits.astype(F32), axis=0)[0]

    x_p = x_prompt.reshape(bp * tp, d)
    x_s = x_sample.reshape(bs * ts, d)
    n_c = bp + bs
    n_c_pad = -(-n_c // 8) * 8
    c_pad = jnp.zeros((n_c_pad, d), F32).at[:n_c].set(jnp.concatenate([c_prompt, c_sample], axis=0))
    mod = _ada(c_pad, w_ada[0], b_ada[0])

    w_in_t = jnp.swapaxes(w_in, 1, 2)
    h_all, gates = _prep(rows, x_p, x_s, g_mix[0], mod, w_in_t, n_main)
    z = _inproj(h_all, w_in_t, n_main)

    og, st = _hgrn(rows, z, state_hgrn_S[0], lb, g_hgrn[0])
    hm, conv_new, c_new, n_new, m_new = _mlstm(
        rows, z, 4, gates, state_conv[0], state_mlstm_C[0], state_mlstm_n[0], state_mlstm_m[0],
        conv_w[0], conv_b[0], b_igate[0], b_fgate[0], g_mlstm[0])

    x1 = _outproj(rows, og, hm, w_out[0], x_p, x_s, mod)

    h2p, e_idx, gate_k, rank, cnt = _router(rows, x1, g_ffn[0], mod, w_router[0], b_router[0])

    counts = cnt[:, 0].astype(jnp.int32)
    nblk_e = (counts + MOE_BLOCK - 1) // MOE_BLOCK
    padded = nblk_e * MOE_BLOCK
    pad_end = jnp.cumsum(padded)
    pstart = jnp.concatenate([jnp.zeros((1,), jnp.int32), pad_end]).astype(jnp.int32)
    n_blocks_max = -(-(rows.rows * TOP_K) // MOE_BLOCK) + ne
    n_rows = n_blocks_max * MOE_BLOCK
    blk_start_e = (pstart[:ne] // MOE_BLOCK).astype(jnp.int32)
    n_used = (pstart[ne] // MOE_BLOCK).astype(jnp.int32)

    row_tok = _invert(e_idx, rank, pstart, n_rows)
    xs = _gather_rows(h2p, row_tok, n_used)
    act = _moe_up(xs, w_gate_up[0], b_gate_up[0], blk_start_e, nblk_e.astype(jnp.int32))
    yb = _moe_down(act, w_down[0], b_down[0], blk_start_e, nblk_e.astype(jnp.int32))

    y_p, y_s = _combine(rows, x1, mod, g_final, yb, e_idx, rank, gate_k.T, pstart)

    y_prompt = y_p.reshape(bp, tp, d)
    y_sample = y_s.reshape(bs, ts, d)
    s_all = jnp.swapaxes(st, 2, 3)
    m_all = m_new.reshape(rows.nseq, nh_m)

    def split(a):
        return a[:bp][None], a[bp:][None]

    p_s, s_s = split(s_all)
    p_conv, s_conv = split(conv_new)
    p_c, s_c = split(c_new)
    p_n, s_n = split(n_new)
    p_m, s_m = split(m_all)
    return (y_prompt, y_sample, p_s, p_conv, p_c, p_n, p_m, s_s, s_conv, s_c, s_n, s_m)
```

```python
import functools

import numpy as np
import jax
import jax.numpy as jnp
from jax import lax
from jax.experimental import pallas as pl
from jax.experimental.pallas import tpu as pltpu

F32 = jnp.float32
BF16 = jnp.bfloat16

CHUNK = 64
TOP_K = 4
CONV_K = 4
RMS_EPS = 1e-6
SWIGLU_LIMIT = 7.0
SWIGLU_ALPHA = 1.702
MOE_BLOCK = 256
LANES = 128
GATE_LANES = 128
VMEM_LIMIT = 60 * 1024 * 1024
HGRN_LEVELS = (32, 16, 8, 4, 2, 1)
HGRN_MATMUL_LEVELS = (2, 1)
HGRN_HEAD_UNROLL = 8


def _cparams(n_axes, vmem=VMEM_LIMIT):
    return pltpu.CompilerParams(dimension_semantics=("arbitrary",) * n_axes,
                                vmem_limit_bytes=vmem)


def _sigmoid(x):
    return 1.0 / (1.0 + jnp.exp(-x))


def _silu(x):
    return x * _sigmoid(x)


def _dot_f32x3(a, b):
    a_hi = a.astype(BF16)
    a_lo = (a - a_hi.astype(F32)).astype(BF16)
    b_hi = b.astype(BF16)
    b_lo = (b - b_hi.astype(F32)).astype(BF16)
    return (jnp.dot(a_hi, b_hi, preferred_element_type=F32)
            + jnp.dot(a_hi, b_lo, preferred_element_type=F32)
            + jnp.dot(a_lo, b_hi, preferred_element_type=F32))


class _Rows:
    def __init__(self, bp, tp, bs, ts):
        assert tp % CHUNK == 0 and ts % CHUNK == 0
        self.bp, self.tp, self.bs, self.ts = bp, tp, bs, ts
        self.nblk_p, self.nblk_s = tp // CHUNK, ts // CHUNK
        self.groups_p = bp * self.nblk_p
        self.groups = self.groups_p + bs * self.nblk_s
        self.rows = self.groups * CHUNK
        self.nseq = bp + bs

    def seq_of_group(self, g):
        return jnp.where(g < self.groups_p, g // self.nblk_p,
                         self.bp + (g - self.groups_p) // self.nblk_s)

    def blk_in_seq(self, g):
        return jnp.where(g < self.groups_p, g % self.nblk_p, (g - self.groups_p) % self.nblk_s)

    def nblk_of_group(self, g):
        return jnp.where(g < self.groups_p, self.nblk_p, self.nblk_s)


def _ada_kernel(c_ref, w_ref, b_ref, o_ref):
    c = c_ref[...]
    s = _silu(c).astype(BF16)
    o_ref[...] = jnp.dot(s, w_ref[...].astype(BF16), preferred_element_type=F32) + b_ref[...]


def _ada(c_pad, w_ada, b_ada, tn=512):
    m, d = c_pad.shape
    n = w_ada.shape[1]
    tn = min(tn, n)
    assert n % tn == 0
    return pl.pallas_call(
        _ada_kernel,
        out_shape=jax.ShapeDtypeStruct((m, n), F32),
        grid=(n // tn,),
        in_specs=[pl.BlockSpec((m, d), lambda j: (0, 0)),
                  pl.BlockSpec((d, tn), lambda j: (0, j)),
                  pl.BlockSpec((1, tn), lambda j: (0, j))],
        out_specs=pl.BlockSpec((m, tn), lambda j: (0, j)),
        compiler_params=_cparams(1),
        name="ada_mod",
    )(c_pad, w_ada, b_ada.reshape(1, n))


def _prep_kernel(rows, tm, d, n_gate, xp_ref, xs_ref, g_ref, mod_ref, wtail_ref,
                 h_ref, gates_ref, wg_ref, hf_ref):
    i = pl.program_id(0)
    is_prompt = i < rows.groups_p * CHUNK // tm

    @pl.when(i == 0)
    def _():
        lanes = wg_ref.shape[1]
        padded = jnp.concatenate([wtail_ref[0], jnp.zeros((lanes - n_gate, d), F32)], axis=0)
        wg_ref[...] = padded.T

    g = g_ref[...]
    for gi in range(tm // CHUNK):
        seq = rows.seq_of_group(i * (tm // CHUNK) + gi)
        m = mod_ref[pl.ds(seq, 1), :]
        shift, scale = m[:, 0:d], m[:, d:2 * d]
        rs = slice(gi * CHUNK, (gi + 1) * CHUNK)
        x = jnp.where(is_prompt, xp_ref[rs, :], xs_ref[rs, :])
        ms = jnp.mean(x * x, axis=-1, keepdims=True)
        y = x * lax.rsqrt(ms + RMS_EPS) * g
        h = y * (1.0 + scale) + shift
        h_ref[rs, :] = h.astype(BF16)
        hf_ref[rs, :] = h
    gates_ref[...] = _dot_f32x3(hf_ref[...], wg_ref[...])


def _split_row_specs(rows, tm, tn, col_major=False):
    n_pt = rows.groups_p * CHUNK // tm
    n_st = rows.rows // tm - n_pt
    assert n_pt * tm == rows.groups_p * CHUNK and n_st >= 1

    def ij(args):
        return (args[1], args[0]) if col_major else (args[0], 0)

    def p_map(*args):
        i, j = ij(args)
        return (jnp.minimum(i, n_pt - 1), j)

    def s_map(*args):
        i, j = ij(args)
        return (jnp.maximum(i - n_pt, 0), j)

    return pl.BlockSpec((tm, tn), p_map), pl.BlockSpec((tm, tn), s_map)


def _prep(rows, x_p, x_s, g_mix, mod, w_in_t, n_main, tm=256):
    d = x_p.shape[1]
    r = rows.rows
    tm = min(tm, x_s.shape[0])
    n_gate = w_in_t.shape[1] - n_main
    assert r % tm == 0 and n_gate % 8 == 0 and n_main % n_gate == 0 and n_gate <= GATE_LANES
    xp_spec, xs_spec = _split_row_specs(rows, tm, d)
    return pl.pallas_call(
        functools.partial(_prep_kernel, rows, tm, d, n_gate),
        out_shape=(jax.ShapeDtypeStruct((r, d), BF16),
                   jax.ShapeDtypeStruct((r, GATE_LANES), F32)),
        grid=(r // tm,),
        in_specs=[xp_spec, xs_spec,
                  pl.BlockSpec((1, d), lambda i: (0, 0)),
                  pl.BlockSpec(mod.shape, lambda i: (0, 0)),
                  pl.BlockSpec((1, n_gate, d), lambda i: (0, n_main // n_gate, 0))],
        out_specs=(pl.BlockSpec((tm, d), lambda i: (i, 0)),
                   pl.BlockSpec((tm, GATE_LANES), lambda i: (i, 0))),
        scratch_shapes=[pltpu.VMEM((d, GATE_LANES), F32),
                        pltpu.VMEM((tm, d), F32)],
        compiler_params=_cparams(1),
        name="prep_norm_mod",
    )(x_p, x_s, g_mix.reshape(1, d), mod, w_in_t)


def _inproj_kernel(h_ref, w_ref, z_ref, wb_ref):
    @pl.when(pl.program_id(1) == 0)
    def _():
        wb_ref[...] = w_ref[0].astype(BF16)

    z_ref[...] = lax.dot_general(h_ref[...], wb_ref[...], (((1,), (1,)), ((), ())),
                                 preferred_element_type=F32)


def _inproj(h_all, w_in_t, n_main, tm=1024, tn=512):
    r, d = h_all.shape
    tm, tn = min(tm, r), min(tn, n_main)
    assert r % tm == 0 and n_main % tn == 0
    return pl.pallas_call(
        _inproj_kernel,
        out_shape=jax.ShapeDtypeStruct((r, n_main), F32),
        grid=(n_main // tn, r // tm),
        in_specs=[pl.BlockSpec((tm, d), lambda j, i: (i, 0)),
                  pl.BlockSpec((1, tn, d), lambda j, i: (0, j, 0))],
        out_specs=pl.BlockSpec((tm, tn), lambda j, i: (i, j)),
        scratch_shapes=[pltpu.VMEM((tn, d), BF16)],
        compiler_params=_cparams(2),
        name="in_proj",
    )(h_all, w_in_t)


def _hgrn_consts():
    L = CHUNK
    t = np.arange(L)[:, None]
    s = np.arange(L)[None, :]
    mats = [(s <= t)]
    masks = []
    for m in HGRN_LEVELS:
        start = (t // (2 * m)) * (2 * m)
        if m in HGRN_MATMUL_LEVELS:
            mats.append(s <= start + m - 1)
        masks.append((t // (2 * m) == s // (2 * m)) & (t % (2 * m) >= m) & (s % (2 * m) < m))
    mstack = np.concatenate(mats, axis=0).astype(np.float32)
    masks = np.stack(masks, axis=0).astype(np.float32)
    return jnp.asarray(mstack, BF16), jnp.asarray(masks, F32)


def _split3(x):
    x1 = x.astype(BF16)
    r1 = x - x1.astype(F32)
    x2 = r1.astype(BF16)
    x3 = (r1 - x2.astype(F32)).astype(BF16)
    return x1, x2, x3


def _hgrn_kernel(rows, nh, dk, dv, hq_ref, hf_ref, hi_ref, hg_ref, s0_ref, lb_ref, gn_ref,
                 mstack_ref, masks_ref, og_ref, st_ref, r_ref, o_ref, k_ref):
    L = CHUNK
    g = pl.program_id(0)
    blk = rows.blk_in_seq(g)
    is_prompt = g < rows.groups_p

    @pl.when(jnp.logical_and(blk == 0, is_prompt))
    def _():
        st_ref[...] = jnp.zeros_like(st_ref)

    @pl.when(jnp.logical_and(blk == 0, jnp.logical_not(is_prompt)))
    def _():
        for h in range(nh):
            st_ref[0, h] = s0_ref[0, h].T

    def sig(x):
        return 0.5 * jnp.tanh(0.5 * x) + 0.5

    lb = lb_ref[...]
    f = lb + (1.0 - lb) * sig(hf_ref[...])
    lf = jnp.log(f)
    mstack = mstack_ref[...]
    p1, p2, p3 = _split3(lf)
    r_ref[...] = (jnp.dot(mstack, p1, preferred_element_type=F32)
                  + jnp.dot(mstack, p2, preferred_element_type=F32)
                  + jnp.dot(mstack, p3, preferred_element_type=F32))

    k_ref[...] = 1.0 - f

    def head(h, carry):
        cs = pl.ds(pl.multiple_of(h * dk, dk), dk)
        vs = pl.ds(pl.multiple_of(h * dv, dv), dv)
        hq = hq_ref[:, cs]
        q = hq * sig(hq) * (dk ** -0.5)
        k = k_ref[:, cs]
        v = hi_ref[:, vs]
        vb = v.astype(BF16)
        b = r_ref[0:L, cs]
        st = st_ref[0, h]
        qe = (q * jnp.exp(b)).astype(BF16)
        o = lax.dot_general(qe, st.astype(BF16), (((1,), (1,)), ((), ())),
                            preferred_element_type=F32)
        a = jnp.zeros((L, L), F32)
        for li, m in enumerate(HGRN_LEVELS):
            if m in HGRN_MATMUL_LEVELS:
                mi = 1 + HGRN_MATMUL_LEVELS.index(m)
                rl = r_ref[mi * L:(mi + 1) * L, cs]
            else:
                rl = jnp.concatenate(
                    [jnp.broadcast_to(b[s0 + m - 1:s0 + m, :], (2 * m, dk)) for s0 in range(0, L, 2 * m)],
                    axis=0)
            e = jnp.exp(-jnp.abs(b - rl))
            al = lax.dot_general((q * e).astype(BF16), (k * e).astype(BF16),
                                 (((1,), (1,)), ((), ())), preferred_element_type=F32)
            a = a + al * masks_ref[li]
        diag = jnp.sum(q * k, axis=-1, keepdims=True)
        o = o + jnp.dot(a.astype(BF16), vb, preferred_element_type=F32) + diag * v
        o_ref[:, vs] = o
        b_last = b[L - 1:L, :]
        kd = (k * jnp.exp(b_last - b)).astype(BF16)
        st_ref[0, h] = st * jnp.exp(b_last) + lax.dot_general(
            vb, kd, (((0,), (0,)), ((), ())), preferred_element_type=F32)
        return carry

    lax.fori_loop(0, nh, head, 0, unroll=HGRN_HEAD_UNROLL)

    o = o_ref[...]
    ms = jnp.mean(o * o, axis=-1, keepdims=True)
    hg = hg_ref[...]
    og = o * lax.rsqrt(ms + RMS_EPS) * gn_ref[...] * (hg * sig(hg))
    og_ref[...] = og.astype(BF16)


def _hgrn(rows, z, s0, lb, g_hgrn):
    bs, nh, dk, dv = s0.shape
    hw = nh * dk
    assert nh * dv == hw and hw % LANES == 0
    mstack, masks = _hgrn_consts()
    n_lv = len(HGRN_LEVELS)
    L = CHUNK

    def zspec(sec):
        return pl.BlockSpec((L, hw), lambda g, sec=sec: (g, sec))

    og, st = pl.pallas_call(
        functools.partial(_hgrn_kernel, rows, nh, dk, dv),
        out_shape=(jax.ShapeDtypeStruct((rows.rows, hw), BF16),
                   jax.ShapeDtypeStruct((rows.nseq, nh, dv, dk), F32)),
        grid=(rows.groups,),
        in_specs=[zspec(0), zspec(1), zspec(2), zspec(3),
                  pl.BlockSpec((1, nh, dk, dv),
                               lambda g: (jnp.maximum(rows.seq_of_group(g) - rows.bp, 0), 0, 0, 0)),
                  pl.BlockSpec((1, hw), lambda g: (0, 0)),
                  pl.BlockSpec((1, hw), lambda g: (0, 0)),
                  pl.BlockSpec(mstack.shape, lambda g: (0, 0)),
                  pl.BlockSpec(masks.shape, lambda g: (0, 0, 0))],
        out_specs=(pl.BlockSpec((L, hw), lambda g: (g, 0)),
                   pl.BlockSpec((1, nh, dv, dk), lambda g: (rows.seq_of_group(g), 0, 0, 0))),
        scratch_shapes=[pltpu.VMEM(((len(HGRN_MATMUL_LEVELS) + 1) * L, hw), F32),
                        pltpu.VMEM((L, hw), F32),
                        pltpu.VMEM((L, hw), F32)],
        compiler_params=_cparams(1),
        name="hgrn2",
    )(z, z, z, z, s0, lb.reshape(1, hw), g_hgrn.reshape(1, hw), mstack, masks)
    return og, st


def _log_sigmoid(x):
    return jnp.minimum(x, 0.0) - jnp.log(1.0 + jnp.exp(-jnp.abs(x)))


def _mlstm_kernel(rows, nh, dqk, dv, mqk_ref, mv_ref, mo_ref, gates_ref, conv0_ref, c0_ref,
                  n0_ref, m0_ref, cw_ref, cb_ref, gbias_ref, gn_ref,
                  hm_ref, conv_ref, c_ref, n_ref, m_ref, ubuf_ref):
    L = CHUNK
    kq = nh * dqk
    pad = 8
    g = pl.program_id(0)
    blk = rows.blk_in_seq(g)
    is_prompt = g < rows.groups_p

    @pl.when(jnp.logical_and(blk == 0, is_prompt))
    def _():
        ubuf_ref[0:pad, :] = jnp.zeros((pad, 2 * kq), F32)
        c_ref[...] = jnp.zeros_like(c_ref)
        n_ref[...] = jnp.zeros_like(n_ref)
        m_ref[...] = jnp.zeros_like(m_ref)

    @pl.when(jnp.logical_and(blk == 0, jnp.logical_not(is_prompt)))
    def _():
        ubuf_ref[0:pad, :] = jnp.zeros((pad, 2 * kq), F32)
        ubuf_ref[pad - (CONV_K - 1):pad, :] = conv0_ref[0]
        c_ref[...] = c0_ref[...]
        n_ref[...] = n0_ref[...]
        m_ref[...] = m0_ref[...]

    ubuf_ref[pad:pad + L, :] = mqk_ref[...]
    acc = cb_ref[...] + jnp.zeros((L, 2 * kq), F32)
    for j in range(CONV_K):
        off = pad - (CONV_K - 1) + j
        acc = acc + ubuf_ref[off:off + L, :] * cw_ref[j:j + 1, :]
    new_tail = ubuf_ref[pad + L - (CONV_K - 1):pad + L, :]
    conv_ref[0] = new_tail
    ubuf_ref[pad - (CONV_K - 1):pad, :] = new_tail
    qk = _silu(acc)

    gt = gates_ref[...] + gbias_ref[...]
    lane = lax.broadcasted_iota(jnp.int32, gt.shape, 1)
    pg = jnp.where(lane < nh, gt, _log_sigmoid(gt))
    pgt = pg.T
    ti = lax.broadcasted_iota(jnp.int32, (L, L), 0)
    si = lax.broadcasted_iota(jnp.int32, (L, L), 1)
    tri = si <= ti

    for h in range(nh):
        q = qk[:, h * dqk:(h + 1) * dqk]
        k = qk[:, kq + h * dqk:kq + (h + 1) * dqk] * (dqk ** -0.5)
        v = mv_ref[:, h * dv:(h + 1) * dv]
        qb, kb, vb = q.astype(BF16), k.astype(BF16), v.astype(BF16)
        ig_c, lf_c = pg[:, h:h + 1], pg[:, nh + h:nh + h + 1]
        ig_r, lf_r = pgt[h:h + 1, :], pgt[nh + h:nh + h + 1, :]
        b_c = jnp.sum(jnp.where(tri, lf_r, 0.0), axis=1, keepdims=True)
        b_r = jnp.sum(jnp.where(ti <= si, lf_c, 0.0), axis=0, keepdims=True)
        m_prev = m_ref[0, :, h:h + 1]
        log_d = jnp.where(tri, b_c - b_r + ig_r, -jnp.inf)
        log_inter = b_c + m_prev
        m_t = jnp.maximum(jnp.max(log_d, axis=1, keepdims=True), log_inter)
        dm = jnp.exp(log_d - m_t)
        s_mat = lax.dot_general(qb, kb, (((1,), (1,)), ((), ())), preferred_element_type=F32) * dm
        w_inter = jnp.exp(log_inter - m_t)
        c_h = c_ref[0, h]
        n_h = n_ref[0, h:h + 1, :]
        num = (jnp.dot(s_mat.astype(BF16), vb, preferred_element_type=F32)
               + w_inter * jnp.dot(qb, c_h.astype(BF16), preferred_element_type=F32))
        den = (jnp.sum(s_mat, axis=1, keepdims=True)
               + w_inter * jnp.sum(q * n_h, axis=1, keepdims=True))
        hh = num / jnp.maximum(jnp.abs(den), jnp.exp(-m_t))
        b_last = b_c[L - 1:L, :]
        lw_c = b_last - b_c + ig_c
        lw_r = b_last - b_r + ig_r
        m_new = jnp.maximum(b_last + m_prev, jnp.max(lw_r, axis=1, keepdims=True))
        decay = jnp.exp(b_last + m_prev - m_new)
        kw = k * jnp.exp(lw_c - m_new)
        c_ref[0, h] = decay * c_h + lax.dot_general(
            kw.astype(BF16), vb, (((0,), (0,)), ((), ())), preferred_element_type=F32)
        n_ref[0, h:h + 1, :] = decay * n_h + jnp.sum(kw, axis=0, keepdims=True)
        m_ref[0, :, h:h + 1] = m_new
        ms = jnp.mean(hh * hh, axis=-1, keepdims=True)
        hn = hh * lax.rsqrt(ms + RMS_EPS) * gn_ref[:, h * dv:(h + 1) * dv]
        hm_ref[:, h * dv:(h + 1) * dv] = (hn * _sigmoid(mo_ref[:, h * dv:(h + 1) * dv])).astype(BF16)


def _mlstm(rows, z, sec_qk, gates, conv0, c0, n0, m0, conv_w, conv_b, b_igate, b_fgate, g_mlstm):
    bs, nh, dqk, dv = c0.shape
    kq = nh * dqk
    mw = nh * dv
    L = CHUNK
    assert 2 * kq == mw, "q/k conv width must equal the value width for the column sections"
    gbias = jnp.zeros((1, GATE_LANES), F32)
    gbias = gbias.at[0, 0:nh].set(b_igate.astype(F32)).at[0, nh:2 * nh].set(b_fgate.astype(F32))

    def sseq(g):
        return jnp.maximum(rows.seq_of_group(g) - rows.bp, 0)

    outs = pl.pallas_call(
        functools.partial(_mlstm_kernel, rows, nh, dqk, dv),
        out_shape=(jax.ShapeDtypeStruct((rows.rows, mw), BF16),
                   jax.ShapeDtypeStruct((rows.nseq, CONV_K - 1, 2 * kq), F32),
                   jax.ShapeDtypeStruct((rows.nseq, nh, dqk, dv), F32),
                   jax.ShapeDtypeStruct((rows.nseq, nh, dqk), F32),
                   jax.ShapeDtypeStruct((rows.nseq, 1, nh), F32)),
        grid=(rows.groups,),
        in_specs=[pl.BlockSpec((L, mw), lambda g: (g, sec_qk)),
                  pl.BlockSpec((L, mw), lambda g: (g, sec_qk + 1)),
                  pl.BlockSpec((L, mw), lambda g: (g, sec_qk + 2)),
                  pl.BlockSpec((L, GATE_LANES), lambda g: (g, 0)),
                  pl.BlockSpec((1, CONV_K - 1, 2 * kq), lambda g: (sseq(g), 0, 0)),
                  pl.BlockSpec((1, nh, dqk, dv), lambda g: (sseq(g), 0, 0, 0)),
                  pl.BlockSpec((1, nh, dqk), lambda g: (sseq(g), 0, 0)),
                  pl.BlockSpec((1, 1, nh), lambda g: (sseq(g), 0, 0)),
                  pl.BlockSpec((CONV_K, 2 * kq), lambda g: (0, 0)),
                  pl.BlockSpec((1, 2 * kq), lambda g: (0, 0)),
                  pl.BlockSpec((1, GATE_LANES), lambda g: (0, 0)),
                  pl.BlockSpec((1, mw), lambda g: (0, 0))],
        out_specs=(pl.BlockSpec((L, mw), lambda g: (g, 0)),
                   pl.BlockSpec((1, CONV_K - 1, 2 * kq), lambda g: (rows.seq_of_group(g), 0, 0)),
                   pl.BlockSpec((1, nh, dqk, dv), lambda g: (rows.seq_of_group(g), 0, 0, 0)),
                   pl.BlockSpec((1, nh, dqk), lambda g: (rows.seq_of_group(g), 0, 0)),
                   pl.BlockSpec((1, 1, nh), lambda g: (rows.seq_of_group(g), 0, 0))),
        scratch_shapes=[pltpu.VMEM((8 + L, 2 * kq), F32)],
        compiler_params=_cparams(1),
        name="mlstm",
    )(z, z, z, gates, conv0, c0, n0, m0.reshape(bs, 1, nh), conv_w, conv_b.reshape(1, 2 * kq),
      gbias, g_mlstm.reshape(1, mw))
    return outs


def _outproj_kernel(rows, tm, d, hw, og_ref, hm_ref, w_ref, xp_ref, xs_ref, mod_ref, o_ref, wb_ref):
    j, i = pl.program_id(0), pl.program_id(1)
    is_prompt = i < rows.groups_p * CHUNK // tm

    @pl.when(i == 0)
    def _():
        wb_ref[...] = w_ref[...].astype(BF16)

    mix = (jnp.dot(og_ref[...], wb_ref[0:hw, :], preferred_element_type=F32)
           + jnp.dot(hm_ref[...], wb_ref[hw:, :], preferred_element_type=F32))
    for gi in range(tm // CHUNK):
        seq = rows.seq_of_group(i * (tm // CHUNK) + gi)
        gate = mod_ref[pl.ds(seq, 1), :]
        rs = slice(gi * CHUNK, (gi + 1) * CHUNK)
        x = jnp.where(is_prompt, xp_ref[rs, :], xs_ref[rs, :])
        o_ref[rs, :] = x + gate * mix[rs, :]


def _outproj(rows, og, hm, w_out, x_p, x_s, mod, tm=1024, tn=512):
    d = x_p.shape[1]
    r = rows.rows
    hw = og.shape[1]
    tm, tn = min(tm, x_s.shape[0]), min(tn, d)
    assert r % tm == 0 and d % tn == 0
    xp_spec, xs_spec = _split_row_specs(rows, tm, tn, col_major=True)
    gate_blk0 = 2 * d // tn
    return pl.pallas_call(
        functools.partial(_outproj_kernel, rows, tm, d, hw),
        out_shape=jax.ShapeDtypeStruct((r, d), F32),
        grid=(d // tn, r // tm),
        in_specs=[pl.BlockSpec((tm, hw), lambda j, i: (i, 0)),
                  pl.BlockSpec((tm, hm.shape[1]), lambda j, i: (i, 0)),
                  pl.BlockSpec((w_out.shape[0], tn), lambda j, i: (0, j)),
                  xp_spec, xs_spec,
                  pl.BlockSpec((mod.shape[0], tn), lambda j, i: (0, gate_blk0 + j))],
        out_specs=pl.BlockSpec((tm, tn), lambda j, i: (i, j)),
        scratch_shapes=[pltpu.VMEM((w_out.shape[0], tn), BF16)],
        compiler_params=_cparams(2),
        name="out_proj",
    )(og, hm, w_out, x_p, x_s, mod)


def _router_kernel(rows, tm, d, ne, x_ref, g_ref, mod_ref, wr_ref, br_ref, ut_ref,
                   h2_ref, e_ref, gate_ref, rank_ref, cnt_ref, h2s_ref):
    i = pl.program_id(0)
    half = d // 2

    @pl.when(i == 0)
    def _():
        cnt_ref[...] = jnp.zeros_like(cnt_ref)

    g = g_ref[...]
    for gi in range(tm // CHUNK):
        seq = rows.seq_of_group(i * (tm // CHUNK) + gi)
        m = mod_ref[pl.ds(seq, 1), :]
        shift, scale = m[:, 3 * d:4 * d], m[:, 4 * d:5 * d]
        x = x_ref[gi * CHUNK:(gi + 1) * CHUNK, :]
        ms = jnp.mean(x * x, axis=-1, keepdims=True)
        h2s_ref[gi * CHUNK:(gi + 1) * CHUNK, :] = (x * lax.rsqrt(ms + RMS_EPS) * g) * (1.0 + scale) + shift
    h2 = h2s_ref[...]
    lo = pltpu.bitcast(h2[:, :half].astype(BF16).astype(F32), jnp.uint32)
    hi = pltpu.bitcast(h2[:, half:].astype(BF16).astype(F32), jnp.uint32)
    h2_ref[...] = (hi & jnp.uint32(0xFFFF0000)) | (lo >> 16)

    logits = _dot_f32x3(h2, wr_ref[...])
    lt = logits.T[0:ne, :] + br_ref[...]
    eidx = lax.broadcasted_iota(jnp.int32, (ne, tm), 0)
    cur = lt
    tops, sels, hots = [], [], []
    for _ in range(TOP_K):
        mx = jnp.max(cur, axis=0, keepdims=True)
        sel = jnp.min(jnp.where(cur == mx, eidx, ne), axis=0, keepdims=True)
        hot = eidx == sel
        tops.append(mx)
        sels.append(sel)
        hots.append(hot)
        cur = jnp.where(hot, -jnp.inf, cur)
    ex = [jnp.exp(t - tops[0]) for t in tops]
    tot = ex[0] + ex[1] + ex[2] + ex[3]
    oh = jnp.zeros((ne, tm), F32)
    for hot in hots:
        oh = oh + hot.astype(F32)
    prefix = jnp.dot(oh.astype(BF16), ut_ref[...], preferred_element_type=F32)
    base = cnt_ref[:, 0:1]
    pos = base + prefix
    for kk in range(TOP_K):
        e_ref[kk:kk + 1, :] = sels[kk]
        gate_ref[kk:kk + 1, :] = ex[kk] / tot
        rank_ref[kk:kk + 1, :] = jnp.sum(jnp.where(hots[kk], pos, 0.0), axis=0,
                                         keepdims=True).astype(jnp.int32)
    cnt_ref[...] = cnt_ref[...] + jnp.sum(oh, axis=1, keepdims=True)


def _router(rows, x1, g_ffn, mod, w_router, b_router, tm=256):
    r, d = x1.shape
    ne = w_router.shape[1]
    tm = min(tm, r)
    assert r % tm == 0 and ne % 8 == 0 and ne <= LANES
    wr_pad = jnp.zeros((d, LANES), F32).at[:, :ne].set(w_router.astype(F32))
    tt = np.arange(tm)
    ut = jnp.asarray((tt[:, None] < tt[None, :]).astype(np.float32), BF16)
    return pl.pallas_call(
        functools.partial(_router_kernel, rows, tm, d, ne),
        out_shape=(jax.ShapeDtypeStruct((r, d // 2), jnp.uint32),
                   jax.ShapeDtypeStruct((TOP_K, r), jnp.int32),
                   jax.ShapeDtypeStruct((TOP_K, r), F32),
                   jax.ShapeDtypeStruct((TOP_K, r), jnp.int32),
                   jax.ShapeDtypeStruct((ne, LANES), F32)),
        grid=(r // tm,),
        in_specs=[pl.BlockSpec((tm, d), lambda i: (i, 0)),
                  pl.BlockSpec((1, d), lambda i: (0, 0)),
                  pl.BlockSpec(mod.shape, lambda i: (0, 0)),
                  pl.BlockSpec((d, LANES), lambda i: (0, 0)),
                  pl.BlockSpec((ne, 1), lambda i: (0, 0)),
                  pl.BlockSpec((tm, tm), lambda i: (0, 0))],
        out_specs=(pl.BlockSpec((tm, d // 2), lambda i: (i, 0)),
                   pl.BlockSpec((TOP_K, tm), lambda i: (0, i)),
                   pl.BlockSpec((TOP_K, tm), lambda i: (0, i)),
                   pl.BlockSpec((TOP_K, tm), lambda i: (0, i)),
                   pl.BlockSpec((ne, LANES), lambda i: (0, 0))),
        scratch_shapes=[pltpu.VMEM((tm, d), F32)],
        compiler_params=_cparams(1),
        name="router",
    )(x1, g_ffn.reshape(1, d), mod, wr_pad, b_router.reshape(ne, 1).astype(F32), ut)


def _invert_kernel(tm, pstart_ref, e_ref, rank_ref, rt_ref):
    i = pl.program_id(0)

    @pl.when(i == 0)
    def _():
        def clear(r, carry):
            rt_ref[r] = 0
            return carry

        lax.fori_loop(0, rt_ref.shape[0], clear, 0, unroll=8)

    def body(t, carry):
        for kk in range(TOP_K):
            rt_ref[pstart_ref[e_ref[kk, t]] + rank_ref[kk, t]] = i * tm + t
        return carry

    lax.fori_loop(0, tm, body, 0, unroll=4)


def _invert(e_idx, rank, pstart, n_rows, tm=256):
    r = e_idx.shape[1]
    tm = min(tm, r)
    assert r % tm == 0
    return pl.pallas_call(
        functools.partial(_invert_kernel, tm),
        out_shape=jax.ShapeDtypeStruct((n_rows,), jnp.int32),
        grid_spec=pltpu.PrefetchScalarGridSpec(
            num_scalar_prefetch=1,
            grid=(r // tm,),
            in_specs=[pl.BlockSpec((TOP_K, tm), lambda i, *_: (0, i), memory_space=pltpu.SMEM),
                      pl.BlockSpec((TOP_K, tm), lambda i, *_: (0, i), memory_space=pltpu.SMEM)],
            out_specs=pl.BlockSpec(memory_space=pltpu.SMEM)),
        compiler_params=_cparams(1),
        name="moe_invert",
    )(pstart, e_idx, rank)


def _unpack_rows(xw):
    lo = pltpu.bitcast(xw << 16, F32).astype(BF16)
    hi = pltpu.bitcast(xw & jnp.uint32(0xFFFF0000), F32).astype(BF16)
    return lo, hi


def _gather_kernel(nused_ref, rt_cur_ref, rt_next_ref, h2_ref, xs_ref, buf_ref, sem):
    b = pl.program_id(0)
    n_used = nused_ref[0]
    half = buf_ref.shape[2]

    def row_copy(slot, src_row, dst_row):
        return pltpu.make_async_copy(h2_ref.at[pl.ds(src_row, 1), :],
                                     buf_ref.at[slot, pl.ds(dst_row, 1), :], sem.at[slot])

    def issue(rt_ref, slot):
        def pair(t2, carry):
            for p in range(2):
                t = 2 * t2 + p
                row_copy(slot, rt_ref[0, 0, t], t).start(priority=p)
            return carry

        lax.fori_loop(0, MOE_BLOCK // 2, pair, 0, unroll=4)

    @pl.when(jnp.logical_and(b == 0, n_used > 0))
    def _():
        issue(rt_cur_ref, 0)

    @pl.when(b + 1 < n_used)
    def _():
        issue(rt_next_ref, (b + 1) % 2)

    @pl.when(b < n_used)
    def _():
        slot = b % 2
        pltpu.make_async_copy(buf_ref.at[slot], buf_ref.at[slot], sem.at[slot]).wait()
        lo, hi = _unpack_rows(buf_ref[slot])
        xs_ref[:, 0:half] = lo
        xs_ref[:, half:] = hi

    @pl.when(b >= n_used)
    def _():
        xs_ref[...] = jnp.zeros_like(xs_ref)


def _gather_rows(h2p, row_tok, n_used):
    r, half = h2p.shape
    n_blocks = row_tok.shape[0] // MOE_BLOCK
    rt3 = row_tok.reshape(n_blocks, 1, MOE_BLOCK)
    return pl.pallas_call(
        _gather_kernel,
        out_shape=jax.ShapeDtypeStruct((n_blocks * MOE_BLOCK, 2 * half), BF16),
        grid_spec=pltpu.PrefetchScalarGridSpec(
            num_scalar_prefetch=1,
            grid=(n_blocks,),
            in_specs=[pl.BlockSpec((1, 1, MOE_BLOCK), lambda b, *_: (b, 0, 0), memory_space=pltpu.SMEM),
                      pl.BlockSpec((1, 1, MOE_BLOCK), lambda b, *_: (jnp.minimum(b + 1, n_blocks - 1), 0, 0),
                                   memory_space=pltpu.SMEM),
                      pl.BlockSpec(memory_space=pl.ANY)],
            out_specs=pl.BlockSpec((MOE_BLOCK, 2 * half), lambda b, *_: (b, 0)),
            scratch_shapes=[pltpu.VMEM((2, MOE_BLOCK, half), jnp.uint32),
                            pltpu.SemaphoreType.DMA((2,))]),
        compiler_params=_cparams(1),
        name="moe_gather",
    )(n_used.reshape(1), rt3, rt3, h2p)


MOE_CACHE_BLOCKS = 4


class _ExpertRows:
    def __init__(self, src_hbm, slots_ref, sems, b0, nb, first_tile):
        self.src, self.slots, self.sems = src_hbm, slots_ref, sems
        self.b0, self.nb, self.first_tile = b0, nb, first_tile

    def slot_of(self, i):
        return jnp.where(i < MOE_CACHE_BLOCKS, i, MOE_CACHE_BLOCKS + i % 2)

    def _needs_load(self, i):
        return jnp.logical_or(self.first_tile, i >= MOE_CACHE_BLOCKS)

    def _copy(self, i):
        row0 = pl.multiple_of((self.b0 + i) * MOE_BLOCK, MOE_BLOCK)
        return pltpu.make_async_copy(self.src.at[pl.ds(row0, MOE_BLOCK), :],
                                     self.slots.at[self.slot_of(i)], self.sems.at[i % 2])

    def request(self, i):
        @pl.when(jnp.logical_and(i < self.nb, self._needs_load(i)))
        def _():
            self._copy(i).start()

    def arrive(self, i):
        @pl.when(self._needs_load(i))
        def _():
            self._copy(i).wait()

    def request_first(self, e):
        @pl.when(e == 0)
        def _():
            self.request(0)

    def request_next_expert(self, e, ne, last_tile, bstart_ref, nblk_ref):
        @pl.when(jnp.logical_and(last_tile, e + 1 < ne))
        def _():
            @pl.when(nblk_ref[e + 1] > 0)
            def _():
                row0 = pl.multiple_of(bstart_ref[e + 1] * MOE_BLOCK, MOE_BLOCK)
                pltpu.make_async_copy(self.src.at[pl.ds(row0, MOE_BLOCK), :],
                                      self.slots.at[0], self.sems.at[0]).start()


def _block_window(dst_hbm, blk, col0, width):
    row0 = pl.multiple_of(blk * MOE_BLOCK, MOE_BLOCK)
    return dst_hbm.at[pl.ds(row0, MOE_BLOCK), pl.ds(pl.multiple_of(col0, width), width)]


def _zero_tail_blocks(stage_ref, dst_hbm, sem, first_blk, col0, width):
    n_blocks = dst_hbm.shape[0] // MOE_BLOCK
    stage_ref[0] = jnp.zeros(stage_ref.shape[1:], stage_ref.dtype)

    def cp(blk):
        return pltpu.make_async_copy(stage_ref.at[0], _block_window(dst_hbm, blk, col0, width), sem.at[0])

    def start(blk, carry):
        cp(blk).start()
        return carry

    def wait(blk, carry):
        cp(blk).wait()
        return carry

    lax.fori_loop(first_blk, n_blocks, start, 0)
    lax.fori_loop(first_blk, n_blocks, wait, 0)


class _OutRing:
    def __init__(self, stage_ref, sems, cnt_ref, dst_hbm, width):
        self.stage, self.sems, self.cnt, self.dst, self.width = stage_ref, sems, cnt_ref, dst_hbm, width

    def init(self):
        self.cnt[0] = 0
        self.cnt[1] = 0

    def _await_next(self, limit):
        w = self.cnt[1]

        @pl.when(w < limit)
        def _():
            pltpu.make_async_copy(self.stage.at[w % 2], _block_window(self.dst, 0, 0, self.width),
                                  self.sems.at[w % 2]).wait()
            self.cnt[1] = w + 1

    def reserve(self):
        self._await_next(self.cnt[0] - 1)

    def push(self, value, blk, col0):
        c = self.cnt[0]
        self.stage[c % 2] = value
        pltpu.make_async_copy(self.stage.at[c % 2], _block_window(self.dst, blk, col0, self.width),
                              self.sems.at[c % 2]).start()
        self.cnt[0] = c + 1

    def drain(self):
        for _ in range(2):
            self._await_next(self.cnt[0])


MOE_CAST_CHUNKS = 4


def _cast_chunk(w_ref, wbf_ref, c):
    kc = wbf_ref.shape[0] // MOE_CAST_CHUNKS
    wbf_ref[c * kc:(c + 1) * kc, :] = w_ref[0, c * kc:(c + 1) * kc, :].astype(BF16)


def _first_block_dot(x_blk_ref, w_ref, wbf_ref):
    kc = wbf_ref.shape[0] // MOE_CAST_CHUNKS
    acc = None
    for c in range(MOE_CAST_CHUNKS):
        if c + 1 < MOE_CAST_CHUNKS:
            _cast_chunk(w_ref, wbf_ref, c + 1)
        part = jnp.dot(x_blk_ref[:, c * kc:(c + 1) * kc], wbf_ref[c * kc:(c + 1) * kc, :],
                       preferred_element_type=F32)
        acc = part if acc is None else acc + part
    return acc


def _moe_up_kernel(tf, nj, ne, bstart_ref, nblk_ref, xs_hbm, w_ref, b_ref, act_hbm,
                   wbf_ref, xb_ref, gt_ref, ostage_ref, gu_carry_ref, cnt_ref, xsem, osem):
    g = pl.program_id(0)
    e, j = g // nj, g % nj
    nb, b0 = nblk_ref[e], bstart_ref[e]
    nslab = MOE_BLOCK // LANES
    xrows = _ExpertRows(xs_hbm, xb_ref, xsem, b0, nb, j == 0)
    ring = _OutRing(ostage_ref, osem, cnt_ref, act_hbm, tf)
    PEND, PEND_BLK, PEND_COL = 2, 3, 4

    @pl.when(g == 0)
    def _():
        ring.init()
        cnt_ref[PEND] = 0
        gu_carry_ref[...] = jnp.zeros_like(gu_carry_ref)

    def up_dot(i):
        return jnp.dot(xb_ref[xrows.slot_of(i)], wbf_ref[...], preferred_element_type=F32) + b_ref[0]

    def swiglu(gu):
        gut = gu.T
        for sl in range(nslab):
            gt_ref[sl] = gut[:, sl * LANES:(sl + 1) * LANES]
        parts = []
        for sl in range(nslab):
            gate = jnp.minimum(gt_ref[sl, pl.ds(0, tf, stride=2), :], SWIGLU_LIMIT)
            up = jnp.clip(gt_ref[sl, pl.ds(1, tf, stride=2), :], -SWIGLU_LIMIT, SWIGLU_LIMIT)
            parts.append((up + 1.0) * gate * _sigmoid(SWIGLU_ALPHA * gate))
        return jnp.concatenate(parts, axis=1).T.astype(BF16)

    @pl.when(nb > 0)
    def _():
        xrows.request_first(e)
        _cast_chunk(w_ref, wbf_ref, 0)
        xrows.arrive(0)
        xrows.request(1)
        ring.reserve()
        owed = cnt_ref[PEND] == 1
        gu0 = _first_block_dot(xb_ref.at[0], w_ref, wbf_ref) + b_ref[0]
        act_owed = swiglu(gu_carry_ref[...])

        @pl.when(owed)
        def _():
            ring.push(act_owed, cnt_ref[PEND_BLK], cnt_ref[PEND_COL])
            cnt_ref[PEND] = 0

        def body(i, gu_prev):
            xrows.arrive(i)
            xrows.request(i + 1)
            ring.reserve()
            gu = up_dot(i)
            ring.push(swiglu(gu_prev), b0 + i - 1, j * tf)
            return gu

        gu_last = lax.fori_loop(1, nb, body, gu0)

        @pl.when(e < ne - 1)
        def _():
            gu_carry_ref[...] = gu_last
            cnt_ref[PEND] = 1
            cnt_ref[PEND_BLK] = b0 + nb - 1
            cnt_ref[PEND_COL] = j * tf

        @pl.when(e == ne - 1)
        def _():
            ring.reserve()
            ring.push(swiglu(gu_last), b0 + nb - 1, j * tf)

    xrows.request_next_expert(e, ne, j == nj - 1, bstart_ref, nblk_ref)

    @pl.when(e == ne - 1)
    def _():
        @pl.when(cnt_ref[PEND] == 1)
        def _():
            ring.reserve()
            ring.push(swiglu(gu_carry_ref[...]), cnt_ref[PEND_BLK], cnt_ref[PEND_COL])
            cnt_ref[PEND] = 0

        ring.drain()
        _zero_tail_blocks(ostage_ref, act_hbm, osem, b0 + nb, j * tf, tf)


def _moe_up(xs, w_gu, b_gu, bstart, nblk, tf=512):
    n_rows, d = xs.shape
    ne, _, f2 = w_gu.shape
    f = f2 // 2
    tf = min(tf, f)
    assert f % tf == 0 and w_gu.shape[1] == d
    nj = f // tf
    return pl.pallas_call(
        functools.partial(_moe_up_kernel, tf, nj, ne),
        out_shape=jax.ShapeDtypeStruct((n_rows, f), BF16),
        grid_spec=pltpu.PrefetchScalarGridSpec(
            num_scalar_prefetch=2,
            grid=(ne * nj,),
            in_specs=[pl.BlockSpec(memory_space=pl.ANY),
                      pl.BlockSpec((1, d, 2 * tf), lambda g, *_: (g // nj, 0, g % nj)),
                      pl.BlockSpec((1, 1, 2 * tf), lambda g, *_: (g // nj, 0, g % nj))],
            out_specs=pl.BlockSpec(memory_space=pl.ANY),
            scratch_shapes=[pltpu.VMEM((d, 2 * tf), BF16),
                            pltpu.VMEM((MOE_CACHE_BLOCKS + 2, MOE_BLOCK, d), BF16),
                            pltpu.VMEM((MOE_BLOCK // LANES, 2 * tf, LANES), F32),
                            pltpu.VMEM((2, MOE_BLOCK, tf), BF16),
                            pltpu.VMEM((MOE_BLOCK, 2 * tf), F32),
                            pltpu.SMEM((8,), jnp.int32),
                            pltpu.SemaphoreType.DMA((2,)),
                            pltpu.SemaphoreType.DMA((2,))]),
        compiler_params=_cparams(1),
        name="moe_up",
    )(bstart, nblk, xs, w_gu, b_gu.reshape(ne, 1, f2))


def _moe_down_kernel(tn, nj, ne, bstart_ref, nblk_ref, act_hbm, w_ref, b_ref, yb_hbm,
                     wbf_ref, ab_ref, ostage_ref, cnt_ref, asem, osem):
    g = pl.program_id(0)
    e, j = g // nj, g % nj
    nb, b0 = nblk_ref[e], bstart_ref[e]
    arows = _ExpertRows(act_hbm, ab_ref, asem, b0, nb, j == 0)
    ring = _OutRing(ostage_ref, osem, cnt_ref, yb_hbm, tn)

    @pl.when(g == 0)
    def _():
        ring.init()

    @pl.when(nb > 0)
    def _():
        arows.request_first(e)
        _cast_chunk(w_ref, wbf_ref, 0)
        npairs = jnp.minimum((nb - 1) // 2, MOE_CACHE_BLOCKS // 2)
        arows.arrive(0)
        arows.request(1)

        @pl.when(npairs > 0)
        def _():
            arows.request(2)

        ring.reserve()
        ring.push(_first_block_dot(ab_ref.at[0], w_ref, wbf_ref) + b_ref[0], b0, j * tn)

        def pair_body(p, carry):
            i = 1 + 2 * p
            arows.arrive(i)
            arows.arrive(i + 1)
            arows.request(i + 2)

            @pl.when(p + 1 < npairs)
            def _():
                arows.request(i + 3)

            ring.reserve()
            lhs = ab_ref[pl.ds(i, 2)].reshape(2 * MOE_BLOCK, ab_ref.shape[2])
            y = jnp.dot(lhs, wbf_ref[...], preferred_element_type=F32) + b_ref[0]
            ring.push(y[:MOE_BLOCK], b0 + i, j * tn)
            ring.reserve()
            ring.push(y[MOE_BLOCK:], b0 + i + 1, j * tn)
            return carry

        lax.fori_loop(0, npairs, pair_body, 0)

        def body(i, carry):
            arows.arrive(i)
            arows.request(i + 1)
            ring.reserve()
            y = jnp.dot(ab_ref[arows.slot_of(i)], wbf_ref[...], preferred_element_type=F32) + b_ref[0]
            ring.push(y, b0 + i, j * tn)
            return carry

        lax.fori_loop(1 + 2 * npairs, nb, body, 0)

    arows.request_next_expert(e, ne, j == nj - 1, bstart_ref, nblk_ref)

    @pl.when(e == ne - 1)
    def _():
        ring.drain()
        _zero_tail_blocks(ostage_ref, yb_hbm, osem, b0 + nb, j * tn, tn)


def _moe_down(act, w_d, b_d, bstart, nblk, tn=1024):
    n_rows, f = act.shape
    ne, _, d = w_d.shape
    tn = min(tn, d)
    assert d % tn == 0
    nj = d // tn
    return pl.pallas_call(
        functools.partial(_moe_down_kernel, tn, nj, ne),
        out_shape=jax.ShapeDtypeStruct((n_rows, d), F32),
        grid_spec=pltpu.PrefetchScalarGridSpec(
            num_scalar_prefetch=2,
            grid=(ne * nj,),
            in_specs=[pl.BlockSpec(memory_space=pl.ANY),
                      pl.BlockSpec((1, f, tn), lambda g, *_: (g // nj, 0, g % nj)),
                      pl.BlockSpec((1, 1, tn), lambda g, *_: (g // nj, 0, g % nj))],
            out_specs=pl.BlockSpec(memory_space=pl.ANY),
            scratch_shapes=[pltpu.VMEM((f, tn), BF16),
                            pltpu.VMEM((MOE_CACHE_BLOCKS + 2, MOE_BLOCK, f), BF16),
                            pltpu.VMEM((2, MOE_BLOCK, tn), F32),
                            pltpu.SMEM((2,), jnp.int32),
                            pltpu.SemaphoreType.DMA((2,)),
                            pltpu.SemaphoreType.DMA((2,))]),
        compiler_params=_cparams(1),
        name="moe_down",
    )(bstart, nblk, act, w_d, b_d.reshape(ne, 1, d))


def _combine_kernel(rows, tm, d, n_tiles, pstart_ref, e_ref, rank_ref, en_ref, rankn_ref, gt_ref,
                    x_ref, mod_ref, gf_ref, yb_ref, yp_ref, ys_ref, buf_ref, sem):
    i = pl.program_id(0)
    is_prompt = i < rows.groups_p * CHUNK // tm
    slot = i % 2

    def row_copy(s, dst_k, dst_t, src_row):
        return pltpu.make_async_copy(yb_ref.at[pl.ds(src_row, 1), :],
                                     buf_ref.at[s, dst_k, pl.ds(dst_t, 1), :], sem.at[s])

    def issue(eref, rref, s):
        def one(t, carry):
            for kk in range(TOP_K):
                src = pstart_ref[eref[kk, t]] + rref[kk, t]
                row_copy(s, kk, t, src).start(priority=kk % 2)
            return carry

        lax.fori_loop(0, tm, one, 0, unroll=2)

    @pl.when(i == 0)
    def _():
        issue(e_ref, rank_ref, 0)

    @pl.when(i + 1 < n_tiles)
    def _():
        issue(en_ref, rankn_ref, (i + 1) % 2)

    pltpu.make_async_copy(buf_ref.at[slot], buf_ref.at[slot], sem.at[slot]).wait()

    gf = gf_ref[...]
    for gi in range(tm // CHUNK):
        rs = slice(gi * CHUNK, (gi + 1) * CHUNK)
        seq = rows.seq_of_group(i * (tm // CHUNK) + gi)
        gate2 = mod_ref[pl.ds(seq, 1), :]
        ff = jnp.zeros((CHUNK, d), F32)
        for kk in range(TOP_K):
            ff = ff + buf_ref[slot, kk, rs, :] * gt_ref[rs, kk:kk + 1]
        x2 = x_ref[rs, :] + gate2 * ff
        ms = jnp.mean(x2 * x2, axis=-1, keepdims=True)
        y = x2 * lax.rsqrt(ms + RMS_EPS) * gf

        @pl.when(is_prompt)
        def _():
            yp_ref[rs, :] = y

        @pl.when(jnp.logical_not(is_prompt))
        def _():
            ys_ref[rs, :] = y


def _combine(rows, x1, mod, g_final, yb, e_idx, rank, gates_t, pstart, tm=128):
    r, d = x1.shape
    tm = min(tm, r)
    assert r % tm == 0
    gate2_blk = 5
    yp_spec, ys_spec = _split_row_specs(rows, tm, d)
    n_tiles = r // tm

    def nxt(i, *_):
        return (0, jnp.minimum(i + 1, n_tiles - 1))

    return pl.pallas_call(
        functools.partial(_combine_kernel, rows, tm, d, n_tiles),
        out_shape=(jax.ShapeDtypeStruct((rows.groups_p * CHUNK, d), F32),
                   jax.ShapeDtypeStruct((r - rows.groups_p * CHUNK, d), F32)),
        grid_spec=pltpu.PrefetchScalarGridSpec(
            num_scalar_prefetch=1,
            grid=(n_tiles,),
            in_specs=[pl.BlockSpec((TOP_K, tm), lambda i, *_: (0, i), memory_space=pltpu.SMEM),
                      pl.BlockSpec((TOP_K, tm), lambda i, *_: (0, i), memory_space=pltpu.SMEM),
                      pl.BlockSpec((TOP_K, tm), nxt, memory_space=pltpu.SMEM),
                      pl.BlockSpec((TOP_K, tm), nxt, memory_space=pltpu.SMEM),
                      pl.BlockSpec((tm, TOP_K), lambda i, *_: (i, 0)),
                      pl.BlockSpec((tm, d), lambda i, *_: (i, 0)),
                      pl.BlockSpec((mod.shape[0], d), lambda i, *_: (0, gate2_blk)),
                      pl.BlockSpec((1, d), lambda i, *_: (0, 0)),
                      pl.BlockSpec(memory_space=pl.ANY)],
            out_specs=(yp_spec, ys_spec),
            scratch_shapes=[pltpu.VMEM((2, TOP_K, tm, d), F32),
                            pltpu.SemaphoreType.DMA((2,))]),
        compiler_params=_cparams(1),
        name="moe_combine",
    )(pstart, e_idx, rank, e_idx, rank, gates_t, x1, mod, g_final.reshape(1, d), yb)


def kernel(x_prompt, x_sample, c_prompt, c_sample, state_hgrn_S, state_conv, state_mlstm_C,
           state_mlstm_n, state_mlstm_m, w_ada, b_ada, g_mix, g_ffn, w_in, lb_logits, conv_w,
           conv_b, b_igate, b_fgate, g_hgrn, g_mlstm, w_out, w_router, b_router, w_gate_up,
           b_gate_up, w_down, b_down, g_final):
    depth = w_ada.shape[0]
    assert depth == 1, "single-layer trunk"
    bp, tp, d = x_prompt.shape
    bs, ts, _ = x_sample.shape
    rows = _Rows(bp, tp, bs, ts)
    _, _, nh_h, dk, dv = state_hgrn_S.shape
    _, _, nh_m, dqk, dvm = state_mlstm_C.shape
    hw = nh_h * dk
    mw = nh_m * dvm
    n_main = 4 * hw + 3 * mw
    assert w_in.shape[2] == n_main + 2 * nh_m and 2 * nh_m <= GATE_LANES
    assert mw == hw, "column sections of the input projection are addressed in hw-wide blocks"
    ne = w_router.shape[2]

    lb = jax.nn.softmax(lb_logits.astype(F32), axis=0)[0]

    x_p = x_prompt.reshape(bp * tp, d)
    x_s = x_sample.reshape(bs * ts, d)
    n_c = bp + bs
    n_c_pad = -(-n_c // 8) * 8
    c_pad = jnp.zeros((n_c_pad, d), F32).at[:n_c].set(jnp.concatenate([c_prompt, c_sample], axis=0))
    mod = _ada(c_pad, w_ada[0], b_ada[0])

    w_in_t = jnp.swapaxes(w_in, 1, 2)
    h_all, gates = _prep(rows, x_p, x_s, g_mix[0], mod, w_in_t, n_main)
    z = _inproj(h_all, w_in_t, n_main)

    og, st = _hgrn(rows, z, state_hgrn_S[0], lb, g_hgrn[0])
    hm, conv_new, c_new, n_new, m_new = _mlstm(
        rows, z, 4, gates, state_conv[0], state_mlstm_C[0], state_mlstm_n[0], state_mlstm_m[0],
        conv_w[0], conv_b[0], b_igate[0], b_fgate[0], g_mlstm[0])

    x1 = _outproj(rows, og, hm, w_out[0], x_p, x_s, mod)

    h2p, e_idx, gate_k, rank, cnt = _router(rows, x1, g_ffn[0], mod, w_router[0], b_router[0])

    counts = cnt[:, 0].astype(jnp.int32)
    nblk_e = (counts + MOE_BLOCK - 1) // MOE_BLOCK
    padded = nblk_e * MOE_BLOCK
    pad_end = jnp.cumsum(padded)
    pstart = jnp.concatenate([jnp.zeros((1,), jnp.int32), pad_end]).astype(jnp.int32)
    n_blocks_max = -(-(rows.rows * TOP_K) // MOE_BLOCK) + ne
    n_rows = n_blocks_max * MOE_BLOCK
    blk_start_e = (pstart[:ne] // MOE_BLOCK).astype(jnp.int32)
    n_used = (pstart[ne] // MOE_BLOCK).astype(jnp.int32)

    row_tok = _invert(e_idx, rank, pstart, n_rows)
    xs = _gather_rows(h2p, row_tok, n_used)
    act = _moe_up(xs, w_gate_up[0], b_gate_up[0], blk_start_e, nblk_e.astype(jnp.int32))
    yb = _moe_down(act, w_down[0], b_down[0], blk_start_e, nblk_e.astype(jnp.int32))

    y_p, y_s = _combine(rows, x1, mod, g_final, yb, e_idx, rank, gate_k.T, pstart)

    y_prompt = y_p.reshape(bp, tp, d)
    y_sample = y_s.reshape(bs, ts, d)
    s_all = jnp.swapaxes(st, 2, 3)
    m_all = m_new.reshape(rows.nseq, nh_m)

    def split(a):
        return a[:bp][None], a[bp:][None]

    p_s, s_s = split(s_all)
    p_conv, s_conv = split(conv_new)
    p_c, s_c = split(c_new)
    p_n, s_n = split(n_new)
    p_m, s_m = split(m_all)
    return (y_prompt, y_sample, p_s, p_conv, p_c, p_n, p_m, s_s, s_conv, s_c, s_n, s_m)
```
